```python
import math
import jax, jax.numpy as jnp
from jax import lax
import numpy as np

D_MODEL = 1024
BATCH = 4
SEQ = 8192
DEPTH = 1

MLA_HEADS = 8
MLA_NOPE = 64
MLA_ROPE = 32
MLA_V = 64
Q_LORA = 256
KV_LORA = 128
D_MLA = MLA_HEADS * MLA_V
MLA_SCALE = 1.0 / math.sqrt(MLA_NOPE + MLA_ROPE)
ROPE_BASE = 10000.0

SWA_HEADS = 8
SWA_KV_HEADS = 2
SWA_HEAD_DIM = 64
SWA_GROUP = SWA_HEADS // SWA_KV_HEADS
D_SWA = SWA_HEADS * SWA_HEAD_DIM
WINDOW = 128
BLOCK = 128
SWA_SCALE = 1.0 / math.sqrt(SWA_HEAD_DIM)

N_BUCKETS = 32
MAX_DISTANCE = 128

D_MIX = D_MLA + D_SWA

ALPHA = (2.0 * DEPTH) ** 0.25
BETA = (8.0 * DEPTH) ** -0.25

IN_SPLITS = (Q_LORA, KV_LORA, MLA_ROPE, D_MLA,
             SWA_HEADS * SWA_HEAD_DIM, SWA_KV_HEADS * SWA_HEAD_DIM,
             SWA_KV_HEADS * SWA_HEAD_DIM, D_SWA)
D_IN = int(sum(IN_SPLITS))
IN_OFFSETS = tuple(int(o) for o in np.cumsum(IN_SPLITS)[:-1])

kernel_name = "hybrid_mla_swa_deepnorm_block"


def _rms_norm(x, g, eps=1e-6):
    xf = x.astype(jnp.float32)
    y = xf * lax.rsqrt(jnp.mean(xf * xf, axis=-1, keepdims=True) + eps)
    return y.astype(x.dtype) * g


def _layer_norm(x, g, b, eps=1e-5):
    xf = x.astype(jnp.float32)
    mu = jnp.mean(xf, axis=-1, keepdims=True)
    var = jnp.mean(jnp.square(xf - mu), axis=-1, keepdims=True)
    return ((xf - mu) * lax.rsqrt(var + eps)).astype(x.dtype) * g + b


def _rope(x, cos, sin):
    half = x.shape[-1] // 2
    x1, x2 = x[..., :half], x[..., half:]
    cos = cos.astype(x.dtype)
    sin = sin.astype(x.dtype)
    return jnp.concatenate([x1 * cos - x2 * sin, x2 * cos + x1 * sin], axis=-1)


def _t5_bucket(rel):
    half = N_BUCKETS // 2
    ret = np.where(rel > 0, half, 0)
    n = np.abs(rel)
    max_exact = half // 2
    large = max_exact + (np.log(np.maximum(n, 1).astype(np.float32) / max_exact)
                         / np.log(MAX_DISTANCE / max_exact) * (half - max_exact)).astype(np.int32)
    large = np.minimum(large, half - 1)
    return (ret + np.where(n < max_exact, n, large)).astype(np.int32)


def _mla(c_q, c_kv, k_rope, g_q, g_kv, w_uq, w_ukv, cos, sin):
    B, S, _ = c_q.shape
    nb = S // BLOCK
    q = (_rms_norm(c_q, g_q) @ w_uq).reshape(B, S, MLA_HEADS, MLA_NOPE + MLA_ROPE)
    q_nope = q[..., :MLA_NOPE]
    q_rope = _rope(q[..., MLA_NOPE:], cos[:, None, :], sin[:, None, :])
    kv = (_rms_norm(c_kv, g_kv) @ w_ukv).reshape(B, S, MLA_HEADS, MLA_NOPE + MLA_V)
    k_nope, v = kv[..., :MLA_NOPE], kv[..., MLA_NOPE:]
    k_r = _rope(k_rope, cos, sin)
    qn = q_nope.reshape(B, nb, BLOCK, MLA_HEADS, MLA_NOPE).transpose(1, 0, 2, 3, 4)
    qr = q_rope.reshape(B, nb, BLOCK, MLA_HEADS, MLA_ROPE).transpose(1, 0, 2, 3, 4)

    def attend_block(args):
        qn_b, qr_b = args
        s = (jnp.einsum('bqhd,bkhd->bhqk', qn_b, k_nope)
             + jnp.einsum('bqhr,bkr->bhqk', qr_b, k_r)).astype(jnp.float32) * MLA_SCALE
        p = jax.nn.softmax(s, axis=-1).astype(v.dtype)
        return jnp.einsum('bhqk,bkhd->bqhd', p, v)

    o = lax.map(attend_block, (qn, qr))
    return o.transpose(1, 0, 2, 3, 4).reshape(B, S, D_MLA)


def _swa(q, k, v, sink, rel_bias):
    B, S, _, _ = q.shape
    nb = S // BLOCK
    qb = q.reshape(B, nb, BLOCK, SWA_KV_HEADS, SWA_GROUP, SWA_HEAD_DIM)
    pad = ((0, 0), (BLOCK, BLOCK), (0, 0), (0, 0))
    kp = jnp.pad(k, pad).reshape(B, nb + 2, BLOCK, SWA_KV_HEADS, SWA_HEAD_DIM)
    vp = jnp.pad(v, pad).reshape(B, nb + 2, BLOCK, SWA_KV_HEADS, SWA_HEAD_DIM)
    kband = jnp.concatenate([kp[:, :-2], kp[:, 1:-1], kp[:, 2:]], axis=2)
    vband = jnp.concatenate([vp[:, :-2], vp[:, 1:-1], vp[:, 2:]], axis=2)

    q_loc = np.arange(BLOCK)
    k_loc = np.arange(3 * BLOCK) - BLOCK
    rel = k_loc[None, :] - q_loc[:, None]
    band = np.abs(rel) <= WINDOW
    bucket = _t5_bucket(rel)
    bias = rel_bias[bucket].astype(jnp.float32)
    bias = bias.transpose(2, 0, 1).reshape(SWA_KV_HEADS, SWA_GROUP, BLOCK, 3 * BLOCK)

    k_abs = jnp.arange(nb)[:, None] * BLOCK + jnp.asarray(k_loc)[None, :]
    valid = (k_abs >= 0) & (k_abs < S)
    mask = jnp.asarray(band)[None, :, :] & valid[:, None, :]

    s = jnp.einsum('bnqhgd,bnjhd->bnhgqj', qb, kband).astype(jnp.float32) * SWA_SCALE
    s = jnp.where(mask[None, :, None, None], s + bias, -jnp.inf)
    sink_b = sink.astype(jnp.float32).reshape(SWA_KV_HEADS, SWA_GROUP)[None, None, :, :, None, None]
    m = jnp.maximum(jnp.max(s, axis=-1, keepdims=True), sink_b)
    e = jnp.exp(s - m)
    p = e / (jnp.sum(e, axis=-1, keepdims=True) + jnp.exp(sink_b - m))
    o = jnp.einsum('bnhgqj,bnjhd->bnqhgd', p.astype(v.dtype), vband)
    return o.reshape(B, S, D_SWA)


def setup_inputs(seed: int = 0) -> dict:
    key = jax.random.key(seed)
    ks = jax.random.split(key, 12)
    f32 = jnp.float32
    x = jax.random.normal(ks[0], (BATCH, SEQ, D_MODEL), f32)
    w_in = jax.random.normal(ks[1], (D_MODEL, D_IN), f32) * D_MODEL ** -0.5
    g_q = 1.0 + 0.05 * jax.random.normal(ks[2], (Q_LORA,), f32)
    g_kv = 1.0 + 0.05 * jax.random.normal(ks[3], (KV_LORA,), f32)
    w_uq = jax.random.normal(ks[4], (Q_LORA, MLA_HEADS * (MLA_NOPE + MLA_ROPE)), f32) * Q_LORA ** -0.5
    w_ukv = jax.random.normal(ks[5], (KV_LORA, MLA_HEADS * (MLA_NOPE + MLA_V)), f32) * KV_LORA ** -0.5
    sink = 0.5 * jax.random.normal(ks[6], (SWA_HEADS,), f32)
    rel_bias = 0.5 * jax.random.normal(ks[7], (N_BUCKETS, SWA_HEADS), f32)
    w_out = jax.random.normal(ks[8], (D_MIX, D_MODEL), f32) * (D_MIX ** -0.5) * BETA
    ln_g = 1.0 + 0.05 * jax.random.normal(ks[9], (D_MODEL,), f32)
    ln_b = 0.02 * jax.random.normal(ks[10], (D_MODEL,), f32)
    return {"x": x, "w_in": w_in, "g_q": g_q, "g_kv": g_kv, "w_uq": w_uq,
            "w_ukv": w_ukv, "sink": sink, "rel_bias": rel_bias, "w_out": w_out,
            "ln_g": ln_g, "ln_b": ln_b}


def reference(x, w_in, g_q, g_kv, w_uq, w_ukv, sink, rel_bias, w_out, ln_g, ln_b):
    B, S, _ = x.shape
    pos = jnp.arange(S, dtype=jnp.float32)
    inv_freq = ROPE_BASE ** (-jnp.arange(0, MLA_ROPE, 2, dtype=jnp.float32) / MLA_ROPE)
    ang = pos[:, None] * inv_freq[None, :]
    cos, sin = jnp.cos(ang), jnp.sin(ang)

    h = x
    for _layer in range(DEPTH):
        proj = h @ w_in
        c_q, c_kv, k_rope, gate_a, q_s, k_s, v_s, gate_b = jnp.split(proj, IN_OFFSETS, axis=-1)
        o_a = _mla(c_q, c_kv, k_rope, g_q, g_kv, w_uq, w_ukv, cos, sin)
        o_b = _swa(q_s.reshape(B, S, SWA_HEADS, SWA_HEAD_DIM),
                   k_s.reshape(B, S, SWA_KV_HEADS, SWA_HEAD_DIM),
                   v_s.reshape(B, S, SWA_KV_HEADS, SWA_HEAD_DIM),
                   sink, rel_bias)
        mixed = jnp.concatenate([o_a * jax.nn.silu(gate_a), o_b * jax.nn.silu(gate_b)], axis=-1)
        h = _layer_norm(ALPHA * h + mixed @ w_out, ln_g, ln_b)
    return h
```

```python
import functools
import math

import jax
import jax.numpy as jnp
import numpy as np
from jax import lax
from jax.experimental import pallas as pl
from jax.experimental.pallas import tpu as pltpu

D_MODEL = 1024
MLA_HEADS = 8
MLA_NOPE = 64
MLA_ROPE = 32
MLA_V = 64
MLA_QK = MLA_NOPE + MLA_ROPE
Q_LORA = 256
KV_LORA = 128
D_MLA = MLA_HEADS * MLA_V
MLA_SCALE = 1.0 / math.sqrt(MLA_QK)
ROPE_BASE = 10000.0

SWA_HEADS = 8
SWA_KV_HEADS = 2
SWA_HEAD_DIM = 64
SWA_GROUP = SWA_HEADS // SWA_KV_HEADS
D_SWA = SWA_HEADS * SWA_HEAD_DIM
WINDOW = 128
BLOCK = 128
SWA_SCALE = 1.0 / math.sqrt(SWA_HEAD_DIM)
N_BUCKETS = 32
MAX_DISTANCE = 128

DEPTH = 1
ALPHA = (2.0 * DEPTH) ** 0.25

IN_SPLITS = (Q_LORA, KV_LORA, MLA_ROPE, D_MLA, D_SWA,
             SWA_KV_HEADS * SWA_HEAD_DIM, SWA_KV_HEADS * SWA_HEAD_DIM, D_SWA)
IN_OFFSETS = tuple(int(o) for o in np.cumsum((0,) + IN_SPLITS))

LANE = 128
QK_PAD = LANE
VMEM_LIMIT_BYTES = 56 * 1024 * 1024

PROJ_TOKENS = 512
MLA_TQ = 512
MLA_TK = 512
SWA_TOKENS = 1024
OUT_TOKENS = 512

TR_CQ = 0
TR_GA = TR_CQ + Q_LORA
TR_QS = TR_GA + D_MLA
TR_VS = TR_QS + D_SWA
TR_GB = TR_VS + SWA_KV_HEADS * SWA_HEAD_DIM
TR_ROWS = TR_GB + D_SWA
NAT_CKV = 0
NAT_KS = NAT_CKV + KV_LORA
NAT_KR = NAT_KS + SWA_KV_HEADS * SWA_HEAD_DIM
NAT_KR_ROT = NAT_KR + LANE
NAT_COLS = NAT_KR_ROT + LANE

_NT = (((1,), (1,)), ((), ()))
_TN = (((0,), (0,)), ((), ()))


def _rsqrt_mean_sq(x, axis, eps):
    return lax.rsqrt(jnp.mean(x * x, axis=axis, keepdims=True) + eps)


def _proj_kernel(x_ref, wtr_ref, wnat_ref, gq_ref, gkv_ref, wuqt_ref, wuk_ref, wuvt_ref,
                 cos_t_ref, sin_t_ref, cos_n_ref, sin_n_ref,
                 qt_ref, k_ref, vt_ref, gat_ref, qst_ref, ks_ref, vst_ref, gbt_ref,
                 tr_ref, nat_ref):
    xb = x_ref[0].astype(jnp.bfloat16)
    tr_ref[...] = lax.dot_general(wtr_ref[...], xb, _NT, preferred_element_type=jnp.float32)
    nat_ref[...] = jnp.dot(xb, wnat_ref[...], preferred_element_type=jnp.float32)

    cq = tr_ref[TR_CQ:TR_CQ + Q_LORA, :]
    cqn = (cq * _rsqrt_mean_sq(cq, 0, 1e-6)) * gq_ref[...]
    q_t = jnp.dot(wuqt_ref[...], cqn.astype(jnp.bfloat16),
                  preferred_element_type=jnp.float32)
    cos_t = cos_t_ref[...]
    sin_t = sin_t_ref[...]
    half = MLA_ROPE // 2
    zeros_pad = jnp.zeros((QK_PAD - MLA_QK, q_t.shape[1]), jnp.bfloat16)
    for h in range(MLA_HEADS):
        base = h * MLA_QK
        nope = q_t[base:base + MLA_NOPE]
        r1 = q_t[base + MLA_NOPE:base + MLA_NOPE + half]
        r2 = q_t[base + MLA_NOPE + half:base + MLA_QK]
        qt_ref[0, h, 0:MLA_NOPE, :] = (nope * MLA_SCALE).astype(jnp.bfloat16)
        qt_ref[0, h, MLA_NOPE:MLA_NOPE + half, :] = (
            (r1 * cos_t - r2 * sin_t) * MLA_SCALE).astype(jnp.bfloat16)
        qt_ref[0, h, MLA_NOPE + half:MLA_QK, :] = (
            (r2 * cos_t + r1 * sin_t) * MLA_SCALE).astype(jnp.bfloat16)
        qt_ref[0, h, MLA_QK:QK_PAD, :] = zeros_pad

    ckv = nat_ref[:, NAT_CKV:NAT_CKV + KV_LORA]
    kvn = ((ckv * _rsqrt_mean_sq(ckv, 1, 1e-6)) * gkv_ref[...]).astype(jnp.bfloat16)
    k_nope = jnp.dot(kvn, wuk_ref[...], preferred_element_type=jnp.float32)
    v_t = lax.dot_general(wuvt_ref[...], kvn, _NT, preferred_element_type=jnp.float32)
    k_rope = (nat_ref[:, NAT_KR:NAT_KR + LANE] * cos_n_ref[...]
              + nat_ref[:, NAT_KR_ROT:NAT_KR_ROT + LANE] * sin_n_ref[...])
    for h in range(MLA_HEADS):
        k_ref[0, h] = (k_nope[:, h * QK_PAD:(h + 1) * QK_PAD] + k_rope).astype(jnp.bfloat16)
        vt_ref[0, h] = v_t[h * MLA_V:(h + 1) * MLA_V].astype(jnp.bfloat16)

    gat_ref[0] = jax.nn.silu(tr_ref[TR_GA:TR_GA + D_MLA, :]).astype(jnp.bfloat16)
    gbt_ref[0] = jax.nn.silu(tr_ref[TR_GB:TR_GB + D_SWA, :]).astype(jnp.bfloat16)

    n_blocks = x_ref.shape[1] // BLOCK
    for g in range(SWA_KV_HEADS):
        ks_ref[0, g] = nat_ref[:, NAT_KS + g * SWA_HEAD_DIM:
                               NAT_KS + (g + 1) * SWA_HEAD_DIM].astype(jnp.bfloat16)
        vst_ref[0, g] = tr_ref[TR_VS + g * SWA_HEAD_DIM:
                               TR_VS + (g + 1) * SWA_HEAD_DIM, :].astype(jnp.bfloat16)
        for hh in range(SWA_GROUP):
            row = TR_QS + (g * SWA_GROUP + hh) * SWA_HEAD_DIM
            q_h = (tr_ref[row:row + SWA_HEAD_DIM, :] * SWA_SCALE).astype(jnp.bfloat16)
            for nb in range(n_blocks):
                col = (nb * SWA_GROUP + hh) * BLOCK
                qst_ref[0, g, :, col:col + BLOCK] = q_h[:, nb * BLOCK:(nb + 1) * BLOCK]


def _project(x, w_tr, w_nat, g_q, g_kv, w_uq_t, w_uk, w_uv_t, cos_t, sin_t, cos_n, sin_n):
    B, S, _ = x.shape
    T = PROJ_TOKENS
    grid = (B, S // T)
    const = lambda *shape: pl.BlockSpec(shape, lambda b, t: (0,) * len(shape))
    bf = jnp.bfloat16
    out_shape = (
        jax.ShapeDtypeStruct((B, MLA_HEADS, QK_PAD, S), bf),
        jax.ShapeDtypeStruct((B, MLA_HEADS, S, QK_PAD), bf),
        jax.ShapeDtypeStruct((B, MLA_HEADS, MLA_V, S), bf),
        jax.ShapeDtypeStruct((B, D_MLA, S), bf),
        jax.ShapeDtypeStruct((B, SWA_KV_HEADS, SWA_HEAD_DIM, SWA_GROUP * S), bf),
        jax.ShapeDtypeStruct((B, SWA_KV_HEADS, S, SWA_HEAD_DIM), bf),
        jax.ShapeDtypeStruct((B, SWA_KV_HEADS, SWA_HEAD_DIM, S), bf),
        jax.ShapeDtypeStruct((B, D_SWA, S), bf),
    )
    out_specs = (
        pl.BlockSpec((1, MLA_HEADS, QK_PAD, T), lambda b, t: (b, 0, 0, t)),
        pl.BlockSpec((1, MLA_HEADS, T, QK_PAD), lambda b, t: (b, 0, t, 0)),
        pl.BlockSpec((1, MLA_HEADS, MLA_V, T), lambda b, t: (b, 0, 0, t)),
        pl.BlockSpec((1, D_MLA, T), lambda b, t: (b, 0, t)),
        pl.BlockSpec((1, SWA_KV_HEADS, SWA_HEAD_DIM, SWA_GROUP * T), lambda b, t: (b, 0, 0, t)),
        pl.BlockSpec((1, SWA_KV_HEADS, T, SWA_HEAD_DIM), lambda b, t: (b, 0, t, 0)),
        pl.BlockSpec((1, SWA_KV_HEADS, SWA_HEAD_DIM, T), lambda b, t: (b, 0, 0, t)),
        pl.BlockSpec((1, D_SWA, T), lambda b, t: (b, 0, t)),
    )
    in_specs = [
        pl.BlockSpec((1, T, D_MODEL), lambda b, t: (b, t, 0)),
        const(TR_ROWS, D_MODEL),
        const(D_MODEL, NAT_COLS),
        const(Q_LORA, 1),
        const(1, KV_LORA),
        const(MLA_HEADS * MLA_QK, Q_LORA),
        const(KV_LORA, MLA_HEADS * QK_PAD),
        const(D_MLA, KV_LORA),
        pl.BlockSpec((MLA_ROPE // 2, T), lambda b, t: (0, t)),
        pl.BlockSpec((MLA_ROPE // 2, T), lambda b, t: (0, t)),
        pl.BlockSpec((T, LANE), lambda b, t: (t, 0)),
        pl.BlockSpec((T, LANE), lambda b, t: (t, 0)),
    ]
    return pl.pallas_call(
        _proj_kernel,
        grid=grid,
        in_specs=in_specs,
        out_specs=out_specs,
        out_shape=out_shape,
        scratch_shapes=[pltpu.VMEM((TR_ROWS, T), jnp.float32),
                        pltpu.VMEM((T, NAT_COLS), jnp.float32)],
        compiler_params=pltpu.CompilerParams(
            dimension_semantics=("arbitrary", "arbitrary"),
            vmem_limit_bytes=VMEM_LIMIT_BYTES),
        name="proj",
    )(x, w_tr, w_nat, g_q, g_kv, w_uq_t, w_uk, w_uv_t, cos_t, sin_t, cos_n, sin_n)


def _mla_kernel(qt_ref, k_ref, vt_ref, gate_ref, o_ref, m_ref, l_ref, acc_ref):
    n_chunks = k_ref.shape[2] // MLA_TK
    m_ref[...] = jnp.full(m_ref.shape, -jnp.inf, jnp.float32)
    l_ref[...] = jnp.zeros(l_ref.shape, jnp.float32)
    acc_ref[...] = jnp.zeros(acc_ref.shape, jnp.float32)
    q_t = qt_ref[0, 0]

    def chunk(c, carry):
        start = pl.multiple_of(c * MLA_TK, MLA_TK)
        k = k_ref[0, 0, pl.ds(start, MLA_TK), :]
        s = jnp.dot(k, q_t, preferred_element_type=jnp.float32)
        m_prev = m_ref[...]
        m_new = jnp.maximum(m_prev, jnp.max(s, axis=0, keepdims=True))
        alpha = jnp.exp(m_prev - m_new)
        p = jnp.exp(s - m_new)
        l_ref[...] = alpha * l_ref[...] + jnp.sum(p, axis=0, keepdims=True)
        v_t = vt_ref[0, 0, :, pl.ds(start, MLA_TK)]
        acc_ref[...] = alpha * acc_ref[...] + jnp.dot(
            v_t, p.astype(jnp.bfloat16), preferred_element_type=jnp.float32)
        m_ref[...] = m_new
        return carry

    lax.fori_loop(0, n_chunks, chunk, 0)
    o = acc_ref[...] / l_ref[...]
    o_ref[0] = (o * gate_ref[0].astype(jnp.float32)).astype(o_ref.dtype)


def _mla_attention(q_t, k, v_t, gate_t):
    B, H, _, S = q_t.shape
    TQ = MLA_TQ
    grid = (B, H, S // TQ)
    return pl.pallas_call(
        _mla_kernel,
        grid=grid,
        in_specs=[
            pl.BlockSpec((1, 1, QK_PAD, TQ), lambda b, h, i: (b, h, 0, i)),
            pl.BlockSpec((1, 1, S, QK_PAD), lambda b, h, i: (b, h, 0, 0)),
            pl.BlockSpec((1, 1, MLA_V, S), lambda b, h, i: (b, h, 0, 0)),
            pl.BlockSpec((1, MLA_V, TQ), lambda b, h, i: (b, h, i)),
        ],
        out_specs=pl.BlockSpec((1, MLA_V, TQ), lambda b, h, i: (b, h, i)),
        out_shape=jax.ShapeDtypeStruct((B, D_MLA, S), jnp.bfloat16),
        scratch_shapes=[pltpu.VMEM((1, TQ), jnp.float32),
                        pltpu.VMEM((1, TQ), jnp.float32),
                        pltpu.VMEM((MLA_V, TQ), jnp.float32)],
        compiler_params=pltpu.CompilerParams(
            dimension_semantics=("arbitrary", "arbitrary", "arbitrary"),
            vmem_limit_bytes=VMEM_LIMIT_BYTES),
        name="mla",
    )(q_t, k, v_t, gate_t)


def _swa_kernel(qst_ref, ks_ref, vst_ref, bias_ref, sink_ref, gate_ref, o_ref):
    n_total = ks_ref.shape[2] // BLOCK
    n_local = o_ref.shape[2] // BLOCK
    step = pl.program_id(2)
    sink = sink_ref[0]
    neg_inf = jnp.float32(-jnp.inf)
    for j in range(n_local):
        n = step * n_local + j
        q_t = qst_ref[0, 0, :, j * SWA_GROUP * BLOCK:(j + 1) * SWA_GROUP * BLOCK]
        starts = (jnp.maximum(n - 1, 0), n, jnp.minimum(n + 1, n_total - 1))
        penalties = (jnp.where(n > 0, 0.0, neg_inf), None,
                     jnp.where(n < n_total - 1, 0.0, neg_inf))
        scores = []
        for part in range(3):
            start = pl.multiple_of(starts[part] * BLOCK, BLOCK)
            k = ks_ref[0, 0, pl.ds(start, BLOCK), :]
            s = jnp.dot(k, q_t, preferred_element_type=jnp.float32) + bias_ref[0, part]
            if penalties[part] is not None:
                s = s + penalties[part]
            scores.append(s)
        m = sink
        for s in scores:
            m = jnp.maximum(m, jnp.max(s, axis=0, keepdims=True))
        denom = jnp.exp(sink - m)
        acc = jnp.zeros((SWA_HEAD_DIM, SWA_GROUP * BLOCK), jnp.float32)
        for part in range(3):
            start = pl.multiple_of(starts[part] * BLOCK, BLOCK)
            e = jnp.exp(scores[part] - m)
            denom = denom + jnp.sum(e, axis=0, keepdims=True)
            v_t = vst_ref[0, 0, :, pl.ds(start, BLOCK)]
            acc = acc + jnp.dot(v_t, e.astype(jnp.bfloat16), preferred_element_type=jnp.float32)
        o = acc / denom
        for hh in range(SWA_GROUP):
            gate = gate_ref[0, hh * SWA_HEAD_DIM:(hh + 1) * SWA_HEAD_DIM,
                            j * BLOCK:(j + 1) * BLOCK].astype(jnp.float32)
            o_ref[0, hh * SWA_HEAD_DIM:(hh + 1) * SWA_HEAD_DIM, j * BLOCK:(j + 1) * BLOCK] = (
                o[:, hh * BLOCK:(hh + 1) * BLOCK] * gate).astype(o_ref.dtype)


def _swa_attention(qs_t, ks, vs_t, bias_t, sink_rows, gate_t):
    B, G, _, S = vs_t.shape
    T = SWA_TOKENS
    grid = (B, G, S // T)
    rows = SWA_GROUP * SWA_HEAD_DIM
    return pl.pallas_call(
        _swa_kernel,
        grid=grid,
        in_specs=[
            pl.BlockSpec((1, 1, SWA_HEAD_DIM, SWA_GROUP * T), lambda b, g, t: (b, g, 0, t)),
            pl.BlockSpec((1, 1, S, SWA_HEAD_DIM), lambda b, g, t: (b, g, 0, 0)),
            pl.BlockSpec((1, 1, SWA_HEAD_DIM, S), lambda b, g, t: (b, g, 0, 0)),
            pl.BlockSpec((1, 3, BLOCK, SWA_GROUP * BLOCK), lambda b, g, t: (g, 0, 0, 0)),
            pl.BlockSpec((1, 1, SWA_GROUP * BLOCK), lambda b, g, t: (g, 0, 0)),
            pl.BlockSpec((1, rows, T), lambda b, g, t: (b, g, t)),
        ],
        out_specs=pl.BlockSpec((1, rows, T), lambda b, g, t: (b, g, t)),
        out_shape=jax.ShapeDtypeStruct((B, D_SWA, S), jnp.bfloat16),
        compiler_params=pltpu.CompilerParams(
            dimension_semantics=("arbitrary", "arbitrary", "arbitrary"),
            vmem_limit_bytes=VMEM_LIMIT_BYTES),
        name="swa",
    )(qs_t, ks, vs_t, bias_t, sink_rows, gate_t)


def _out_kernel(x_ref, ma_ref, mb_ref, wa_ref, wb_ref, g_ref, b_ref, y_ref):
    out = lax.dot_general(ma_ref[0], wa_ref[...], _TN, preferred_element_type=jnp.float32)
    out = out + lax.dot_general(mb_ref[0], wb_ref[...], _TN, preferred_element_type=jnp.float32)
    h = ALPHA * x_ref[0] + out
    mu = jnp.mean(h, axis=-1, keepdims=True)
    var = jnp.mean(jnp.square(h - mu), axis=-1, keepdims=True)
    y_ref[0] = ((h - mu) * lax.rsqrt(var + 1e-5)) * g_ref[...] + b_ref[...]


def _out_project(x, mixed_a_t, mixed_b_t, w_a, w_b, ln_g, ln_b):
    B, S, D = x.shape
    T = OUT_TOKENS
    grid = (B, S // T)
    const = lambda *shape: pl.BlockSpec(shape, lambda b, t: (0,) * len(shape))
    return pl.pallas_call(
        _out_kernel,
        grid=grid,
        in_specs=[
            pl.BlockSpec((1, T, D), lambda b, t: (b, t, 0)),
            pl.BlockSpec((1, D_MLA, T), lambda b, t: (b, 0, t)),
            pl.BlockSpec((1, D_SWA, T), lambda b, t: (b, 0, t)),
            const(D_MLA, D),
            const(D_SWA, D),
            const(1, D),
            const(1, D),
        ],
        out_specs=pl.BlockSpec((1, T, D), lambda b, t: (b, t, 0)),
        out_shape=jax.ShapeDtypeStruct((B, S, D), x.dtype),
        compiler_params=pltpu.CompilerParams(
            dimension_semantics=("arbitrary", "arbitrary"),
            vmem_limit_bytes=VMEM_LIMIT_BYTES),
        name="outproj",
    )(x, mixed_a_t, mixed_b_t, w_a, w_b, ln_g, ln_b)


def _t5_bucket(rel):
    half = N_BUCKETS // 2
    ret = np.where(rel > 0, half, 0)
    n = np.abs(rel)
    max_exact = half // 2
    large = max_exact + (np.log(np.maximum(n, 1).astype(np.float32) / max_exact)
                         / np.log(MAX_DISTANCE / max_exact) * (half - max_exact)).astype(np.int32)
    large = np.minimum(large, half - 1)
    return (ret + np.where(n < max_exact, n, large)).astype(np.int32)


def _band_geometry():
    q_loc = np.arange(BLOCK)
    k_loc = np.arange(3 * BLOCK) - BLOCK
    rel = k_loc[:, None] - q_loc[None, :]
    return _t5_bucket(rel), np.abs(rel) <= WINDOW


def _col(w_in, i):
    return w_in[:, IN_OFFSETS[i]:IN_OFFSETS[i + 1]]


def kernel(x, w_in, g_q, g_kv, w_uq, w_ukv, sink, rel_bias, w_out, ln_g, ln_b):
    B, S, _ = x.shape
    bf = jnp.bfloat16
    f32 = jnp.float32

    pos = jnp.arange(S, dtype=f32)
    inv_freq = ROPE_BASE ** (-jnp.arange(0, MLA_ROPE, 2, dtype=f32) / MLA_ROPE)
    ang = pos[:, None] * inv_freq[None, :]
    cos, sin = jnp.cos(ang), jnp.sin(ang)
    half = MLA_ROPE // 2
    lane_pad = lambda a, b_: jnp.concatenate(
        [jnp.zeros((S, MLA_NOPE), f32), a, b_, jnp.zeros((S, LANE - MLA_QK), f32)], axis=1)
    cos_n = lane_pad(cos, cos)
    sin_n = lane_pad(sin, sin)

    c_q, c_kv, k_rope, gate_a, q_s, k_s, v_s, gate_b = (_col(w_in, i) for i in range(8))
    w_tr = jnp.concatenate([c_q, gate_a, q_s, v_s, gate_b], axis=1).T.astype(bf)
    zeros = lambda n: jnp.zeros((D_MODEL, n), f32)
    kr_pad = jnp.concatenate([zeros(MLA_NOPE), k_rope, zeros(LANE - MLA_QK)], axis=1)
    kr_rot = jnp.concatenate([zeros(MLA_NOPE), -k_rope[:, half:], k_rope[:, :half],
                              zeros(LANE - MLA_QK)], axis=1)
    w_nat = jnp.concatenate([c_kv, k_s, kr_pad, kr_rot], axis=1).astype(bf)
    w_ukv3 = w_ukv.reshape(KV_LORA, MLA_HEADS, MLA_NOPE + MLA_V)
    w_uk = jnp.pad(w_ukv3[:, :, :MLA_NOPE], ((0, 0), (0, 0), (0, QK_PAD - MLA_NOPE)))
    w_uk = w_uk.reshape(KV_LORA, MLA_HEADS * QK_PAD).astype(bf)
    w_uv_t = w_ukv3[:, :, MLA_NOPE:].reshape(KV_LORA, D_MLA).T.astype(bf)
    w_uq_t = w_uq.T.astype(bf)

    q_t, k, v_t, gate_a_t, qs_t, ks, vs_t, gate_b_t = _project(
        x, w_tr, w_nat, g_q.reshape(Q_LORA, 1), g_kv.reshape(1, KV_LORA),
        w_uq_t, w_uk, w_uv_t, cos.T, sin.T, cos_n, sin_n)

    mixed_a_t = _mla_attention(q_t, k, v_t, gate_a_t)

    bucket, band = _band_geometry()
    bias = jnp.where(band[:, :, None], rel_bias.astype(f32)[bucket], -jnp.inf)
    bias = bias.reshape(3, BLOCK, BLOCK, SWA_KV_HEADS, SWA_GROUP)
    bias_t = bias.transpose(3, 0, 1, 4, 2).reshape(SWA_KV_HEADS, 3, BLOCK, SWA_GROUP * BLOCK)
    sink_rows = jnp.repeat(sink.astype(f32).reshape(SWA_KV_HEADS, 1, SWA_GROUP), BLOCK, axis=2)

    mixed_b_t = _swa_attention(qs_t, ks, vs_t, bias_t, sink_rows, gate_b_t)

    w_out_bf = w_out.astype(bf)
    return _out_project(x, mixed_a_t, mixed_b_t, w_out_bf[:D_MLA], w_out_bf[D_MLA:],
                        ln_g.reshape(1, D_MODEL), ln_b.reshape(1, D_MODEL))
```

```python
import functools
import math

import jax
import jax.numpy as jnp
import numpy as np
from jax import lax
from jax.experimental import pallas as pl
from jax.experimental.pallas import tpu as pltpu

D_MODEL = 1024
MLA_HEADS = 8
MLA_NOPE = 64
MLA_ROPE = 32
MLA_V = 64
MLA_QK = MLA_NOPE + MLA_ROPE
Q_LORA = 256
KV_LORA = 128
D_MLA = MLA_HEADS * MLA_V
MLA_SCALE = 1.0 / math.sqrt(MLA_QK)
MLA_Q_SCALE = MLA_SCALE * math.log2(math.e)
BF16_SUBLANES = 16
MLA_V_ROWS = MLA_V + BF16_SUBLANES
ROPE_BASE = 10000.0

SWA_HEADS = 8
SWA_KV_HEADS = 2
SWA_HEAD_DIM = 64
SWA_GROUP = SWA_HEADS // SWA_KV_HEADS
D_SWA = SWA_HEADS * SWA_HEAD_DIM
WINDOW = 128
BLOCK = 128
SWA_SCALE = 1.0 / math.sqrt(SWA_HEAD_DIM)
N_BUCKETS = 32
MAX_DISTANCE = 128

DEPTH = 1
ALPHA = (2.0 * DEPTH) ** 0.25

IN_SPLITS = (Q_LORA, KV_LORA, MLA_ROPE, D_MLA, D_SWA,
             SWA_KV_HEADS * SWA_HEAD_DIM, SWA_KV_HEADS * SWA_HEAD_DIM, D_SWA)
IN_OFFSETS = tuple(int(o) for o in np.cumsum((0,) + IN_SPLITS))

LANE = 128
QK_PAD = LANE
VMEM_LIMIT_BYTES = 56 * 1024 * 1024

PROJ_TOKENS = 512
MLA_TQ = 512
MLA_TK = 512
MLA_MAX_EXCESS = 64.0
SWA_TOKENS = 1024
OUT_TOKENS = 512

TR_CQ = 0
TR_GA = TR_CQ + Q_LORA
TR_QS = TR_GA + D_MLA
TR_VS = TR_QS + D_SWA
TR_GB = TR_VS + SWA_KV_HEADS * SWA_HEAD_DIM
TR_ROWS = TR_GB + D_SWA
NAT_CKV = 0
NAT_KS = NAT_CKV + KV_LORA
NAT_KR = NAT_KS + SWA_KV_HEADS * SWA_HEAD_DIM
NAT_KR_ROT = NAT_KR + LANE
NAT_COLS = NAT_KR_ROT + LANE

_NT = (((1,), (1,)), ((), ()))
_TN = (((0,), (0,)), ((), ()))


def _rsqrt_mean_sq(x, axis, eps):
    return lax.rsqrt(jnp.mean(x * x, axis=axis, keepdims=True) + eps)


def _proj_kernel(x_ref, wtr_ref, wnat_ref, gq_ref, gkv_ref, wuqt_ref, wuk_ref, wuvt_ref,
                 cos_t_ref, sin_t_ref, cos_n_ref, sin_n_ref,
                 qt_ref, k_ref, vt_ref, gat_ref, qst_ref, ks_ref, vst_ref, gbt_ref,
                 tr_ref, nat_ref):
    xb = x_ref[0].astype(jnp.bfloat16)
    tr_ref[...] = lax.dot_general(wtr_ref[...], xb, _NT, preferred_element_type=jnp.float32)
    nat_ref[...] = jnp.dot(xb, wnat_ref[...], preferred_element_type=jnp.float32)

    cq = tr_ref[TR_CQ:TR_CQ + Q_LORA, :]
    cqn = (cq * _rsqrt_mean_sq(cq, 0, 1e-6)) * gq_ref[...]
    q_t = jnp.dot(wuqt_ref[...], cqn.astype(jnp.bfloat16),
                  preferred_element_type=jnp.float32)
    cos_t = cos_t_ref[...]
    sin_t = sin_t_ref[...]
    half = MLA_ROPE // 2
    zeros_pad = jnp.zeros((QK_PAD - MLA_QK, q_t.shape[1]), jnp.bfloat16)
    for h in range(MLA_HEADS):
        base = h * MLA_QK
        nope = q_t[base:base + MLA_NOPE]
        r1 = q_t[base + MLA_NOPE:base + MLA_NOPE + half]
        r2 = q_t[base + MLA_NOPE + half:base + MLA_QK]
        qt_ref[0, h, 0:MLA_NOPE, :] = (nope * MLA_Q_SCALE).astype(jnp.bfloat16)
        qt_ref[0, h, MLA_NOPE:MLA_NOPE + half, :] = (
            (r1 * cos_t - r2 * sin_t) * MLA_Q_SCALE).astype(jnp.bfloat16)
        qt_ref[0, h, MLA_NOPE + half:MLA_QK, :] = (
            (r2 * cos_t + r1 * sin_t) * MLA_Q_SCALE).astype(jnp.bfloat16)
        qt_ref[0, h, MLA_QK:QK_PAD, :] = zeros_pad

    ckv = nat_ref[:, NAT_CKV:NAT_CKV + KV_LORA]
    kvn = ((ckv * _rsqrt_mean_sq(ckv, 1, 1e-6)) * gkv_ref[...]).astype(jnp.bfloat16)
    k_nope = jnp.dot(kvn, wuk_ref[...], preferred_element_type=jnp.float32)
    v_t = lax.dot_general(wuvt_ref[...], kvn, _NT, preferred_element_type=jnp.float32)
    k_rope = (nat_ref[:, NAT_KR:NAT_KR + LANE] * cos_n_ref[...]
              + nat_ref[:, NAT_KR_ROT:NAT_KR_ROT + LANE] * sin_n_ref[...])
    row_id = lax.broadcasted_iota(jnp.int32, (BF16_SUBLANES, v_t.shape[1]), 0)
    ones_row = jnp.where(row_id == 0, 1.0, 0.0).astype(jnp.bfloat16)
    for h in range(MLA_HEADS):
        k_ref[0, h] = (k_nope[:, h * QK_PAD:(h + 1) * QK_PAD] + k_rope).astype(jnp.bfloat16)
        vt_ref[0, h, 0:MLA_V, :] = v_t[h * MLA_V:(h + 1) * MLA_V].astype(jnp.bfloat16)
        vt_ref[0, h, MLA_V:MLA_V_ROWS, :] = ones_row

    gat_ref[0] = jax.nn.silu(tr_ref[TR_GA:TR_GA + D_MLA, :]).astype(jnp.bfloat16)
    gbt_ref[0] = jax.nn.silu(tr_ref[TR_GB:TR_GB + D_SWA, :]).astype(jnp.bfloat16)

    n_blocks = x_ref.shape[1] // BLOCK
    for g in range(SWA_KV_HEADS):
        ks_ref[0, g] = nat_ref[:, NAT_KS + g * SWA_HEAD_DIM:
                               NAT_KS + (g + 1) * SWA_HEAD_DIM].astype(jnp.bfloat16)
        vst_ref[0, g] = tr_ref[TR_VS + g * SWA_HEAD_DIM:
                               TR_VS + (g + 1) * SWA_HEAD_DIM, :].astype(jnp.bfloat16)
        for hh in range(SWA_GROUP):
            row = TR_QS + (g * SWA_GROUP + hh) * SWA_HEAD_DIM
            q_h = (tr_ref[row:row + SWA_HEAD_DIM, :] * SWA_SCALE).astype(jnp.bfloat16)
            for nb in range(n_blocks):
                col = (nb * SWA_GROUP + hh) * BLOCK
                qst_ref[0, g, :, col:col + BLOCK] = q_h[:, nb * BLOCK:(nb + 1) * BLOCK]


def _project(x, w_tr, w_nat, g_q, g_kv, w_uq_t, w_uk, w_uv_t, cos_t, sin_t, cos_n, sin_n):
    B, S, _ = x.shape
    T = PROJ_TOKENS
    grid = (B, S // T)
    const = lambda *shape: pl.BlockSpec(shape, lambda b, t: (0,) * len(shape))
    bf = jnp.bfloat16
    out_shape = (
        jax.ShapeDtypeStruct((B, MLA_HEADS, QK_PAD, S), bf),
        jax.ShapeDtypeStruct((B, MLA_HEADS, S, QK_PAD), bf),
        jax.ShapeDtypeStruct((B, MLA_HEADS, MLA_V_ROWS, S), bf),
        jax.ShapeDtypeStruct((B, D_MLA, S), bf),
        jax.ShapeDtypeStruct((B, SWA_KV_HEADS, SWA_HEAD_DIM, SWA_GROUP * S), bf),
        jax.ShapeDtypeStruct((B, SWA_KV_HEADS, S, SWA_HEAD_DIM), bf),
        jax.ShapeDtypeStruct((B, SWA_KV_HEADS, SWA_HEAD_DIM, S), bf),
        jax.ShapeDtypeStruct((B, D_SWA, S), bf),
    )
    out_specs = (
        pl.BlockSpec((1, MLA_HEADS, QK_PAD, T), lambda b, t: (b, 0, 0, t)),
        pl.BlockSpec((1, MLA_HEADS, T, QK_PAD), lambda b, t: (b, 0, t, 0)),
        pl.BlockSpec((1, MLA_HEADS, MLA_V_ROWS, T), lambda b, t: (b, 0, 0, t)),
        pl.BlockSpec((1, D_MLA, T), lambda b, t: (b, 0, t)),
        pl.BlockSpec((1, SWA_KV_HEADS, SWA_HEAD_DIM, SWA_GROUP * T), lambda b, t: (b, 0, 0, t)),
        pl.BlockSpec((1, SWA_KV_HEADS, T, SWA_HEAD_DIM), lambda b, t: (b, 0, t, 0)),
        pl.BlockSpec((1, SWA_KV_HEADS, SWA_HEAD_DIM, T), lambda b, t: (b, 0, 0, t)),
        pl.BlockSpec((1, D_SWA, T), lambda b, t: (b, 0, t)),
    )
    in_specs = [
        pl.BlockSpec((1, T, D_MODEL), lambda b, t: (b, t, 0)),
        const(TR_ROWS, D_MODEL),
        const(D_MODEL, NAT_COLS),
        const(Q_LORA, 1),
        const(1, KV_LORA),
        const(MLA_HEADS * MLA_QK, Q_LORA),
        const(KV_LORA, MLA_HEADS * QK_PAD),
        const(D_MLA, KV_LORA),
        pl.BlockSpec((MLA_ROPE // 2, T), lambda b, t: (0, t)),
        pl.BlockSpec((MLA_ROPE // 2, T), lambda b, t: (0, t)),
        pl.BlockSpec((T, LANE), lambda b, t: (t, 0)),
        pl.BlockSpec((T, LANE), lambda b, t: (t, 0)),
    ]
    return pl.pallas_call(
        _proj_kernel,
        grid=grid,
        in_specs=in_specs,
        out_specs=out_specs,
        out_shape=out_shape,
        scratch_shapes=[pltpu.VMEM((TR_ROWS, T), jnp.float32),
                        pltpu.VMEM((T, NAT_COLS), jnp.float32)],
        compiler_params=pltpu.CompilerParams(
            dimension_semantics=("arbitrary", "arbitrary"),
            vmem_limit_bytes=VMEM_LIMIT_BYTES),
        name="proj",
    )(x, w_tr, w_nat, g_q, g_kv, w_uq_t, w_uk, w_uv_t, cos_t, sin_t, cos_n, sin_n)


def _mla_kernel(qt_ref, k_ref, vt_ref, gate_ref, o_ref, m_ref, acc_ref):
    n_chunks = k_ref.shape[2] // MLA_TK
    q_t = qt_ref[0, 0]

    def chunk_slice(c):
        start = c * MLA_TK
        return pl.ds(start if isinstance(c, int) else pl.multiple_of(start, MLA_TK), MLA_TK)

    def key_chunk(c):
        return k_ref[0, 0, chunk_slice(c), :]

    def value_chunk(c):
        return vt_ref[0, 0, :, chunk_slice(c)]

    s0 = jnp.dot(k_ref[0, 0, 0:BF16_SUBLANES, :], q_t, preferred_element_type=jnp.float32)
    m_use = jnp.max(s0, axis=0, keepdims=True)
    excess = jnp.zeros_like(m_use)
    acc = None
    s_next = jnp.dot(key_chunk(0), q_t, preferred_element_type=jnp.float32)
    for c in range(n_chunks):
        s = s_next
        if c + 1 < n_chunks:
            s_next = jnp.dot(key_chunk(c + 1), q_t, preferred_element_type=jnp.float32)
        p = jnp.exp2(s - m_use).astype(jnp.bfloat16)
        pv = jnp.dot(value_chunk(c), p, preferred_element_type=jnp.float32)
        acc = pv if acc is None else acc + pv
        cmax = jnp.max(s, axis=0, keepdims=True)
        excess = jnp.maximum(excess, cmax - m_use)
        if c + 1 < n_chunks:
            m_next = jnp.maximum(m_use, cmax)
            acc = acc * jnp.exp2(m_use - m_next)
            m_use = m_next
    acc_ref[...] = acc

    @pl.when(jnp.max(excess) > MLA_MAX_EXCESS)
    def _():
        m_ref[...] = jnp.full(m_ref.shape, -jnp.inf, jnp.float32)
        acc_ref[...] = jnp.zeros(acc_ref.shape, jnp.float32)

        def chunk(c, carry):
            s = jnp.dot(key_chunk(c), q_t, preferred_element_type=jnp.float32)
            m_prev = m_ref[...]
            m_new = jnp.maximum(m_prev, jnp.max(s, axis=0, keepdims=True))
            p = jnp.exp2(s - m_new).astype(jnp.bfloat16)
            acc_ref[...] = jnp.exp2(m_prev - m_new) * acc_ref[...] + jnp.dot(
                value_chunk(c), p, preferred_element_type=jnp.float32)
            m_ref[...] = m_new
            return carry

        lax.fori_loop(0, n_chunks, chunk, 0)

    o = acc_ref[0:MLA_V, :] / acc_ref[MLA_V:MLA_V + 1, :]
    o_ref[0] = (o * gate_ref[0].astype(jnp.float32)).astype(o_ref.dtype)


def _mla_attention(q_t, k, v_t, gate_t):
    B, H, _, S = q_t.shape
    TQ = MLA_TQ
    grid = (B, H, S // TQ)
    return pl.pallas_call(
        _mla_kernel,
        grid=grid,
        in_specs=[
            pl.BlockSpec((1, 1, QK_PAD, TQ), lambda b, h, i: (b, h, 0, i)),
            pl.BlockSpec((1, 1, S, QK_PAD), lambda b, h, i: (b, h, 0, 0)),
            pl.BlockSpec((1, 1, MLA_V_ROWS, S), lambda b, h, i: (b, h, 0, 0)),
            pl.BlockSpec((1, MLA_V, TQ), lambda b, h, i: (b, h, i)),
        ],
        out_specs=pl.BlockSpec((1, MLA_V, TQ), lambda b, h, i: (b, h, i)),
        out_shape=jax.ShapeDtypeStruct((B, D_MLA, S), jnp.bfloat16),
        scratch_shapes=[pltpu.VMEM((1, TQ), jnp.float32),
                        pltpu.VMEM((MLA_V_ROWS, TQ), jnp.float32)],
        compiler_params=pltpu.CompilerParams(
            dimension_semantics=("arbitrary", "arbitrary", "arbitrary"),
            vmem_limit_bytes=VMEM_LIMIT_BYTES),
        name="mla",
    )(q_t, k, v_t, gate_t)


def _swa_kernel(qst_ref, ks_ref, vst_ref, bias_ref, sink_ref, gate_ref, o_ref):
    n_total = ks_ref.shape[2] // BLOCK
    n_local = o_ref.shape[2] // BLOCK
    step = pl.program_id(2)
    sink = sink_ref[0]
    neg_inf = jnp.float32(-jnp.inf)
    for j in range(n_local):
        n = step * n_local + j
        q_t = qst_ref[0, 0, :, j * SWA_GROUP * BLOCK:(j + 1) * SWA_GROUP * BLOCK]
        starts = (jnp.maximum(n - 1, 0), n, jnp.minimum(n + 1, n_total - 1))
        penalties = (jnp.where(n > 0, 0.0, neg_inf), None,
                     jnp.where(n < n_total - 1, 0.0, neg_inf))
        scores = []
        for part in range(3):
            start = pl.multiple_of(starts[part] * BLOCK, BLOCK)
            k = ks_ref[0, 0, pl.ds(start, BLOCK), :]
            s = jnp.dot(k, q_t, preferred_element_type=jnp.float32) + bias_ref[0, part]
            if penalties[part] is not None:
                s = s + penalties[part]
            scores.append(s)
        m = sink
        for s in scores:
            m = jnp.maximum(m, jnp.max(s, axis=0, keepdims=True))
        denom = jnp.exp(sink - m)
        acc = jnp.zeros((SWA_HEAD_DIM, SWA_GROUP * BLOCK), jnp.float32)
        for part in range(3):
            start = pl.multiple_of(starts[part] * BLOCK, BLOCK)
            e = jnp.exp(scores[part] - m)
            denom = denom + jnp.sum(e, axis=0, keepdims=True)
            v_t = vst_ref[0, 0, :, pl.ds(start, BLOCK)]
            acc = acc + jnp.dot(v_t, e.astype(jnp.bfloat16), preferred_element_type=jnp.float32)
        o = acc / denom
        for hh in range(SWA_GROUP):
            gate = gate_ref[0, hh * SWA_HEAD_DIM:(hh + 1) * SWA_HEAD_DIM,
                            j * BLOCK:(j + 1) * BLOCK].astype(jnp.float32)
            o_ref[0, hh * SWA_HEAD_DIM:(hh + 1) * SWA_HEAD_DIM, j * BLOCK:(j + 1) * BLOCK] = (
                o[:, hh * BLOCK:(hh + 1) * BLOCK] * gate).astype(o_ref.dtype)


def _swa_attention(qs_t, ks, vs_t, bias_t, sink_rows, gate_t):
    B, G, _, S = vs_t.shape
    T = SWA_TOKENS
    grid = (B, G, S // T)
    rows = SWA_GROUP * SWA_HEAD_DIM
    return pl.pallas_call(
        _swa_kernel,
        grid=grid,
        in_specs=[
            pl.BlockSpec((1, 1, SWA_HEAD_DIM, SWA_GROUP * T), lambda b, g, t: (b, g, 0, t)),
            pl.BlockSpec((1, 1, S, SWA_HEAD_DIM), lambda b, g, t: (b, g, 0, 0)),
            pl.BlockSpec((1, 1, SWA_HEAD_DIM, S), lambda b, g, t: (b, g, 0, 0)),
            pl.BlockSpec((1, 3, BLOCK, SWA_GROUP * BLOCK), lambda b, g, t: (g, 0, 0, 0)),
            pl.BlockSpec((1, 1, SWA_GROUP * BLOCK), lambda b, g, t: (g, 0, 0)),
            pl.BlockSpec((1, rows, T), lambda b, g, t: (b, g, t)),
        ],
        out_specs=pl.BlockSpec((1, rows, T), lambda b, g, t: (b, g, t)),
        out_shape=jax.ShapeDtypeStruct((B, D_SWA, S), jnp.bfloat16),
        compiler_params=pltpu.CompilerParams(
            dimension_semantics=("arbitrary", "arbitrary", "arbitrary"),
            vmem_limit_bytes=VMEM_LIMIT_BYTES),
        name="swa",
    )(qs_t, ks, vs_t, bias_t, sink_rows, gate_t)


def _out_kernel(x_ref, ma_ref, mb_ref, wa_ref, wb_ref, g_ref, b_ref, y_ref):
    out = lax.dot_general(ma_ref[0], wa_ref[...], _TN, preferred_element_type=jnp.float32)
    out = out + lax.dot_general(mb_ref[0], wb_ref[...], _TN, preferred_element_type=jnp.float32)
    h = ALPHA * x_ref[0] + out
    mu = jnp.mean(h, axis=-1, keepdims=True)
    var = jnp.mean(jnp.square(h - mu), axis=-1, keepdims=True)
    y_ref[0] = ((h - mu) * lax.rsqrt(var + 1e-5)) * g_ref[...] + b_ref[...]


def _out_project(x, mixed_a_t, mixed_b_t, w_a, w_b, ln_g, ln_b):
    B, S, D = x.shape
    T = OUT_TOKENS
    grid = (B, S // T)
    const = lambda *shape: pl.BlockSpec(shape, lambda b, t: (0,) * len(shape))
    return pl.pallas_call(
        _out_kernel,
        grid=grid,
        in_specs=[
            pl.BlockSpec((1, T, D), lambda b, t: (b, t, 0)),
            pl.BlockSpec((1, D_MLA, T), lambda b, t: (b, 0, t)),
            pl.BlockSpec((1, D_SWA, T), lambda b, t: (b, 0, t)),
            const(D_MLA, D),
            const(D_SWA, D),
            const(1, D),
            const(1, D),
        ],
        out_specs=pl.BlockSpec((1, T, D), lambda b, t: (b, t, 0)),
        out_shape=jax.ShapeDtypeStruct((B, S, D), x.dtype),
        compiler_params=pltpu.CompilerParams(
            dimension_semantics=("arbitrary", "arbitrary"),
            vmem_limit_bytes=VMEM_LIMIT_BYTES),
        name="outproj",
    )(x, mixed_a_t, mixed_b_t, w_a, w_b, ln_g, ln_b)


def _t5_bucket(rel):
    half = N_BUCKETS // 2
    ret = np.where(rel > 0, half, 0)
    n = np.abs(rel)
    max_exact = half // 2
    large = max_exact + (np.log(np.maximum(n, 1).astype(np.float32) / max_exact)
                         / np.log(MAX_DISTANCE / max_exact) * (half - max_exact)).astype(np.int32)
    large = np.minimum(large, half - 1)
    return (ret + np.where(n < max_exact, n, large)).astype(np.int32)


def _band_geometry():
    q_loc = np.arange(BLOCK)
    k_loc = np.arange(3 * BLOCK) - BLOCK
    rel = k_loc[:, None] - q_loc[None, :]
    return _t5_bucket(rel), np.abs(rel) <= WINDOW


def _col(w_in, i):
    return w_in[:, IN_OFFSETS[i]:IN_OFFSETS[i + 1]]


def kernel(x, w_in, g_q, g_kv, w_uq, w_ukv, sink, rel_bias, w_out, ln_g, ln_b):
    B, S, _ = x.shape
    bf = jnp.bfloat16
    f32 = jnp.float32

    pos = jnp.arange(S, dtype=f32)
    inv_freq = ROPE_BASE ** (-jnp.arange(0, MLA_ROPE, 2, dtype=f32) / MLA_ROPE)
    ang = pos[:, None] * inv_freq[None, :]
    cos, sin = jnp.cos(ang), jnp.sin(ang)
    half = MLA_ROPE // 2
    lane_pad = lambda a, b_: jnp.concatenate(
        [jnp.zeros((S, MLA_NOPE), f32), a, b_, jnp.zeros((S, LANE - MLA_QK), f32)], axis=1)
    cos_n = lane_pad(cos, cos)
    sin_n = lane_pad(sin, sin)

    c_q, c_kv, k_rope, gate_a, q_s, k_s, v_s, gate_b = (_col(w_in, i) for i in range(8))
    w_tr = jnp.concatenate([c_q, gate_a, q_s, v_s, gate_b], axis=1).T.astype(bf)
    zeros = lambda n: jnp.zeros((D_MODEL, n), f32)
    kr_pad = jnp.concatenate([zeros(MLA_NOPE), k_rope, zeros(LANE - MLA_QK)], axis=1)
    kr_rot = jnp.concatenate([zeros(MLA_NOPE), -k_rope[:, half:], k_rope[:, :half],
                              zeros(LANE - MLA_QK)], axis=1)
    w_nat = jnp.concatenate([c_kv, k_s, kr_pad, kr_rot], axis=1).astype(bf)
    w_ukv3 = w_ukv.reshape(KV_LORA, MLA_HEADS, MLA_NOPE + MLA_V)
    w_uk = jnp.pad(w_ukv3[:, :, :MLA_NOPE], ((0, 0), (0, 0), (0, QK_PAD - MLA_NOPE)))
    w_uk = w_uk.reshape(KV_LORA, MLA_HEADS * QK_PAD).astype(bf)
    w_uv_t = w_ukv3[:, :, MLA_NOPE:].reshape(KV_LORA, D_MLA).T.astype(bf)
    w_uq_t = w_uq.T.astype(bf)

    q_t, k, v_t, gate_a_t, qs_t, ks, vs_t, gate_b_t = _project(
        x, w_tr, w_nat, g_q.reshape(Q_LORA, 1), g_kv.reshape(1, KV_LORA),
        w_uq_t, w_uk, w_uv_t, cos.T, sin.T, cos_n, sin_n)

    mixed_a_t = _mla_attention(q_t, k, v_t, gate_a_t)

    bucket, band = _band_geometry()
    bias = jnp.where(band[:, :, None], rel_bias.astype(f32)[bucket], -jnp.inf)
    bias = bias.reshape(3, BLOCK, BLOCK, SWA_KV_HEADS, SWA_GROUP)
    bias_t = bias.transpose(3, 0, 1, 4, 2).reshape(SWA_KV_HEADS, 3, BLOCK, SWA_GROUP * BLOCK)
    sink_rows = jnp.repeat(sink.astype(f32).reshape(SWA_KV_HEADS, 1, SWA_GROUP), BLOCK, axis=2)

    mixed_b_t = _swa_attention(qs_t, ks, vs_t, bias_t, sink_rows, gate_b_t)

    w_out_bf = w_out.astype(bf)
    return _out_project(x, mixed_a_t, mixed_b_t, w_out_bf[:D_MLA], w_out_bf[D_MLA:],
                        ln_g.reshape(1, D_MODEL), ln_b.reshape(1, D_MODEL))
```

```python
import functools
import math

import jax
import jax.numpy as jnp
import numpy as np
from jax import lax
from jax.experimental import pallas as pl
from jax.experimental.pallas import tpu as pltpu

D_MODEL = 1024
MLA_HEADS = 8
MLA_NOPE = 64
MLA_ROPE = 32
MLA_V = 64
MLA_QK = MLA_NOPE + MLA_ROPE
Q_LORA = 256
KV_LORA = 128
D_MLA = MLA_HEADS * MLA_V
MLA_SCALE = 1.0 / math.sqrt(MLA_QK)
LOG2_E = math.log2(math.e)
MLA_Q_SCALE = MLA_SCALE * LOG2_E
BF16_SUBLANES = 16
MLA_V_ROWS = MLA_V + BF16_SUBLANES
ROPE_BASE = 10000.0

SWA_HEADS = 8
SWA_KV_HEADS = 2
SWA_HEAD_DIM = 64
SWA_GROUP = SWA_HEADS // SWA_KV_HEADS
D_SWA = SWA_HEADS * SWA_HEAD_DIM
WINDOW = 128
BLOCK = 128
SWA_SCALE = 1.0 / math.sqrt(SWA_HEAD_DIM)
SWA_Q_SCALE = SWA_SCALE * LOG2_E
SWA_V_ROWS = SWA_HEAD_DIM + BF16_SUBLANES
N_BUCKETS = 32
MAX_DISTANCE = 128

DEPTH = 1
ALPHA = (2.0 * DEPTH) ** 0.25

IN_SPLITS = (Q_LORA, KV_LORA, MLA_ROPE, D_MLA, D_SWA,
             SWA_KV_HEADS * SWA_HEAD_DIM, SWA_KV_HEADS * SWA_HEAD_DIM, D_SWA)
IN_OFFSETS = tuple(int(o) for o in np.cumsum((0,) + IN_SPLITS))

LANE = 128
QK_PAD = LANE
VMEM_LIMIT_BYTES = 56 * 1024 * 1024

PROJ_TOKENS = 512
MLA_TQ = 512
MLA_TK = 512
MLA_MAX_EXCESS = 64.0
SWA_TOKENS = 1024
OUT_TOKENS = 512

TR_CQ = 0
TR_GA = TR_CQ + Q_LORA
TR_QS = TR_GA + D_MLA
TR_VS = TR_QS + D_SWA
TR_GB = TR_VS + SWA_KV_HEADS * SWA_HEAD_DIM
TR_ROWS = TR_GB + D_SWA
NAT_CKV = 0
NAT_KS = NAT_CKV + KV_LORA
NAT_KR = NAT_KS + SWA_KV_HEADS * SWA_HEAD_DIM
NAT_KR_ROT = NAT_KR + LANE
NAT_COLS = NAT_KR_ROT + LANE

_NT = (((1,), (1,)), ((), ()))
_TN = (((0,), (0,)), ((), ()))


def _rsqrt_mean_sq(x, axis, eps):
    return lax.rsqrt(jnp.mean(x * x, axis=axis, keepdims=True) + eps)


def _proj_kernel(x_ref, wtr_ref, wnat_ref, gq_ref, gkv_ref, wuqt_ref, wuk_ref, wuvt_ref,
                 cos_t_ref, sin_t_ref, cos_n_ref, sin_n_ref,
                 qt_ref, k_ref, vt_ref, gat_ref, qst_ref, ks_ref, vst_ref, gbt_ref,
                 tr_ref, nat_ref):
    xb = x_ref[0].astype(jnp.bfloat16)
    tr_ref[...] = lax.dot_general(wtr_ref[...], xb, _NT, preferred_element_type=jnp.float32)
    nat_ref[...] = jnp.dot(xb, wnat_ref[...], preferred_element_type=jnp.float32)

    cq = tr_ref[TR_CQ:TR_CQ + Q_LORA, :]
    cqn = (cq * _rsqrt_mean_sq(cq, 0, 1e-6)) * gq_ref[...]
    q_t = jnp.dot(wuqt_ref[...], cqn.astype(jnp.bfloat16),
                  preferred_element_type=jnp.float32)
    cos_t = cos_t_ref[...]
    sin_t = sin_t_ref[...]
    half = MLA_ROPE // 2
    zeros_pad = jnp.zeros((QK_PAD - MLA_QK, q_t.shape[1]), jnp.bfloat16)
    for h in range(MLA_HEADS):
        base = h * MLA_QK
        nope = q_t[base:base + MLA_NOPE]
        r1 = q_t[base + MLA_NOPE:base + MLA_NOPE + half]
        r2 = q_t[base + MLA_NOPE + half:base + MLA_QK]
        qt_ref[0, h, 0:MLA_NOPE, :] = (nope * MLA_Q_SCALE).astype(jnp.bfloat16)
        qt_ref[0, h, MLA_NOPE:MLA_NOPE + half, :] = (
            (r1 * cos_t - r2 * sin_t) * MLA_Q_SCALE).astype(jnp.bfloat16)
        qt_ref[0, h, MLA_NOPE + half:MLA_QK, :] = (
            (r2 * cos_t + r1 * sin_t) * MLA_Q_SCALE).astype(jnp.bfloat16)
        qt_ref[0, h, MLA_QK:QK_PAD, :] = zeros_pad

    ckv = nat_ref[:, NAT_CKV:NAT_CKV + KV_LORA]
    kvn = ((ckv * _rsqrt_mean_sq(ckv, 1, 1e-6)) * gkv_ref[...]).astype(jnp.bfloat16)
    k_nope = jnp.dot(kvn, wuk_ref[...], preferred_element_type=jnp.float32)
    v_t = lax.dot_general(wuvt_ref[...], kvn, _NT, preferred_element_type=jnp.float32)
    k_rope = (nat_ref[:, NAT_KR:NAT_KR + LANE] * cos_n_ref[...]
              + nat_ref[:, NAT_KR_ROT:NAT_KR_ROT + LANE] * sin_n_ref[...])
    row_id = lax.broadcasted_iota(jnp.int32, (BF16_SUBLANES, v_t.shape[1]), 0)
    ones_row = jnp.where(row_id == 0, 1.0, 0.0).astype(jnp.bfloat16)
    for h in range(MLA_HEADS):
        k_ref[0, h] = (k_nope[:, h * QK_PAD:(h + 1) * QK_PAD] + k_rope).astype(jnp.bfloat16)
        vt_ref[0, h, 0:MLA_V, :] = v_t[h * MLA_V:(h + 1) * MLA_V].astype(jnp.bfloat16)
        vt_ref[0, h, MLA_V:MLA_V_ROWS, :] = ones_row

    gat_ref[0] = jax.nn.silu(tr_ref[TR_GA:TR_GA + D_MLA, :]).astype(jnp.bfloat16)
    gbt_ref[0] = jax.nn.silu(tr_ref[TR_GB:TR_GB + D_SWA, :]).astype(jnp.bfloat16)

    n_blocks = x_ref.shape[1] // BLOCK
    for g in range(SWA_KV_HEADS):
        ks_ref[0, g] = nat_ref[:, NAT_KS + g * SWA_HEAD_DIM:
                               NAT_KS + (g + 1) * SWA_HEAD_DIM].astype(jnp.bfloat16)
        vst_ref[0, g, 0:SWA_HEAD_DIM, :] = tr_ref[TR_VS + g * SWA_HEAD_DIM:
                                                  TR_VS + (g + 1) * SWA_HEAD_DIM, :].astype(jnp.bfloat16)
        vst_ref[0, g, SWA_HEAD_DIM:SWA_V_ROWS, :] = ones_row
        for hh in range(SWA_GROUP):
            row = TR_QS + (g * SWA_GROUP + hh) * SWA_HEAD_DIM
            q_h = (tr_ref[row:row + SWA_HEAD_DIM, :] * SWA_Q_SCALE).astype(jnp.bfloat16)
            for nb in range(n_blocks):
                col = (nb * SWA_GROUP + hh) * BLOCK
                qst_ref[0, g, :, col:col + BLOCK] = q_h[:, nb * BLOCK:(nb + 1) * BLOCK]


def _project(x, w_tr, w_nat, g_q, g_kv, w_uq_t, w_uk, w_uv_t, cos_t, sin_t, cos_n, sin_n):
    B, S, _ = x.shape
    T = PROJ_TOKENS
    grid = (B, S // T)
    const = lambda *shape: pl.BlockSpec(shape, lambda b, t: (0,) * len(shape))
    bf = jnp.bfloat16
    out_shape = (
        jax.ShapeDtypeStruct((B, MLA_HEADS, QK_PAD, S), bf),
        jax.ShapeDtypeStruct((B, MLA_HEADS, S, QK_PAD), bf),
        jax.ShapeDtypeStruct((B, MLA_HEADS, MLA_V_ROWS, S), bf),
        jax.ShapeDtypeStruct((B, D_MLA, S), bf),
        jax.ShapeDtypeStruct((B, SWA_KV_HEADS, SWA_HEAD_DIM, SWA_GROUP * S), bf),
        jax.ShapeDtypeStruct((B, SWA_KV_HEADS, S, SWA_HEAD_DIM), bf),
        jax.ShapeDtypeStruct((B, SWA_KV_HEADS, SWA_V_ROWS, S), bf),
        jax.ShapeDtypeStruct((B, D_SWA, S), bf),
    )
    out_specs = (
        pl.BlockSpec((1, MLA_HEADS, QK_PAD, T), lambda b, t: (b, 0, 0, t)),
        pl.BlockSpec((1, MLA_HEADS, T, QK_PAD), lambda b, t: (b, 0, t, 0)),
        pl.BlockSpec((1, MLA_HEADS, MLA_V_ROWS, T), lambda b, t: (b, 0, 0, t)),
        pl.BlockSpec((1, D_MLA, T), lambda b, t: (b, 0, t)),
        pl.BlockSpec((1, SWA_KV_HEADS, SWA_HEAD_DIM, SWA_GROUP * T), lambda b, t: (b, 0, 0, t)),
        pl.BlockSpec((1, SWA_KV_HEADS, T, SWA_HEAD_DIM), lambda b, t: (b, 0, t, 0)),
        pl.BlockSpec((1, SWA_KV_HEADS, SWA_V_ROWS, T), lambda b, t: (b, 0, 0, t)),
        pl.BlockSpec((1, D_SWA, T), lambda b, t: (b, 0, t)),
    )
    in_specs = [
        pl.BlockSpec((1, T, D_MODEL), lambda b, t: (b, t, 0)),
        const(TR_ROWS, D_MODEL),
        const(D_MODEL, NAT_COLS),
        const(Q_LORA, 1),
        const(1, KV_LORA),
        const(MLA_HEADS * MLA_QK, Q_LORA),
        const(KV_LORA, MLA_HEADS * QK_PAD),
        const(D_MLA, KV_LORA),
        pl.BlockSpec((MLA_ROPE // 2, T), lambda b, t: (0, t)),
        pl.BlockSpec((MLA_ROPE // 2, T), lambda b, t: (0, t)),
        pl.BlockSpec((T, LANE), lambda b, t: (t, 0)),
        pl.BlockSpec((T, LANE), lambda b, t: (t, 0)),
    ]
    return pl.pallas_call(
        _proj_kernel,
        grid=grid,
        in_specs=in_specs,
        out_specs=out_specs,
        out_shape=out_shape,
        scratch_shapes=[pltpu.VMEM((TR_ROWS, T), jnp.float32),
                        pltpu.VMEM((T, NAT_COLS), jnp.float32)],
        compiler_params=pltpu.CompilerParams(
            dimension_semantics=("arbitrary", "arbitrary"),
            vmem_limit_bytes=VMEM_LIMIT_BYTES),
        name="proj",
    )(x, w_tr, w_nat, g_q, g_kv, w_uq_t, w_uk, w_uv_t, cos_t, sin_t, cos_n, sin_n)


def _mla_kernel(qt_ref, k_ref, vt_ref, gate_ref, o_ref, m_ref, acc_ref):
    n_chunks = k_ref.shape[2] // MLA_TK
    q_t = qt_ref[0, 0]

    def chunk_slice(c):
        start = c * MLA_TK
        return pl.ds(start if isinstance(c, int) else pl.multiple_of(start, MLA_TK), MLA_TK)

    def key_chunk(c):
        return k_ref[0, 0, chunk_slice(c), :]

    def value_chunk(c):
        return vt_ref[0, 0, :, chunk_slice(c)]

    s0 = jnp.dot(k_ref[0, 0, 0:BF16_SUBLANES, :], q_t, preferred_element_type=jnp.float32)
    m_use = jnp.max(s0, axis=0, keepdims=True)
    excess = jnp.zeros_like(m_use)
    acc = None
    s_next = jnp.dot(key_chunk(0), q_t, preferred_element_type=jnp.float32)
    for c in range(n_chunks):
        s = s_next
        if c + 1 < n_chunks:
            s_next = jnp.dot(key_chunk(c + 1), q_t, preferred_element_type=jnp.float32)
        p = jnp.exp2(s - m_use).astype(jnp.bfloat16)
        pv = jnp.dot(value_chunk(c), p, preferred_element_type=jnp.float32)
        acc = pv if acc is None else acc + pv
        cmax = jnp.max(s, axis=0, keepdims=True)
        excess = jnp.maximum(excess, cmax - m_use)
        if c + 1 < n_chunks:
            m_next = jnp.maximum(m_use, cmax)
            acc = acc * jnp.exp2(m_use - m_next)
            m_use = m_next
    acc_ref[...] = acc

    @pl.when(jnp.max(excess) > MLA_MAX_EXCESS)
    def _():
        m_ref[...] = jnp.full(m_ref.shape, -jnp.inf, jnp.float32)
        acc_ref[...] = jnp.zeros(acc_ref.shape, jnp.float32)

        def chunk(c, carry):
            s = jnp.dot(key_chunk(c), q_t, preferred_element_type=jnp.float32)
            m_prev = m_ref[...]
            m_new = jnp.maximum(m_prev, jnp.max(s, axis=0, keepdims=True))
            p = jnp.exp2(s - m_new).astype(jnp.bfloat16)
            acc_ref[...] = jnp.exp2(m_prev - m_new) * acc_ref[...] + jnp.dot(
                value_chunk(c), p, preferred_element_type=jnp.float32)
            m_ref[...] = m_new
            return carry

        lax.fori_loop(0, n_chunks, chunk, 0)

    o = acc_ref[0:MLA_V, :] / acc_ref[MLA_V:MLA_V + 1, :]
    o_ref[0] = (o * gate_ref[0].astype(jnp.float32)).astype(o_ref.dtype)


def _mla_attention(q_t, k, v_t, gate_t):
    B, H, _, S = q_t.shape
    TQ = MLA_TQ
    grid = (B, H, S // TQ)
    return pl.pallas_call(
        _mla_kernel,
        grid=grid,
        in_specs=[
            pl.BlockSpec((1, 1, QK_PAD, TQ), lambda b, h, i: (b, h, 0, i)),
            pl.BlockSpec((1, 1, S, QK_PAD), lambda b, h, i: (b, h, 0, 0)),
            pl.BlockSpec((1, 1, MLA_V_ROWS, S), lambda b, h, i: (b, h, 0, 0)),
            pl.BlockSpec((1, MLA_V, TQ), lambda b, h, i: (b, h, i)),
        ],
        out_specs=pl.BlockSpec((1, MLA_V, TQ), lambda b, h, i: (b, h, i)),
        out_shape=jax.ShapeDtypeStruct((B, D_MLA, S), jnp.bfloat16),
        scratch_shapes=[pltpu.VMEM((1, TQ), jnp.float32),
                        pltpu.VMEM((MLA_V_ROWS, TQ), jnp.float32)],
        compiler_params=pltpu.CompilerParams(
            dimension_semantics=("arbitrary", "arbitrary", "arbitrary"),
            vmem_limit_bytes=VMEM_LIMIT_BYTES),
        name="mla",
    )(q_t, k, v_t, gate_t)


def _bias_kernel(rel_ref, bucket_ref, o_ref):
    g = pl.program_id(0)
    for part in range(3):
        bucket = bucket_ref[part]
        for hh in range(SWA_GROUP):
            head = g * SWA_GROUP + hh
            tile = jnp.full((BLOCK, BLOCK), -jnp.inf, jnp.float32)
            for b in range(N_BUCKETS):
                tile = jnp.where(bucket == b, rel_ref[b, head] * LOG2_E, tile)
            o_ref[0, part, :, hh * BLOCK:(hh + 1) * BLOCK] = tile


def _bias_table(rel_bias, bucket):
    return pl.pallas_call(
        _bias_kernel,
        grid=(SWA_KV_HEADS,),
        in_specs=[pl.BlockSpec(memory_space=pltpu.SMEM),
                  pl.BlockSpec((3, BLOCK, BLOCK), lambda g: (0, 0, 0))],
        out_specs=pl.BlockSpec((1, 3, BLOCK, SWA_GROUP * BLOCK), lambda g: (g, 0, 0, 0)),
        out_shape=jax.ShapeDtypeStruct((SWA_KV_HEADS, 3, BLOCK, SWA_GROUP * BLOCK), jnp.float32),
        compiler_params=pltpu.CompilerParams(dimension_semantics=("arbitrary",)),
        name="bias",
    )(rel_bias, bucket)


def _swa_kernel(qst_ref, ks_ref, vst_ref, bias_ref, sink_ref, gate_ref, o_ref):
    n_total = ks_ref.shape[2] // BLOCK
    n_local = o_ref.shape[2] // BLOCK
    step = pl.program_id(2)
    n_steps = pl.num_programs(2)
    sink = sink_ref[0] * LOG2_E
    neg_inf = jnp.float32(-jnp.inf)

    def band_starts(j):
        n = step * n_local + j
        return tuple(pl.multiple_of(b * BLOCK, BLOCK)
                     for b in (jnp.maximum(n - 1, 0), n, jnp.minimum(n + 1, n_total - 1)))

    def band_scores(j):
        q_t = qst_ref[0, 0, :, j * SWA_GROUP * BLOCK:(j + 1) * SWA_GROUP * BLOCK]
        penalties = [None, None, None]
        if j == 0:
            penalties[0] = jnp.where(step > 0, 0.0, neg_inf)
        if j == n_local - 1:
            penalties[2] = jnp.where(step < n_steps - 1, 0.0, neg_inf)
        scores = []
        for part, start in enumerate(band_starts(j)):
            k = ks_ref[0, 0, pl.ds(start, BLOCK), :]
            s = jnp.dot(k, q_t, preferred_element_type=jnp.float32) + bias_ref[0, part]
            if penalties[part] is not None:
                s = s + penalties[part]
            scores.append(s)
        return scores

    scores_next = band_scores(0)
    for j in range(n_local):
        scores = scores_next
        if j + 1 < n_local:
            scores_next = band_scores(j + 1)
        m = sink
        for s in scores:
            m = jnp.maximum(m, jnp.max(s, axis=0, keepdims=True))
        acc = None
        for part, start in enumerate(band_starts(j)):
            e = jnp.exp2(scores[part] - m).astype(jnp.bfloat16)
            v_t = vst_ref[0, 0, :, pl.ds(start, BLOCK)]
            pv = jnp.dot(v_t, e, preferred_element_type=jnp.float32)
            acc = pv if acc is None else acc + pv
        denom = acc[SWA_HEAD_DIM:SWA_HEAD_DIM + 1] + jnp.exp2(sink - m)
        o = acc[0:SWA_HEAD_DIM] / denom
        for hh in range(SWA_GROUP):
            gate = gate_ref[0, hh * SWA_HEAD_DIM:(hh + 1) * SWA_HEAD_DIM,
                            j * BLOCK:(j + 1) * BLOCK].astype(jnp.float32)
            o_ref[0, hh * SWA_HEAD_DIM:(hh + 1) * SWA_HEAD_DIM, j * BLOCK:(j + 1) * BLOCK] = (
                o[:, hh * BLOCK:(hh + 1) * BLOCK] * gate).astype(o_ref.dtype)


def _swa_attention(qs_t, ks, vs_t, bias_t, sink_rows, gate_t):
    B, G, _, S = vs_t.shape
    T = SWA_TOKENS
    grid = (B, G, S // T)
    rows = SWA_GROUP * SWA_HEAD_DIM
    return pl.pallas_call(
        _swa_kernel,
        grid=grid,
        in_specs=[
            pl.BlockSpec((1, 1, SWA_HEAD_DIM, SWA_GROUP * T), lambda b, g, t: (b, g, 0, t)),
            pl.BlockSpec((1, 1, S, SWA_HEAD_DIM), lambda b, g, t: (b, g, 0, 0)),
            pl.BlockSpec((1, 1, SWA_V_ROWS, S), lambda b, g, t: (b, g, 0, 0)),
            pl.BlockSpec((1, 3, BLOCK, SWA_GROUP * BLOCK), lambda b, g, t: (g, 0, 0, 0)),
            pl.BlockSpec((1, 1, SWA_GROUP * BLOCK), lambda b, g, t: (g, 0, 0)),
            pl.BlockSpec((1, rows, T), lambda b, g, t: (b, g, t)),
        ],
        out_specs=pl.BlockSpec((1, rows, T), lambda b, g, t: (b, g, t)),
        out_shape=jax.ShapeDtypeStruct((B, D_SWA, S), jnp.bfloat16),
        compiler_params=pltpu.CompilerParams(
            dimension_semantics=("arbitrary", "arbitrary", "arbitrary"),
            vmem_limit_bytes=VMEM_LIMIT_BYTES),
        name="swa",
    )(qs_t, ks, vs_t, bias_t, sink_rows, gate_t)


def _out_kernel(x_ref, ma_ref, mb_ref, wa_ref, wb_ref, g_ref, b_ref, y_ref):
    out = lax.dot_general(ma_ref[0], wa_ref[...], _TN, preferred_element_type=jnp.float32)
    out = out + lax.dot_general(mb_ref[0], wb_ref[...], _TN, preferred_element_type=jnp.float32)
    h = ALPHA * x_ref[0] + out
    mu = jnp.mean(h, axis=-1, keepdims=True)
    var = jnp.mean(jnp.square(h - mu), axis=-1, keepdims=True)
    y_ref[0] = ((h - mu) * lax.rsqrt(var + 1e-5)) * g_ref[...] + b_ref[...]


def _out_project(x, mixed_a_t, mixed_b_t, w_a, w_b, ln_g, ln_b):
    B, S, D = x.shape
    T = OUT_TOKENS
    grid = (B, S // T)
    const = lambda *shape: pl.BlockSpec(shape, lambda b, t: (0,) * len(shape))
    return pl.pallas_call(
        _out_kernel,
        grid=grid,
        in_specs=[
            pl.BlockSpec((1, T, D), lambda b, t: (b, t, 0)),
            pl.BlockSpec((1, D_MLA, T), lambda b, t: (b, 0, t)),
            pl.BlockSpec((1, D_SWA, T), lambda b, t: (b, 0, t)),
            const(D_MLA, D),
            const(D_SWA, D),
            const(1, D),
            const(1, D),
        ],
        out_specs=pl.BlockSpec((1, T, D), lambda b, t: (b, t, 0)),
        out_shape=jax.ShapeDtypeStruct((B, S, D), x.dtype),
        compiler_params=pltpu.CompilerParams(
            dimension_semantics=("arbitrary", "arbitrary"),
            vmem_limit_bytes=VMEM_LIMIT_BYTES),
        name="outproj",
    )(x, mixed_a_t, mixed_b_t, w_a, w_b, ln_g, ln_b)


def _t5_bucket(rel):
    half = N_BUCKETS // 2
    ret = np.where(rel > 0, half, 0)
    n = np.abs(rel)
    max_exact = half // 2
    large = max_exact + (np.log(np.maximum(n, 1).astype(np.float32) / max_exact)
                         / np.log(MAX_DISTANCE / max_exact) * (half - max_exact)).astype(np.int32)
    large = np.minimum(large, half - 1)
    return (ret + np.where(n < max_exact, n, large)).astype(np.int32)


def _band_geometry():
    q_loc = np.arange(BLOCK)
    k_loc = np.arange(3 * BLOCK) - BLOCK
    rel = k_loc[:, None] - q_loc[None, :]
    return _t5_bucket(rel), np.abs(rel) <= WINDOW


def _col(w_in, i):
    return w_in[:, IN_OFFSETS[i]:IN_OFFSETS[i + 1]]


def kernel(x, w_in, g_q, g_kv, w_uq, w_ukv, sink, rel_bias, w_out, ln_g, ln_b):
    B, S, _ = x.shape
    bf = jnp.bfloat16
    f32 = jnp.float32

    pos = jnp.arange(S, dtype=f32)
    inv_freq = ROPE_BASE ** (-jnp.arange(0, MLA_ROPE, 2, dtype=f32) / MLA_ROPE)
    ang = pos[:, None] * inv_freq[None, :]
    cos, sin = jnp.cos(ang), jnp.sin(ang)
    half = MLA_ROPE // 2
    lane_pad = lambda a, b_: jnp.concatenate(
        [jnp.zeros((S, MLA_NOPE), f32), a, b_, jnp.zeros((S, LANE - MLA_QK), f32)], axis=1)
    cos_n = lane_pad(cos, cos)
    sin_n = lane_pad(sin, sin)

    c_q, c_kv, k_rope, gate_a, q_s, k_s, v_s, gate_b = (_col(w_in, i) for i in range(8))
    w_tr = jnp.concatenate([c_q, gate_a, q_s, v_s, gate_b], axis=1).T.astype(bf)
    zeros = lambda n: jnp.zeros((D_MODEL, n), f32)
    kr_pad = jnp.concatenate([zeros(MLA_NOPE), k_rope, zeros(LANE - MLA_QK)], axis=1)
    kr_rot = jnp.concatenate([zeros(MLA_NOPE), -k_rope[:, half:], k_rope[:, :half],
                              zeros(LANE - MLA_QK)], axis=1)
    w_nat = jnp.concatenate([c_kv, k_s, kr_pad, kr_rot], axis=1).astype(bf)
    w_ukv3 = w_ukv.reshape(KV_LORA, MLA_HEADS, MLA_NOPE + MLA_V)
    w_uk = jnp.pad(w_ukv3[:, :, :MLA_NOPE], ((0, 0), (0, 0), (0, QK_PAD - MLA_NOPE)))
    w_uk = w_uk.reshape(KV_LORA, MLA_HEADS * QK_PAD).astype(bf)
    w_uv_t = w_ukv3[:, :, MLA_NOPE:].reshape(KV_LORA, D_MLA).T.astype(bf)
    w_uq_t = w_uq.T.astype(bf)

    q_t, k, v_t, gate_a_t, qs_t, ks, vs_t, gate_b_t = _project(
        x, w_tr, w_nat, g_q.reshape(Q_LORA, 1), g_kv.reshape(1, KV_LORA),
        w_uq_t, w_uk, w_uv_t, cos.T, sin.T, cos_n, sin_n)

    mixed_a_t = _mla_attention(q_t, k, v_t, gate_a_t)

    bucket, band = _band_geometry()
    bucket = jnp.asarray(np.where(band, bucket, -1).reshape(3, BLOCK, BLOCK), jnp.int32)
    bias_t = _bias_table(rel_bias.astype(f32), bucket)
    sink_rows = jnp.repeat(sink.astype(f32).reshape(SWA_KV_HEADS, 1, SWA_GROUP), BLOCK, axis=2)

    mixed_b_t = _swa_attention(qs_t, ks, vs_t, bias_t, sink_rows, gate_b_t)

    w_out_bf = w_out.astype(bf)
    return _out_project(x, mixed_a_t, mixed_b_t, w_out_bf[:D_MLA], w_out_bf[D_MLA:],
                        ln_g.reshape(1, D_MODEL), ln_b.reshape(1, D_MODEL))
```

```python
import functools
import math

import jax
import jax.numpy as jnp
import numpy as np
from jax import lax
from jax.experimental import pallas as pl
from jax.experimental.pallas import tpu as pltpu

D_MODEL = 1024
MLA_HEADS = 8
MLA_NOPE = 64
MLA_ROPE = 32
MLA_V = 64
MLA_QK = MLA_NOPE + MLA_ROPE
Q_LORA = 256
KV_LORA = 128
D_MLA = MLA_HEADS * MLA_V
MLA_SCALE = 1.0 / math.sqrt(MLA_QK)
LOG2_E = math.log2(math.e)
MLA_Q_SCALE = MLA_SCALE * LOG2_E
BF16_SUBLANES = 16
MLA_V_ROWS = MLA_V + BF16_SUBLANES
ROPE_BASE = 10000.0

SWA_HEADS = 8
SWA_KV_HEADS = 2
SWA_HEAD_DIM = 64
SWA_GROUP = SWA_HEADS // SWA_KV_HEADS
D_SWA = SWA_HEADS * SWA_HEAD_DIM
WINDOW = 128
BLOCK = 128
SWA_SCALE = 1.0 / math.sqrt(SWA_HEAD_DIM)
SWA_Q_SCALE = SWA_SCALE * LOG2_E
SWA_V_ROWS = SWA_HEAD_DIM + BF16_SUBLANES
N_BUCKETS = 32
MAX_DISTANCE = 128

DEPTH = 1
ALPHA = (2.0 * DEPTH) ** 0.25

IN_SPLITS = (Q_LORA, KV_LORA, MLA_ROPE, D_MLA, D_SWA,
             SWA_KV_HEADS * SWA_HEAD_DIM, SWA_KV_HEADS * SWA_HEAD_DIM, D_SWA)
IN_OFFSETS = tuple(int(o) for o in np.cumsum((0,) + IN_SPLITS))

LANE = 128
QK_PAD = LANE
VMEM_LIMIT_BYTES = 56 * 1024 * 1024

PROJ_TOKENS = 512
MLA_TQ = 512
MLA_STEP_TILES = 2
MLA_TK = 256
MLA_AHEAD = 2
MLA_MAX_EXCESS = 64.0
SWA_TOKENS = 1024
OUT_TOKENS = 512

TR_CQ = 0
TR_GA = TR_CQ + Q_LORA
TR_QS = TR_GA + D_MLA
TR_VS = TR_QS + D_SWA
TR_GB = TR_VS + SWA_KV_HEADS * SWA_HEAD_DIM
TR_ROWS = TR_GB + D_SWA
NAT_CKV = 0
NAT_KS = NAT_CKV + KV_LORA
NAT_KR = NAT_KS + SWA_KV_HEADS * SWA_HEAD_DIM
NAT_KR_ROT = NAT_KR + LANE
NAT_COLS = NAT_KR_ROT + LANE

_NT = (((1,), (1,)), ((), ()))
_TN = (((0,), (0,)), ((), ()))


def _rsqrt_mean_sq(x, axis, eps):
    return lax.rsqrt(jnp.mean(x * x, axis=axis, keepdims=True) + eps)


def _proj_kernel(x_ref, wtr_ref, wnat_ref, gq_ref, gkv_ref, wuqt_ref, wuk_ref, wuvt_ref,
                 cos_t_ref, sin_t_ref, cos_n_ref, sin_n_ref,
                 qt_ref, k_ref, vt_ref, gat_ref, qst_ref, ks_ref, vst_ref, gbt_ref,
                 tr_ref, nat_ref):
    xb = x_ref[0].astype(jnp.bfloat16)
    tr_ref[...] = lax.dot_general(wtr_ref[...], xb, _NT, preferred_element_type=jnp.float32)
    nat_ref[...] = jnp.dot(xb, wnat_ref[...], preferred_element_type=jnp.float32)

    cq = tr_ref[TR_CQ:TR_CQ + Q_LORA, :]
    cqn = (cq * _rsqrt_mean_sq(cq, 0, 1e-6)) * gq_ref[...]
    q_t = jnp.dot(wuqt_ref[...], cqn.astype(jnp.bfloat16),
                  preferred_element_type=jnp.float32)
    cos_t = cos_t_ref[...]
    sin_t = sin_t_ref[...]
    half = MLA_ROPE // 2
    zeros_pad = jnp.zeros((QK_PAD - MLA_QK, q_t.shape[1]), jnp.bfloat16)
    for h in range(MLA_HEADS):
        base = h * MLA_QK
        nope = q_t[base:base + MLA_NOPE]
        r1 = q_t[base + MLA_NOPE:base + MLA_NOPE + half]
        r2 = q_t[base + MLA_NOPE + half:base + MLA_QK]
        qt_ref[0, h, 0:MLA_NOPE, :] = (nope * MLA_Q_SCALE).astype(jnp.bfloat16)
        qt_ref[0, h, MLA_NOPE:MLA_NOPE + half, :] = (
            (r1 * cos_t - r2 * sin_t) * MLA_Q_SCALE).astype(jnp.bfloat16)
        qt_ref[0, h, MLA_NOPE + half:MLA_QK, :] = (
            (r2 * cos_t + r1 * sin_t) * MLA_Q_SCALE).astype(jnp.bfloat16)
        qt_ref[0, h, MLA_QK:QK_PAD, :] = zeros_pad

    ckv = nat_ref[:, NAT_CKV:NAT_CKV + KV_LORA]
    kvn = ((ckv * _rsqrt_mean_sq(ckv, 1, 1e-6)) * gkv_ref[...]).astype(jnp.bfloat16)
    k_nope = jnp.dot(kvn, wuk_ref[...], preferred_element_type=jnp.float32)
    v_t = lax.dot_general(wuvt_ref[...], kvn, _NT, preferred_element_type=jnp.float32)
    k_rope = (nat_ref[:, NAT_KR:NAT_KR + LANE] * cos_n_ref[...]
              + nat_ref[:, NAT_KR_ROT:NAT_KR_ROT + LANE] * sin_n_ref[...])
    row_id = lax.broadcasted_iota(jnp.int32, (BF16_SUBLANES, v_t.shape[1]), 0)
    ones_row = jnp.where(row_id == 0, 1.0, 0.0).astype(jnp.bfloat16)
    for h in range(MLA_HEADS):
        k_ref[0, h] = (k_nope[:, h * QK_PAD:(h + 1) * QK_PAD] + k_rope).astype(jnp.bfloat16)
        vt_ref[0, h, 0:MLA_V, :] = v_t[h * MLA_V:(h + 1) * MLA_V].astype(jnp.bfloat16)
        vt_ref[0, h, MLA_V:MLA_V_ROWS, :] = ones_row

    gat_ref[0] = jax.nn.silu(tr_ref[TR_GA:TR_GA + D_MLA, :]).astype(jnp.bfloat16)
    gbt_ref[0] = jax.nn.silu(tr_ref[TR_GB:TR_GB + D_SWA, :]).astype(jnp.bfloat16)

    n_blocks = x_ref.shape[1] // BLOCK
    for g in range(SWA_KV_HEADS):
        ks_ref[0, g] = nat_ref[:, NAT_KS + g * SWA_HEAD_DIM:
                               NAT_KS + (g + 1) * SWA_HEAD_DIM].astype(jnp.bfloat16)
        vst_ref[0, g, 0:SWA_HEAD_DIM, :] = tr_ref[TR_VS + g * SWA_HEAD_DIM:
                                                  TR_VS + (g + 1) * SWA_HEAD_DIM, :].astype(jnp.bfloat16)
        vst_ref[0, g, SWA_HEAD_DIM:SWA_V_ROWS, :] = ones_row
        for hh in range(SWA_GROUP):
            row = TR_QS + (g * SWA_GROUP + hh) * SWA_HEAD_DIM
            q_h = (tr_ref[row:row + SWA_HEAD_DIM, :] * SWA_Q_SCALE).astype(jnp.bfloat16)
            for nb in range(n_blocks):
                col = (nb * SWA_GROUP + hh) * BLOCK
                qst_ref[0, g, :, col:col + BLOCK] = q_h[:, nb * BLOCK:(nb + 1) * BLOCK]


def _project(x, w_tr, w_nat, g_q, g_kv, w_uq_t, w_uk, w_uv_t, cos_t, sin_t, cos_n, sin_n):
    B, S, _ = x.shape
    T = PROJ_TOKENS
    grid = (B, S // T)
    const = lambda *shape: pl.BlockSpec(shape, lambda b, t: (0,) * len(shape))
    bf = jnp.bfloat16
    out_shape = (
        jax.ShapeDtypeStruct((B, MLA_HEADS, QK_PAD, S), bf),
        jax.ShapeDtypeStruct((B, MLA_HEADS, S, QK_PAD), bf),
        jax.ShapeDtypeStruct((B, MLA_HEADS, MLA_V_ROWS, S), bf),
        jax.ShapeDtypeStruct((B, D_MLA, S), bf),
        jax.ShapeDtypeStruct((B, SWA_KV_HEADS, SWA_HEAD_DIM, SWA_GROUP * S), bf),
        jax.ShapeDtypeStruct((B, SWA_KV_HEADS, S, SWA_HEAD_DIM), bf),
        jax.ShapeDtypeStruct((B, SWA_KV_HEADS, SWA_V_ROWS, S), bf),
        jax.ShapeDtypeStruct((B, D_SWA, S), bf),
    )
    out_specs = (
        pl.BlockSpec((1, MLA_HEADS, QK_PAD, T), lambda b, t: (b, 0, 0, t)),
        pl.BlockSpec((1, MLA_HEADS, T, QK_PAD), lambda b, t: (b, 0, t, 0)),
        pl.BlockSpec((1, MLA_HEADS, MLA_V_ROWS, T), lambda b, t: (b, 0, 0, t)),
        pl.BlockSpec((1, D_MLA, T), lambda b, t: (b, 0, t)),
        pl.BlockSpec((1, SWA_KV_HEADS, SWA_HEAD_DIM, SWA_GROUP * T), lambda b, t: (b, 0, 0, t)),
        pl.BlockSpec((1, SWA_KV_HEADS, T, SWA_HEAD_DIM), lambda b, t: (b, 0, t, 0)),
        pl.BlockSpec((1, SWA_KV_HEADS, SWA_V_ROWS, T), lambda b, t: (b, 0, 0, t)),
        pl.BlockSpec((1, D_SWA, T), lambda b, t: (b, 0, t)),
    )
    in_specs = [
        pl.BlockSpec((1, T, D_MODEL), lambda b, t: (b, t, 0)),
        const(TR_ROWS, D_MODEL),
        const(D_MODEL, NAT_COLS),
        const(Q_LORA, 1),
        const(1, KV_LORA),
        const(MLA_HEADS * MLA_QK, Q_LORA),
        const(KV_LORA, MLA_HEADS * QK_PAD),
        const(D_MLA, KV_LORA),
        pl.BlockSpec((MLA_ROPE // 2, T), lambda b, t: (0, t)),
        pl.BlockSpec((MLA_ROPE // 2, T), lambda b, t: (0, t)),
        pl.BlockSpec((T, LANE), lambda b, t: (t, 0)),
        pl.BlockSpec((T, LANE), lambda b, t: (t, 0)),
    ]
    return pl.pallas_call(
        _proj_kernel,
        grid=grid,
        in_specs=in_specs,
        out_specs=out_specs,
        out_shape=out_shape,
        scratch_shapes=[pltpu.VMEM((TR_ROWS, T), jnp.float32),
                        pltpu.VMEM((T, NAT_COLS), jnp.float32)],
        compiler_params=pltpu.CompilerParams(
            dimension_semantics=("arbitrary", "arbitrary"),
            vmem_limit_bytes=VMEM_LIMIT_BYTES),
        name="proj",
    )(x, w_tr, w_nat, g_q, g_kv, w_uq_t, w_uk, w_uv_t, cos_t, sin_t, cos_n, sin_n)


def _mla_kernel(qt_ref, k_ref, vt_ref, gate_ref, o_ref, m_ref, acc_ref):
    n_chunks = k_ref.shape[2] // MLA_TK
    n_tiles = qt_ref.shape[3] // MLA_TQ

    def q_tile(t):
        return qt_ref[0, 0, :, t * MLA_TQ:(t + 1) * MLA_TQ]

    def chunk_slice(c):
        start = c * MLA_TK
        return pl.ds(start if isinstance(c, int) else pl.multiple_of(start, MLA_TK), MLA_TK)

    def key_chunk(c):
        return k_ref[0, 0, chunk_slice(c), :]

    def value_chunk(c):
        return vt_ref[0, 0, :, chunk_slice(c)]

    items = [(t, c) for t in range(n_tiles) for c in range(n_chunks)]
    m_use = [None] * n_tiles
    acc = [None] * n_tiles
    excess = None

    def issue_scores(i):
        t, c = items[i]
        if c == 0:
            s0 = jnp.dot(k_ref[0, 0, 0:BF16_SUBLANES, :], q_tile(t),
                         preferred_element_type=jnp.float32)
            m_use[t] = jnp.max(s0, axis=0, keepdims=True)
        return jnp.dot(key_chunk(c), q_tile(t), preferred_element_type=jnp.float32)

    pending = [issue_scores(i) for i in range(MLA_AHEAD)]
    for i, (t, c) in enumerate(items):
        if i + MLA_AHEAD < len(items):
            pending.append(issue_scores(i + MLA_AHEAD))
        s = pending.pop(0)
        p = jnp.exp2(s - m_use[t]).astype(jnp.bfloat16)
        pv = jnp.dot(value_chunk(c), p, preferred_element_type=jnp.float32)
        acc[t] = pv if c == 0 else acc[t] + pv
        cmax = jnp.max(s, axis=0, keepdims=True)
        over = cmax - m_use[t]
        excess = over if excess is None else jnp.maximum(excess, over)
        if c + 1 < n_chunks:
            m_next = jnp.maximum(m_use[t], cmax)
            acc[t] = acc[t] * jnp.exp2(m_use[t] - m_next)
            m_use[t] = m_next
        else:
            acc_ref[t] = acc[t]

    @pl.when(jnp.max(excess) > MLA_MAX_EXCESS)
    def _():
        for t in range(n_tiles):
            m_ref[...] = jnp.full(m_ref.shape, -jnp.inf, jnp.float32)
            acc_ref[t] = jnp.zeros(acc_ref.shape[1:], jnp.float32)

            def chunk(c, carry, t=t):
                s = jnp.dot(key_chunk(c), q_tile(t), preferred_element_type=jnp.float32)
                m_prev = m_ref[...]
                m_new = jnp.maximum(m_prev, jnp.max(s, axis=0, keepdims=True))
                p = jnp.exp2(s - m_new).astype(jnp.bfloat16)
                acc_ref[t] = jnp.exp2(m_prev - m_new) * acc_ref[t] + jnp.dot(
                    value_chunk(c), p, preferred_element_type=jnp.float32)
                m_ref[...] = m_new
                return carry

            lax.fori_loop(0, n_chunks, chunk, 0)

    for t in range(n_tiles):
        cols = slice(t * MLA_TQ, (t + 1) * MLA_TQ)
        o = acc_ref[t, 0:MLA_V, :] / acc_ref[t, MLA_V:MLA_V + 1, :]
        o_ref[0, :, cols] = (o * gate_ref[0, :, cols].astype(jnp.float32)).astype(o_ref.dtype)


def _mla_attention(q_t, k, v_t, gate_t):
    B, H, _, S = q_t.shape
    TQ = MLA_TQ
    step_q = MLA_STEP_TILES * TQ
    grid = (B, H, S // step_q)
    return pl.pallas_call(
        _mla_kernel,
        grid=grid,
        in_specs=[
            pl.BlockSpec((1, 1, QK_PAD, step_q), lambda b, h, i: (b, h, 0, i)),
            pl.BlockSpec((1, 1, S, QK_PAD), lambda b, h, i: (b, h, 0, 0)),
            pl.BlockSpec((1, 1, MLA_V_ROWS, S), lambda b, h, i: (b, h, 0, 0)),
            pl.BlockSpec((1, MLA_V, step_q), lambda b, h, i: (b, h, i)),
        ],
        out_specs=pl.BlockSpec((1, MLA_V, step_q), lambda b, h, i: (b, h, i)),
        out_shape=jax.ShapeDtypeStruct((B, D_MLA, S), jnp.bfloat16),
        scratch_shapes=[pltpu.VMEM((1, TQ), jnp.float32),
                        pltpu.VMEM((MLA_STEP_TILES, MLA_V_ROWS, TQ), jnp.float32)],
        compiler_params=pltpu.CompilerParams(
            dimension_semantics=("arbitrary", "arbitrary", "arbitrary"),
            vmem_limit_bytes=VMEM_LIMIT_BYTES),
        name="mla",
    )(q_t, k, v_t, gate_t)


def _bias_kernel(rel_ref, bucket_ref, o_ref):
    g = pl.program_id(0)
    for part in range(3):
        bucket = bucket_ref[part]
        for hh in range(SWA_GROUP):
            head = g * SWA_GROUP + hh
            tile = jnp.full((BLOCK, BLOCK), -jnp.inf, jnp.float32)
            for b in range(N_BUCKETS):
                tile = jnp.where(bucket == b, rel_ref[b, head] * LOG2_E, tile)
            o_ref[0, part, :, hh * BLOCK:(hh + 1) * BLOCK] = tile


def _bias_table(rel_bias, bucket):
    return pl.pallas_call(
        _bias_kernel,
        grid=(SWA_KV_HEADS,),
        in_specs=[pl.BlockSpec(memory_space=pltpu.SMEM),
                  pl.BlockSpec((3, BLOCK, BLOCK), lambda g: (0, 0, 0))],
        out_specs=pl.BlockSpec((1, 3, BLOCK, SWA_GROUP * BLOCK), lambda g: (g, 0, 0, 0)),
        out_shape=jax.ShapeDtypeStruct((SWA_KV_HEADS, 3, BLOCK, SWA_GROUP * BLOCK), jnp.float32),
        compiler_params=pltpu.CompilerParams(dimension_semantics=("arbitrary",)),
        name="bias",
    )(rel_bias, bucket)


def _swa_kernel(qst_ref, ks_ref, vst_ref, bias_ref, sink_ref, gate_ref, o_ref):
    n_total = ks_ref.shape[2] // BLOCK
    n_local = o_ref.shape[2] // BLOCK
    step = pl.program_id(2)
    n_steps = pl.num_programs(2)
    sink = sink_ref[0] * LOG2_E
    neg_inf = jnp.float32(-jnp.inf)

    def band_starts(j):
        n = step * n_local + j
        return tuple(pl.multiple_of(b * BLOCK, BLOCK)
                     for b in (jnp.maximum(n - 1, 0), n, jnp.minimum(n + 1, n_total - 1)))

    def band_scores(j):
        q_t = qst_ref[0, 0, :, j * SWA_GROUP * BLOCK:(j + 1) * SWA_GROUP * BLOCK]
        penalties = [None, None, None]
        if j == 0:
            penalties[0] = jnp.where(step > 0, 0.0, neg_inf)
        if j == n_local - 1:
            penalties[2] = jnp.where(step < n_steps - 1, 0.0, neg_inf)
        scores = []
        for part, start in enumerate(band_starts(j)):
            k = ks_ref[0, 0, pl.ds(start, BLOCK), :]
            s = jnp.dot(k, q_t, preferred_element_type=jnp.float32) + bias_ref[0, part]
            if penalties[part] is not None:
                s = s + penalties[part]
            scores.append(s)
        return scores

    scores_next = band_scores(0)
    for j in range(n_local):
        scores = scores_next
        if j + 1 < n_local:
            scores_next = band_scores(j + 1)
        m = sink
        for s in scores:
            m = jnp.maximum(m, jnp.max(s, axis=0, keepdims=True))
        acc = None
        for part, start in enumerate(band_starts(j)):
            e = jnp.exp2(scores[part] - m).astype(jnp.bfloat16)
            v_t = vst_ref[0, 0, :, pl.ds(start, BLOCK)]
            pv = jnp.dot(v_t, e, preferred_element_type=jnp.float32)
            acc = pv if acc is None else acc + pv
        denom = acc[SWA_HEAD_DIM:SWA_HEAD_DIM + 1] + jnp.exp2(sink - m)
        o = acc[0:SWA_HEAD_DIM] / denom
        for hh in range(SWA_GROUP):
            gate = gate_ref[0, hh * SWA_HEAD_DIM:(hh + 1) * SWA_HEAD_DIM,
                            j * BLOCK:(j + 1) * BLOCK].astype(jnp.float32)
            o_ref[0, hh * SWA_HEAD_DIM:(hh + 1) * SWA_HEAD_DIM, j * BLOCK:(j + 1) * BLOCK] = (
                o[:, hh * BLOCK:(hh + 1) * BLOCK] * gate).astype(o_ref.dtype)


def _swa_attention(qs_t, ks, vs_t, bias_t, sink_rows, gate_t):
    B, G, _, S = vs_t.shape
    T = SWA_TOKENS
    grid = (B, G, S // T)
    rows = SWA_GROUP * SWA_HEAD_DIM
    return pl.pallas_call(
        _swa_kernel,
        grid=grid,
        in_specs=[
            pl.BlockSpec((1, 1, SWA_HEAD_DIM, SWA_GROUP * T), lambda b, g, t: (b, g, 0, t)),
            pl.BlockSpec((1, 1, S, SWA_HEAD_DIM), lambda b, g, t: (b, g, 0, 0)),
            pl.BlockSpec((1, 1, SWA_V_ROWS, S), lambda b, g, t: (b, g, 0, 0)),
            pl.BlockSpec((1, 3, BLOCK, SWA_GROUP * BLOCK), lambda b, g, t: (g, 0, 0, 0)),
            pl.BlockSpec((1, 1, SWA_GROUP * BLOCK), lambda b, g, t: (g, 0, 0)),
            pl.BlockSpec((1, rows, T), lambda b, g, t: (b, g, t)),
        ],
        out_specs=pl.BlockSpec((1, rows, T), lambda b, g, t: (b, g, t)),
        out_shape=jax.ShapeDtypeStruct((B, D_SWA, S), jnp.bfloat16),
        compiler_params=pltpu.CompilerParams(
            dimension_semantics=("arbitrary", "arbitrary", "arbitrary"),
            vmem_limit_bytes=VMEM_LIMIT_BYTES),
        name="swa",
    )(qs_t, ks, vs_t, bias_t, sink_rows, gate_t)


def _out_kernel(x_ref, ma_ref, mb_ref, wa_ref, wb_ref, g_ref, b_ref, y_ref):
    out = lax.dot_general(ma_ref[0], wa_ref[...], _TN, preferred_element_type=jnp.float32)
    out = out + lax.dot_general(mb_ref[0], wb_ref[...], _TN, preferred_element_type=jnp.float32)
    h = ALPHA * x_ref[0] + out
    mu = jnp.mean(h, axis=-1, keepdims=True)
    var = jnp.mean(jnp.square(h - mu), axis=-1, keepdims=True)
    y_ref[0] = ((h - mu) * lax.rsqrt(var + 1e-5)) * g_ref[...] + b_ref[...]


def _out_project(x, mixed_a_t, mixed_b_t, w_a, w_b, ln_g, ln_b):
    B, S, D = x.shape
    T = OUT_TOKENS
    grid = (B, S // T)
    const = lambda *shape: pl.BlockSpec(shape, lambda b, t: (0,) * len(shape))
    return pl.pallas_call(
        _out_kernel,
        grid=grid,
        in_specs=[
            pl.BlockSpec((1, T, D), lambda b, t: (b, t, 0)),
            pl.BlockSpec((1, D_MLA, T), lambda b, t: (b, 0, t)),
            pl.BlockSpec((1, D_SWA, T), lambda b, t: (b, 0, t)),
            const(D_MLA, D),
            const(D_SWA, D),
            const(1, D),
            const(1, D),
        ],
        out_specs=pl.BlockSpec((1, T, D), lambda b, t: (b, t, 0)),
        out_shape=jax.ShapeDtypeStruct((B, S, D), x.dtype),
        compiler_params=pltpu.CompilerParams(
            dimension_semantics=("arbitrary", "arbitrary"),
            vmem_limit_bytes=VMEM_LIMIT_BYTES),
        name="outproj",
    )(x, mixed_a_t, mixed_b_t, w_a, w_b, ln_g, ln_b)


def _t5_bucket(rel):
    half = N_BUCKETS // 2
    ret = np.where(rel > 0, half, 0)
    n = np.abs(rel)
    max_exact = half // 2
    large = max_exact + (np.log(np.maximum(n, 1).astype(np.float32) / max_exact)
                         / np.log(MAX_DISTANCE / max_exact) * (half - max_exact)).astype(np.int32)
    large = np.minimum(large, half - 1)
    return (ret + np.where(n < max_exact, n, large)).astype(np.int32)


def _band_geometry():
    q_loc = np.arange(BLOCK)
    k_loc = np.arange(3 * BLOCK) - BLOCK
    rel = k_loc[:, None] - q_loc[None, :]
    return _t5_bucket(rel), np.abs(rel) <= WINDOW


def _col(w_in, i):
    return w_in[:, IN_OFFSETS[i]:IN_OFFSETS[i + 1]]


def kernel(x, w_in, g_q, g_kv, w_uq, w_ukv, sink, rel_bias, w_out, ln_g, ln_b):
    B, S, _ = x.shape
    bf = jnp.bfloat16
    f32 = jnp.float32

    pos = jnp.arange(S, dtype=f32)
    inv_freq = ROPE_BASE ** (-jnp.arange(0, MLA_ROPE, 2, dtype=f32) / MLA_ROPE)
    ang = pos[:, None] * inv_freq[None, :]
    cos, sin = jnp.cos(ang), jnp.sin(ang)
    half = MLA_ROPE // 2
    lane_pad = lambda a, b_: jnp.concatenate(
        [jnp.zeros((S, MLA_NOPE), f32), a, b_, jnp.zeros((S, LANE - MLA_QK), f32)], axis=1)
    cos_n = lane_pad(cos, cos)
    sin_n = lane_pad(sin, sin)

    c_q, c_kv, k_rope, gate_a, q_s, k_s, v_s, gate_b = (_col(w_in, i) for i in range(8))
    w_tr = jnp.concatenate([c_q, gate_a, q_s, v_s, gate_b], axis=1).T.astype(bf)
    zeros = lambda n: jnp.zeros((D_MODEL, n), f32)
    kr_pad = jnp.concatenate([zeros(MLA_NOPE), k_rope, zeros(LANE - MLA_QK)], axis=1)
    kr_rot = jnp.concatenate([zeros(MLA_NOPE), -k_rope[:, half:], k_rope[:, :half],
                              zeros(LANE - MLA_QK)], axis=1)
    w_nat = jnp.concatenate([c_kv, k_s, kr_pad, kr_rot], axis=1).astype(bf)
    w_ukv3 = w_ukv.reshape(KV_LORA, MLA_HEADS, MLA_NOPE + MLA_V)
    w_uk = jnp.pad(w_ukv3[:, :, :MLA_NOPE], ((0, 0), (0, 0), (0, QK_PAD - MLA_NOPE)))
    w_uk = w_uk.reshape(KV_LORA, MLA_HEADS * QK_PAD).astype(bf)
    w_uv_t = w_ukv3[:, :, MLA_NOPE:].reshape(KV_LORA, D_MLA).T.astype(bf)
    w_uq_t = w_uq.T.astype(bf)

    q_t, k, v_t, gate_a_t, qs_t, ks, vs_t, gate_b_t = _project(
        x, w_tr, w_nat, g_q.reshape(Q_LORA, 1), g_kv.reshape(1, KV_LORA),
        w_uq_t, w_uk, w_uv_t, cos.T, sin.T, cos_n, sin_n)

    mixed_a_t = _mla_attention(q_t, k, v_t, gate_a_t)

    bucket, band = _band_geometry()
    bucket = jnp.asarray(np.where(band, bucket, -1).reshape(3, BLOCK, BLOCK), jnp.int32)
    bias_t = _bias_table(rel_bias.astype(f32), bucket)
    sink_rows = jnp.repeat(sink.astype(f32).reshape(SWA_KV_HEADS, 1, SWA_GROUP), BLOCK, axis=2)

    mixed_b_t = _swa_attention(qs_t, ks, vs_t, bias_t, sink_rows, gate_b_t)

    w_out_bf = w_out.astype(bf)
    return _out_project(x, mixed_a_t, mixed_b_t, w_out_bf[:D_MLA], w_out_bf[D_MLA:],
                        ln_g.reshape(1, D_MODEL), ln_b.reshape(1, D_MODEL))
```

```python
import functools
import math

import jax
import jax.numpy as jnp
import numpy as np
from jax import lax
from jax.experimental import pallas as pl
from jax.experimental.pallas import tpu as pltpu

D_MODEL = 1024
MLA_HEADS = 8
MLA_NOPE = 64
MLA_ROPE = 32
MLA_V = 64
MLA_QK = MLA_NOPE + MLA_ROPE
Q_LORA = 256
KV_LORA = 128
D_MLA = MLA_HEADS * MLA_V
MLA_SCALE = 1.0 / math.sqrt(MLA_QK)
LOG2_E = math.log2(math.e)
MLA_Q_SCALE = MLA_SCALE * LOG2_E
BF16_SUBLANES = 16
MLA_V_ROWS = MLA_V + BF16_SUBLANES
ROPE_BASE = 10000.0

SWA_HEADS = 8
SWA_KV_HEADS = 2
SWA_HEAD_DIM = 64
SWA_GROUP = SWA_HEADS // SWA_KV_HEADS
D_SWA = SWA_HEADS * SWA_HEAD_DIM
WINDOW = 128
BLOCK = 128
SWA_SCALE = 1.0 / math.sqrt(SWA_HEAD_DIM)
SWA_Q_SCALE = SWA_SCALE * LOG2_E
SWA_V_ROWS = SWA_HEAD_DIM + BF16_SUBLANES
N_BUCKETS = 32
MAX_DISTANCE = 128

DEPTH = 1
ALPHA = (2.0 * DEPTH) ** 0.25

IN_SPLITS = (Q_LORA, KV_LORA, MLA_ROPE, D_MLA, D_SWA,
             SWA_KV_HEADS * SWA_HEAD_DIM, SWA_KV_HEADS * SWA_HEAD_DIM, D_SWA)
IN_OFFSETS = tuple(int(o) for o in np.cumsum((0,) + IN_SPLITS))

LANE = 128
SUBLANES = 8
QK_PAD = LANE
VMEM_LIMIT_BYTES = 56 * 1024 * 1024

PROJ_TOKENS = 512
MLA_TQ = 512
MLA_STEP_TILES = 4
MLA_TK = 256
MLA_AHEAD = 2
MLA_REFRESH = 8
MLA_MAX_EXCESS = 64.0
SWA_TOKENS = 1024
OUT_TOKENS = 512

TR_CQ = 0
TR_GA = TR_CQ + Q_LORA
TR_QS = TR_GA + D_MLA
TR_VS = TR_QS + D_SWA
TR_GB = TR_VS + SWA_KV_HEADS * SWA_HEAD_DIM
TR_ROWS = TR_GB + D_SWA
NAT_CKV = 0
NAT_KS = NAT_CKV + KV_LORA
NAT_KR = NAT_KS + SWA_KV_HEADS * SWA_HEAD_DIM
NAT_KR_ROT = NAT_KR + LANE
NAT_COLS = NAT_KR_ROT + LANE

_NT = (((1,), (1,)), ((), ()))
_TN = (((0,), (0,)), ((), ()))


def _rsqrt_mean_sq(x, axis, eps):
    return lax.rsqrt(jnp.mean(x * x, axis=axis, keepdims=True) + eps)


def _proj_kernel(x_ref, wtr_ref, wnat_ref, gq_ref, gkv_ref, wuqt_ref, wuk_ref, wuvt_ref,
                 cos_t_ref, sin_t_ref, cos_n_ref, sin_n_ref,
                 qt_ref, k_ref, vt_ref, gat_ref, qst_ref, ks_ref, vst_ref, gbt_ref,
                 tr_ref, nat_ref):
    xb = x_ref[0].astype(jnp.bfloat16)
    tr_ref[...] = lax.dot_general(wtr_ref[...], xb, _NT, preferred_element_type=jnp.float32)
    nat_ref[...] = jnp.dot(xb, wnat_ref[...], preferred_element_type=jnp.float32)

    cq = tr_ref[TR_CQ:TR_CQ + Q_LORA, :]
    cqn = (cq * _rsqrt_mean_sq(cq, 0, 1e-6)) * gq_ref[...]
    q_t = jnp.dot(wuqt_ref[...], cqn.astype(jnp.bfloat16),
                  preferred_element_type=jnp.float32)
    cos_t = cos_t_ref[...]
    sin_t = sin_t_ref[...]
    half = MLA_ROPE // 2
    zeros_pad = jnp.zeros((QK_PAD - MLA_QK, q_t.shape[1]), jnp.bfloat16)
    for h in range(MLA_HEADS):
        base = h * MLA_QK
        nope = q_t[base:base + MLA_NOPE]
        r1 = q_t[base + MLA_NOPE:base + MLA_NOPE + half]
        r2 = q_t[base + MLA_NOPE + half:base + MLA_QK]
        qt_ref[0, h, 0:MLA_NOPE, :] = (nope * MLA_Q_SCALE).astype(jnp.bfloat16)
        qt_ref[0, h, MLA_NOPE:MLA_NOPE + half, :] = (
            (r1 * cos_t - r2 * sin_t) * MLA_Q_SCALE).astype(jnp.bfloat16)
        qt_ref[0, h, MLA_NOPE + half:MLA_QK, :] = (
            (r2 * cos_t + r1 * sin_t) * MLA_Q_SCALE).astype(jnp.bfloat16)
        qt_ref[0, h, MLA_QK:QK_PAD, :] = zeros_pad

    ckv = nat_ref[:, NAT_CKV:NAT_CKV + KV_LORA]
    kvn = ((ckv * _rsqrt_mean_sq(ckv, 1, 1e-6)) * gkv_ref[...]).astype(jnp.bfloat16)
    k_nope = jnp.dot(kvn, wuk_ref[...], preferred_element_type=jnp.float32)
    v_t = lax.dot_general(wuvt_ref[...], kvn, _NT, preferred_element_type=jnp.float32)
    k_rope = (nat_ref[:, NAT_KR:NAT_KR + LANE] * cos_n_ref[...]
              + nat_ref[:, NAT_KR_ROT:NAT_KR_ROT + LANE] * sin_n_ref[...])
    row_id = lax.broadcasted_iota(jnp.int32, (BF16_SUBLANES, v_t.shape[1]), 0)
    ones_row = jnp.where(row_id == 0, 1.0, 0.0).astype(jnp.bfloat16)
    for h in range(MLA_HEADS):
        k_ref[0, h] = (k_nope[:, h * QK_PAD:(h + 1) * QK_PAD] + k_rope).astype(jnp.bfloat16)
        vt_ref[0, h, 0:MLA_V, :] = v_t[h * MLA_V:(h + 1) * MLA_V].astype(jnp.bfloat16)
        vt_ref[0, h, MLA_V:MLA_V_ROWS, :] = ones_row

    gat_ref[0] = jax.nn.silu(tr_ref[TR_GA:TR_GA + D_MLA, :]).astype(jnp.bfloat16)
    gbt_ref[0] = jax.nn.silu(tr_ref[TR_GB:TR_GB + D_SWA, :]).astype(jnp.bfloat16)

    n_blocks = x_ref.shape[1] // BLOCK
    for g in range(SWA_KV_HEADS):
        ks_ref[0, g] = nat_ref[:, NAT_KS + g * SWA_HEAD_DIM:
                               NAT_KS + (g + 1) * SWA_HEAD_DIM].astype(jnp.bfloat16)
        vst_ref[0, g, 0:SWA_HEAD_DIM, :] = tr_ref[TR_VS + g * SWA_HEAD_DIM:
                                                  TR_VS + (g + 1) * SWA_HEAD_DIM, :].astype(jnp.bfloat16)
        vst_ref[0, g, SWA_HEAD_DIM:SWA_V_ROWS, :] = ones_row
        for hh in range(SWA_GROUP):
            row = TR_QS + (g * SWA_GROUP + hh) * SWA_HEAD_DIM
            q_h = (tr_ref[row:row + SWA_HEAD_DIM, :] * SWA_Q_SCALE).astype(jnp.bfloat16)
            for nb in range(n_blocks):
                col = (nb * SWA_GROUP + hh) * BLOCK
                qst_ref[0, g, :, col:col + BLOCK] = q_h[:, nb * BLOCK:(nb + 1) * BLOCK]


def _project(x, w_tr, w_nat, g_q, g_kv, w_uq_t, w_uk, w_uv_t, cos_t, sin_t, cos_n, sin_n):
    B, S, _ = x.shape
    T = PROJ_TOKENS
    grid = (B, S // T)
    const = lambda *shape: pl.BlockSpec(shape, lambda b, t: (0,) * len(shape))
    bf = jnp.bfloat16
    out_shape = (
        jax.ShapeDtypeStruct((B, MLA_HEADS, QK_PAD, S), bf),
        jax.ShapeDtypeStruct((B, MLA_HEADS, S, QK_PAD), bf),
        jax.ShapeDtypeStruct((B, MLA_HEADS, MLA_V_ROWS, S), bf),
        jax.ShapeDtypeStruct((B, D_MLA, S), bf),
        jax.ShapeDtypeStruct((B, SWA_KV_HEADS, SWA_HEAD_DIM, SWA_GROUP * S), bf),
        jax.ShapeDtypeStruct((B, SWA_KV_HEADS, S, SWA_HEAD_DIM), bf),
        jax.ShapeDtypeStruct((B, SWA_KV_HEADS, SWA_V_ROWS, S), bf),
        jax.ShapeDtypeStruct((B, D_SWA, S), bf),
    )
    out_specs = (
        pl.BlockSpec((1, MLA_HEADS, QK_PAD, T), lambda b, t: (b, 0, 0, t)),
        pl.BlockSpec((1, MLA_HEADS, T, QK_PAD), lambda b, t: (b, 0, t, 0)),
        pl.BlockSpec((1, MLA_HEADS, MLA_V_ROWS, T), lambda b, t: (b, 0, 0, t)),
        pl.BlockSpec((1, D_MLA, T), lambda b, t: (b, 0, t)),
        pl.BlockSpec((1, SWA_KV_HEADS, SWA_HEAD_DIM, SWA_GROUP * T), lambda b, t: (b, 0, 0, t)),
        pl.BlockSpec((1, SWA_KV_HEADS, T, SWA_HEAD_DIM), lambda b, t: (b, 0, t, 0)),
        pl.BlockSpec((1, SWA_KV_HEADS, SWA_V_ROWS, T), lambda b, t: (b, 0, 0, t)),
        pl.BlockSpec((1, D_SWA, T), lambda b, t: (b, 0, t)),
    )
    in_specs = [
        pl.BlockSpec((1, T, D_MODEL), lambda b, t: (b, t, 0)),
        const(TR_ROWS, D_MODEL),
        const(D_MODEL, NAT_COLS),
        const(Q_LORA, 1),
        const(1, KV_LORA),
        const(MLA_HEADS * MLA_QK, Q_LORA),
        const(KV_LORA, MLA_HEADS * QK_PAD),
        const(D_MLA, KV_LORA),
        pl.BlockSpec((MLA_ROPE // 2, T), lambda b, t: (0, t)),
        pl.BlockSpec((MLA_ROPE // 2, T), lambda b, t: (0, t)),
        pl.BlockSpec((T, LANE), lambda b, t: (t, 0)),
        pl.BlockSpec((T, LANE), lambda b, t: (t, 0)),
    ]
    return pl.pallas_call(
        _proj_kernel,
        grid=grid,
        in_specs=in_specs,
        out_specs=out_specs,
        out_shape=out_shape,
        scratch_shapes=[pltpu.VMEM((TR_ROWS, T), jnp.float32),
                        pltpu.VMEM((T, NAT_COLS), jnp.float32)],
        compiler_params=pltpu.CompilerParams(
            dimension_semantics=("arbitrary", "arbitrary"),
            vmem_limit_bytes=VMEM_LIMIT_BYTES),
        name="proj",
    )(x, w_tr, w_nat, g_q, g_kv, w_uq_t, w_uk, w_uv_t, cos_t, sin_t, cos_n, sin_n)


def _mla_kernel(qt_ref, k_ref, vt_ref, gate_ref, o_ref, m_ref, acc_ref):
    n_chunks = k_ref.shape[2] // MLA_TK
    n_tiles = qt_ref.shape[3] // MLA_TQ

    def q_tile(t):
        return qt_ref[0, 0, :, t * MLA_TQ:(t + 1) * MLA_TQ]

    def chunk_slice(c):
        start = c * MLA_TK
        return pl.ds(start if isinstance(c, int) else pl.multiple_of(start, MLA_TK), MLA_TK)

    def key_chunk(c):
        return k_ref[0, 0, chunk_slice(c), :]

    def value_chunk(c):
        return vt_ref[0, 0, :, chunk_slice(c)]

    items = [(t, c) for t in range(n_tiles) for c in range(n_chunks)]
    m_use = [None] * n_tiles
    seen = [None] * n_tiles
    acc = [None] * n_tiles
    excess = None

    def issue_scores(i):
        t, c = items[i]
        if c == 0:
            s0 = jnp.dot(k_ref[0, 0, 0:BF16_SUBLANES, :], q_tile(t),
                         preferred_element_type=jnp.float32)
            m_use[t] = jnp.max(s0, axis=0, keepdims=True)
        return jnp.dot(key_chunk(c), q_tile(t), preferred_element_type=jnp.float32)

    pending = [issue_scores(i) for i in range(MLA_AHEAD)]
    for i, (t, c) in enumerate(items):
        if i + MLA_AHEAD < len(items):
            pending.append(issue_scores(i + MLA_AHEAD))
        s = pending.pop(0)
        p = jnp.exp2(s - m_use[t]).astype(jnp.bfloat16)
        pv = jnp.dot(value_chunk(c), p, preferred_element_type=jnp.float32)
        acc[t] = pv if c == 0 else acc[t] + pv
        smax = jnp.max(s.reshape(MLA_TK // SUBLANES, SUBLANES, MLA_TQ), axis=0)
        seen[t] = smax if seen[t] is None else jnp.maximum(seen[t], smax)
        last = c + 1 == n_chunks
        if last or c == 0 or (c + 1) % MLA_REFRESH == 0:
            over = seen[t] - m_use[t]
            excess = over if excess is None else jnp.maximum(excess, over)
            if last:
                acc_ref[t] = acc[t]
            else:
                m_next = jnp.maximum(m_use[t], jnp.max(seen[t], axis=0, keepdims=True))
                acc[t] = acc[t] * jnp.exp2(m_use[t] - m_next)
                m_use[t] = m_next
                seen[t] = None

    @pl.when(jnp.max(excess) > MLA_MAX_EXCESS)
    def _():
        for t in range(n_tiles):
            m_ref[...] = jnp.full(m_ref.shape, -jnp.inf, jnp.float32)
            acc_ref[t] = jnp.zeros(acc_ref.shape[1:], jnp.float32)

            def chunk(c, carry, t=t):
                s = jnp.dot(key_chunk(c), q_tile(t), preferred_element_type=jnp.float32)
                m_prev = m_ref[...]
                m_new = jnp.maximum(m_prev, jnp.max(s, axis=0, keepdims=True))
                p = jnp.exp2(s - m_new).astype(jnp.bfloat16)
                acc_ref[t] = jnp.exp2(m_prev - m_new) * acc_ref[t] + jnp.dot(
                    value_chunk(c), p, preferred_element_type=jnp.float32)
                m_ref[...] = m_new
                return carry

            lax.fori_loop(0, n_chunks, chunk, 0)

    for t in range(n_tiles):
        cols = slice(t * MLA_TQ, (t + 1) * MLA_TQ)
        o = acc_ref[t, 0:MLA_V, :] / acc_ref[t, MLA_V:MLA_V + 1, :]
        o_ref[0, :, cols] = (o * gate_ref[0, :, cols].astype(jnp.float32)).astype(o_ref.dtype)


def _mla_attention(q_t, k, v_t, gate_t):
    B, H, _, S = q_t.shape
    TQ = MLA_TQ
    step_q = MLA_STEP_TILES * TQ
    grid = (B, H, S // step_q)
    return pl.pallas_call(
        _mla_kernel,
        grid=grid,
        in_specs=[
            pl.BlockSpec((1, 1, QK_PAD, step_q), lambda b, h, i: (b, h, 0, i)),
            pl.BlockSpec((1, 1, S, QK_PAD), lambda b, h, i: (b, h, 0, 0)),
            pl.BlockSpec((1, 1, MLA_V_ROWS, S), lambda b, h, i: (b, h, 0, 0)),
            pl.BlockSpec((1, MLA_V, step_q), lambda b, h, i: (b, h, i)),
        ],
        out_specs=pl.BlockSpec((1, MLA_V, step_q), lambda b, h, i: (b, h, i)),
        out_shape=jax.ShapeDtypeStruct((B, D_MLA, S), jnp.bfloat16),
        scratch_shapes=[pltpu.VMEM((1, TQ), jnp.float32),
                        pltpu.VMEM((MLA_STEP_TILES, MLA_V_ROWS, TQ), jnp.float32)],
        compiler_params=pltpu.CompilerParams(
            dimension_semantics=("arbitrary", "arbitrary", "arbitrary"),
            vmem_limit_bytes=VMEM_LIMIT_BYTES),
        name="mla",
    )(q_t, k, v_t, gate_t)


def _bias_kernel(rel_ref, bucket_ref, o_ref):
    g = pl.program_id(0)
    for part in range(3):
        bucket = bucket_ref[part]
        for hh in range(SWA_GROUP):
            head = g * SWA_GROUP + hh
            tile = jnp.full((BLOCK, BLOCK), -jnp.inf, jnp.float32)
            for b in range(N_BUCKETS):
                tile = jnp.where(bucket == b, rel_ref[b, head] * LOG2_E, tile)
            o_ref[0, part, :, hh * BLOCK:(hh + 1) * BLOCK] = tile


def _bias_table(rel_bias, bucket):
    return pl.pallas_call(
        _bias_kernel,
        grid=(SWA_KV_HEADS,),
        in_specs=[pl.BlockSpec(memory_space=pltpu.SMEM),
                  pl.BlockSpec((3, BLOCK, BLOCK), lambda g: (0, 0, 0))],
        out_specs=pl.BlockSpec((1, 3, BLOCK, SWA_GROUP * BLOCK), lambda g: (g, 0, 0, 0)),
        out_shape=jax.ShapeDtypeStruct((SWA_KV_HEADS, 3, BLOCK, SWA_GROUP * BLOCK), jnp.float32),
        compiler_params=pltpu.CompilerParams(dimension_semantics=("arbitrary",)),
        name="bias",
    )(rel_bias, bucket)


def _swa_kernel(qst_ref, ks_ref, vst_ref, bias_ref, sink_ref, gate_ref, o_ref):
    n_total = ks_ref.shape[2] // BLOCK
    n_local = o_ref.shape[2] // BLOCK
    step = pl.program_id(2)
    n_steps = pl.num_programs(2)
    sink = sink_ref[0] * LOG2_E
    neg_inf = jnp.float32(-jnp.inf)

    def band_starts(j):
        n = step * n_local + j
        return tuple(pl.multiple_of(b * BLOCK, BLOCK)
                     for b in (jnp.maximum(n - 1, 0), n, jnp.minimum(n + 1, n_total - 1)))

    def band_scores(j):
        q_t = qst_ref[0, 0, :, j * SWA_GROUP * BLOCK:(j + 1) * SWA_GROUP * BLOCK]
        penalties = [None, None, None]
        if j == 0:
            penalties[0] = jnp.where(step > 0, 0.0, neg_inf)
        if j == n_local - 1:
            penalties[2] = jnp.where(step < n_steps - 1, 0.0, neg_inf)
        scores = []
        for part, start in enumerate(band_starts(j)):
            k = ks_ref[0, 0, pl.ds(start, BLOCK), :]
            s = jnp.dot(k, q_t, preferred_element_type=jnp.float32) + bias_ref[0, part]
            if penalties[part] is not None:
                s = s + penalties[part]
            scores.append(s)
        return scores

    scores_next = band_scores(0)
    for j in range(n_local):
        scores = scores_next
        if j + 1 < n_local:
            scores_next = band_scores(j + 1)
        m = sink
        for s in scores:
            m = jnp.maximum(m, jnp.max(s, axis=0, keepdims=True))
        acc = None
        for part, start in enumerate(band_starts(j)):
            e = jnp.exp2(scores[part] - m).astype(jnp.bfloat16)
            v_t = vst_ref[0, 0, :, pl.ds(start, BLOCK)]
            pv = jnp.dot(v_t, e, preferred_element_type=jnp.float32)
            acc = pv if acc is None else acc + pv
        denom = acc[SWA_HEAD_DIM:SWA_HEAD_DIM + 1] + jnp.exp2(sink - m)
        o = acc[0:SWA_HEAD_DIM] / denom
        for hh in range(SWA_GROUP):
            gate = gate_ref[0, hh * SWA_HEAD_DIM:(hh + 1) * SWA_HEAD_DIM,
                            j * BLOCK:(j + 1) * BLOCK].astype(jnp.float32)
            o_ref[0, hh * SWA_HEAD_DIM:(hh + 1) * SWA_HEAD_DIM, j * BLOCK:(j + 1) * BLOCK] = (
                o[:, hh * BLOCK:(hh + 1) * BLOCK] * gate).astype(o_ref.dtype)


def _swa_attention(qs_t, ks, vs_t, bias_t, sink_rows, gate_t):
    B, G, _, S = vs_t.shape
    T = SWA_TOKENS
    grid = (B, G, S // T)
    rows = SWA_GROUP * SWA_HEAD_DIM
    return pl.pallas_call(
        _swa_kernel,
        grid=grid,
        in_specs=[
            pl.BlockSpec((1, 1, SWA_HEAD_DIM, SWA_GROUP * T), lambda b, g, t: (b, g, 0, t)),
            pl.BlockSpec((1, 1, S, SWA_HEAD_DIM), lambda b, g, t: (b, g, 0, 0)),
            pl.BlockSpec((1, 1, SWA_V_ROWS, S), lambda b, g, t: (b, g, 0, 0)),
            pl.BlockSpec((1, 3, BLOCK, SWA_GROUP * BLOCK), lambda b, g, t: (g, 0, 0, 0)),
            pl.BlockSpec((1, 1, SWA_GROUP * BLOCK), lambda b, g, t: (g, 0, 0)),
            pl.BlockSpec((1, rows, T), lambda b, g, t: (b, g, t)),
        ],
        out_specs=pl.BlockSpec((1, rows, T), lambda b, g, t: (b, g, t)),
        out_shape=jax.ShapeDtypeStruct((B, D_SWA, S), jnp.bfloat16),
        compiler_params=pltpu.CompilerParams(
            dimension_semantics=("arbitrary", "arbitrary", "arbitrary"),
            vmem_limit_bytes=VMEM_LIMIT_BYTES),
        name="swa",
    )(qs_t, ks, vs_t, bias_t, sink_rows, gate_t)


def _out_kernel(x_ref, ma_ref, mb_ref, wa_ref, wb_ref, g_ref, b_ref, y_ref):
    out = lax.dot_general(ma_ref[0], wa_ref[...], _TN, preferred_element_type=jnp.float32)
    out = out + lax.dot_general(mb_ref[0], wb_ref[...], _TN, preferred_element_type=jnp.float32)
    h = ALPHA * x_ref[0] + out
    mu = jnp.mean(h, axis=-1, keepdims=True)
    var = jnp.mean(jnp.square(h - mu), axis=-1, keepdims=True)
    y_ref[0] = ((h - mu) * lax.rsqrt(var + 1e-5)) * g_ref[...] + b_ref[...]


def _out_project(x, mixed_a_t, mixed_b_t, w_a, w_b, ln_g, ln_b):
    B, S, D = x.shape
    T = OUT_TOKENS
    grid = (B, S // T)
    const = lambda *shape: pl.BlockSpec(shape, lambda b, t: (0,) * len(shape))
    return pl.pallas_call(
        _out_kernel,
        grid=grid,
        in_specs=[
            pl.BlockSpec((1, T, D), lambda b, t: (b, t, 0)),
            pl.BlockSpec((1, D_MLA, T), lambda b, t: (b, 0, t)),
            pl.BlockSpec((1, D_SWA, T), lambda b, t: (b, 0, t)),
            const(D_MLA, D),
            const(D_SWA, D),
            const(1, D),
            const(1, D),
        ],
        out_specs=pl.BlockSpec((1, T, D), lambda b, t: (b, t, 0)),
        out_shape=jax.ShapeDtypeStruct((B, S, D), x.dtype),
        compiler_params=pltpu.CompilerParams(
            dimension_semantics=("arbitrary", "arbitrary"),
            vmem_limit_bytes=VMEM_LIMIT_BYTES),
        name="outproj",
    )(x, mixed_a_t, mixed_b_t, w_a, w_b, ln_g, ln_b)


def _t5_bucket(rel):
    half = N_BUCKETS // 2
    ret = np.where(rel > 0, half, 0)
    n = np.abs(rel)
    max_exact = half // 2
    large = max_exact + (np.log(np.maximum(n, 1).astype(np.float32) / max_exact)
                         / np.log(MAX_DISTANCE / max_exact) * (half - max_exact)).astype(np.int32)
    large = np.minimum(large, half - 1)
    return (ret + np.where(n < max_exact, n, large)).astype(np.int32)


def _band_geometry():
    q_loc = np.arange(BLOCK)
    k_loc = np.arange(3 * BLOCK) - BLOCK
    rel = k_loc[:, None] - q_loc[None, :]
    return _t5_bucket(rel), np.abs(rel) <= WINDOW


def _col(w_in, i):
    return w_in[:, IN_OFFSETS[i]:IN_OFFSETS[i + 1]]


def kernel(x, w_in, g_q, g_kv, w_uq, w_ukv, sink, rel_bias, w_out, ln_g, ln_b):
    B, S, _ = x.shape
    bf = jnp.bfloat16
    f32 = jnp.float32

    pos = jnp.arange(S, dtype=f32)
    inv_freq = ROPE_BASE ** (-jnp.arange(0, MLA_ROPE, 2, dtype=f32) / MLA_ROPE)
    ang = pos[:, None] * inv_freq[None, :]
    cos, sin = jnp.cos(ang), jnp.sin(ang)
    half = MLA_ROPE // 2
    lane_pad = lambda a, b_: jnp.concatenate(
        [jnp.zeros((S, MLA_NOPE), f32), a, b_, jnp.zeros((S, LANE - MLA_QK), f32)], axis=1)
    cos_n = lane_pad(cos, cos)
    sin_n = lane_pad(sin, sin)

    c_q, c_kv, k_rope, gate_a, q_s, k_s, v_s, gate_b = (_col(w_in, i) for i in range(8))
    w_tr = jnp.concatenate([c_q, gate_a, q_s, v_s, gate_b], axis=1).T.astype(bf)
    zeros = lambda n: jnp.zeros((D_MODEL, n), f32)
    kr_pad = jnp.concatenate([zeros(MLA_NOPE), k_rope, zeros(LANE - MLA_QK)], axis=1)
    kr_rot = jnp.concatenate([zeros(MLA_NOPE), -k_rope[:, half:], k_rope[:, :half],
                              zeros(LANE - MLA_QK)], axis=1)
    w_nat = jnp.concatenate([c_kv, k_s, kr_pad, kr_rot], axis=1).astype(bf)
    w_ukv3 = w_ukv.reshape(KV_LORA, MLA_HEADS, MLA_NOPE + MLA_V)
    w_uk = jnp.pad(w_ukv3[:, :, :MLA_NOPE], ((0, 0), (0, 0), (0, QK_PAD - MLA_NOPE)))
    w_uk = w_uk.reshape(KV_LORA, MLA_HEADS * QK_PAD).astype(bf)
    w_uv_t = w_ukv3[:, :, MLA_NOPE:].reshape(KV_LORA, D_MLA).T.astype(bf)
    w_uq_t = w_uq.T.astype(bf)

    q_t, k, v_t, gate_a_t, qs_t, ks, vs_t, gate_b_t = _project(
        x, w_tr, w_nat, g_q.reshape(Q_LORA, 1), g_kv.reshape(1, KV_LORA),
        w_uq_t, w_uk, w_uv_t, cos.T, sin.T, cos_n, sin_n)

    mixed_a_t = _mla_attention(q_t, k, v_t, gate_a_t)

    bucket, band = _band_geometry()
    bucket = jnp.asarray(np.where(band, bucket, -1).reshape(3, BLOCK, BLOCK), jnp.int32)
    bias_t = _bias_table(rel_bias.astype(f32), bucket)
    sink_rows = jnp.repeat(sink.astype(f32).reshape(SWA_KV_HEADS, 1, SWA_GROUP), BLOCK, axis=2)

    mixed_b_t = _swa_attention(qs_t, ks, vs_t, bias_t, sink_rows, gate_b_t)

    w_out_bf = w_out.astype(bf)
    return _out_project(x, mixed_a_t, mixed_b_t, w_out_bf[:D_MLA], w_out_bf[D_MLA:],
                        ln_g.reshape(1, D_MODEL), ln_b.reshape(1, D_MODEL))
```

```python
import functools
import math

import jax
import jax.numpy as jnp
import numpy as np
from jax import lax
from jax.experimental import pallas as pl
from jax.experimental.pallas import tpu as pltpu

D_MODEL = 1024
MLA_HEADS = 8
MLA_NOPE = 64
MLA_ROPE = 32
MLA_V = 64
MLA_QK = MLA_NOPE + MLA_ROPE
Q_LORA = 256
KV_LORA = 128
D_MLA = MLA_HEADS * MLA_V
MLA_SCALE = 1.0 / math.sqrt(MLA_QK)
LOG2_E = math.log2(math.e)
MLA_Q_SCALE = MLA_SCALE * LOG2_E
BF16_SUBLANES = 16
MLA_V_ROWS = MLA_V + BF16_SUBLANES
ROPE_BASE = 10000.0

SWA_HEADS = 8
SWA_KV_HEADS = 2
SWA_HEAD_DIM = 64
SWA_GROUP = SWA_HEADS // SWA_KV_HEADS
D_SWA = SWA_HEADS * SWA_HEAD_DIM
WINDOW = 128
BLOCK = 128
SWA_SCALE = 1.0 / math.sqrt(SWA_HEAD_DIM)
SWA_Q_SCALE = SWA_SCALE * LOG2_E
SWA_V_ROWS = SWA_HEAD_DIM + BF16_SUBLANES
N_BUCKETS = 32
MAX_DISTANCE = 128

DEPTH = 1
ALPHA = (2.0 * DEPTH) ** 0.25

IN_SPLITS = (Q_LORA, KV_LORA, MLA_ROPE, D_MLA, D_SWA,
             SWA_KV_HEADS * SWA_HEAD_DIM, SWA_KV_HEADS * SWA_HEAD_DIM, D_SWA)
IN_OFFSETS = tuple(int(o) for o in np.cumsum((0,) + IN_SPLITS))

LANE = 128
SUBLANES = 8
QK_PAD = LANE
VMEM_LIMIT_BYTES = 56 * 1024 * 1024

PROJ_TOKENS = 1024
MLA_TQ = 512
MLA_STEP_TILES = 4
MLA_TK = 256
MLA_AHEAD = 2
MLA_REFRESH = 8
MLA_MAX_EXCESS = 64.0
SWA_TOKENS = 2048
SWA_AHEAD = 1
SWA_MAX_EXCESS = 64.0
OUT_TOKENS = 1024
OUT_ROWS = 256

TR_CQ = 0
TR_GA = TR_CQ + Q_LORA
TR_QS = TR_GA + D_MLA
TR_VS = TR_QS + D_SWA
TR_GB = TR_VS + SWA_KV_HEADS * SWA_HEAD_DIM
TR_ROWS = TR_GB + D_SWA
NAT_CKV = 0
NAT_KS = NAT_CKV + KV_LORA
NAT_KR = NAT_KS + SWA_KV_HEADS * SWA_HEAD_DIM
NAT_KR_ROT = NAT_KR + LANE
NAT_COLS = NAT_KR_ROT + LANE

_NT = (((1,), (1,)), ((), ()))
_TN = (((0,), (0,)), ((), ()))


def _rsqrt_mean_sq(x, axis, eps):
    return lax.rsqrt(jnp.mean(x * x, axis=axis, keepdims=True) + eps)


def _proj_kernel(x_ref, wtr_ref, wnat_ref, gq_ref, gkv_ref, wuqt_ref, wuk_ref, wuvt_ref,
                 cos_t_ref, sin_t_ref, cos_n_ref, sin_n_ref,
                 qt_ref, k_ref, vt_ref, gat_ref, qst_ref, ks_ref, vst_ref, gbt_ref,
                 tr_ref, nat_ref):
    xb = x_ref[0].astype(jnp.bfloat16)
    tr_ref[...] = lax.dot_general(wtr_ref[...], xb, _NT, preferred_element_type=jnp.float32)
    nat_ref[...] = jnp.dot(xb, wnat_ref[...], preferred_element_type=jnp.float32)

    cq = tr_ref[TR_CQ:TR_CQ + Q_LORA, :]
    cqn = (cq * _rsqrt_mean_sq(cq, 0, 1e-6)) * gq_ref[...]
    q_t = jnp.dot(wuqt_ref[...], cqn.astype(jnp.bfloat16),
                  preferred_element_type=jnp.float32)
    cos_t = cos_t_ref[...]
    sin_t = sin_t_ref[...]
    half = MLA_ROPE // 2
    zeros_pad = jnp.zeros((QK_PAD - MLA_QK, q_t.shape[1]), jnp.bfloat16)
    for h in range(MLA_HEADS):
        base = h * MLA_QK
        nope = q_t[base:base + MLA_NOPE]
        r1 = q_t[base + MLA_NOPE:base + MLA_NOPE + half]
        r2 = q_t[base + MLA_NOPE + half:base + MLA_QK]
        qt_ref[0, h, 0:MLA_NOPE, :] = (nope * MLA_Q_SCALE).astype(jnp.bfloat16)
        qt_ref[0, h, MLA_NOPE:MLA_NOPE + half, :] = (
            (r1 * cos_t - r2 * sin_t) * MLA_Q_SCALE).astype(jnp.bfloat16)
        qt_ref[0, h, MLA_NOPE + half:MLA_QK, :] = (
            (r2 * cos_t + r1 * sin_t) * MLA_Q_SCALE).astype(jnp.bfloat16)
        qt_ref[0, h, MLA_QK:QK_PAD, :] = zeros_pad

    ckv = nat_ref[:, NAT_CKV:NAT_CKV + KV_LORA]
    kvn = ((ckv * _rsqrt_mean_sq(ckv, 1, 1e-6)) * gkv_ref[...]).astype(jnp.bfloat16)
    k_nope = jnp.dot(kvn, wuk_ref[...], preferred_element_type=jnp.float32)
    v_t = lax.dot_general(wuvt_ref[...], kvn, _NT, preferred_element_type=jnp.float32)
    k_rope = (nat_ref[:, NAT_KR:NAT_KR + LANE] * cos_n_ref[...]
              + nat_ref[:, NAT_KR_ROT:NAT_KR_ROT + LANE] * sin_n_ref[...])
    row_id = lax.broadcasted_iota(jnp.int32, (BF16_SUBLANES, v_t.shape[1]), 0)
    ones_row = jnp.where(row_id == 0, 1.0, 0.0).astype(jnp.bfloat16)
    for h in range(MLA_HEADS):
        k_ref[0, h] = (k_nope[:, h * QK_PAD:(h + 1) * QK_PAD] + k_rope).astype(jnp.bfloat16)
        vt_ref[0, h, 0:MLA_V, :] = v_t[h * MLA_V:(h + 1) * MLA_V].astype(jnp.bfloat16)
        vt_ref[0, h, MLA_V:MLA_V_ROWS, :] = ones_row

    gat_ref[0] = jax.nn.silu(tr_ref[TR_GA:TR_GA + D_MLA, :]).astype(jnp.bfloat16)
    gbt_ref[0] = jax.nn.silu(tr_ref[TR_GB:TR_GB + D_SWA, :]).astype(jnp.bfloat16)

    n_blocks = x_ref.shape[1] // BLOCK
    for g in range(SWA_KV_HEADS):
        ks_ref[0, g] = nat_ref[:, NAT_KS + g * SWA_HEAD_DIM:
                               NAT_KS + (g + 1) * SWA_HEAD_DIM].astype(jnp.bfloat16)
        vst_ref[0, g, 0:SWA_HEAD_DIM, :] = tr_ref[TR_VS + g * SWA_HEAD_DIM:
                                                  TR_VS + (g + 1) * SWA_HEAD_DIM, :].astype(jnp.bfloat16)
        vst_ref[0, g, SWA_HEAD_DIM:SWA_V_ROWS, :] = ones_row
        for hh in range(SWA_GROUP):
            row = TR_QS + (g * SWA_GROUP + hh) * SWA_HEAD_DIM
            q_h = (tr_ref[row:row + SWA_HEAD_DIM, :] * SWA_Q_SCALE).astype(jnp.bfloat16)
            for nb in range(n_blocks):
                col = (nb * SWA_GROUP + hh) * BLOCK
                qst_ref[0, g, :, col:col + BLOCK] = q_h[:, nb * BLOCK:(nb + 1) * BLOCK]


def _project(x, w_tr, w_nat, g_q, g_kv, w_uq_t, w_uk, w_uv_t, cos_t, sin_t, cos_n, sin_n):
    B, S, _ = x.shape
    T = PROJ_TOKENS
    grid = (B, S // T)
    const = lambda *shape: pl.BlockSpec(shape, lambda b, t: (0,) * len(shape))
    bf = jnp.bfloat16
    out_shape = (
        jax.ShapeDtypeStruct((B, MLA_HEADS, QK_PAD, S), bf),
        jax.ShapeDtypeStruct((B, MLA_HEADS, S, QK_PAD), bf),
        jax.ShapeDtypeStruct((B, MLA_HEADS, MLA_V_ROWS, S), bf),
        jax.ShapeDtypeStruct((B, D_MLA, S), bf),
        jax.ShapeDtypeStruct((B, SWA_KV_HEADS, SWA_HEAD_DIM, SWA_GROUP * S), bf),
        jax.ShapeDtypeStruct((B, SWA_KV_HEADS, S, SWA_HEAD_DIM), bf),
        jax.ShapeDtypeStruct((B, SWA_KV_HEADS, SWA_V_ROWS, S), bf),
        jax.ShapeDtypeStruct((B, D_SWA, S), bf),
    )
    out_specs = (
        pl.BlockSpec((1, MLA_HEADS, QK_PAD, T), lambda b, t: (b, 0, 0, t)),
        pl.BlockSpec((1, MLA_HEADS, T, QK_PAD), lambda b, t: (b, 0, t, 0)),
        pl.BlockSpec((1, MLA_HEADS, MLA_V_ROWS, T), lambda b, t: (b, 0, 0, t)),
        pl.BlockSpec((1, D_MLA, T), lambda b, t: (b, 0, t)),
        pl.BlockSpec((1, SWA_KV_HEADS, SWA_HEAD_DIM, SWA_GROUP * T), lambda b, t: (b, 0, 0, t)),
        pl.BlockSpec((1, SWA_KV_HEADS, T, SWA_HEAD_DIM), lambda b, t: (b, 0, t, 0)),
        pl.BlockSpec((1, SWA_KV_HEADS, SWA_V_ROWS, T), lambda b, t: (b, 0, 0, t)),
        pl.BlockSpec((1, D_SWA, T), lambda b, t: (b, 0, t)),
    )
    in_specs = [
        pl.BlockSpec((1, T, D_MODEL), lambda b, t: (b, t, 0)),
        const(TR_ROWS, D_MODEL),
        const(D_MODEL, NAT_COLS),
        const(Q_LORA, 1),
        const(1, KV_LORA),
        const(MLA_HEADS * MLA_QK, Q_LORA),
        const(KV_LORA, MLA_HEADS * QK_PAD),
        const(D_MLA, KV_LORA),
        pl.BlockSpec((MLA_ROPE // 2, T), lambda b, t: (0, t)),
        pl.BlockSpec((MLA_ROPE // 2, T), lambda b, t: (0, t)),
        pl.BlockSpec((T, LANE), lambda b, t: (t, 0)),
        pl.BlockSpec((T, LANE), lambda b, t: (t, 0)),
    ]
    return pl.pallas_call(
        _proj_kernel,
        grid=grid,
        in_specs=in_specs,
        out_specs=out_specs,
        out_shape=out_shape,
        scratch_shapes=[pltpu.VMEM((TR_ROWS, T), jnp.float32),
                        pltpu.VMEM((T, NAT_COLS), jnp.float32)],
        compiler_params=pltpu.CompilerParams(
            dimension_semantics=("arbitrary", "arbitrary"),
            vmem_limit_bytes=VMEM_LIMIT_BYTES),
        name="proj",
    )(x, w_tr, w_nat, g_q, g_kv, w_uq_t, w_uk, w_uv_t, cos_t, sin_t, cos_n, sin_n)


def _mla_kernel(qt_ref, k_ref, vt_ref, gate_ref, o_ref, m_ref, acc_ref):
    n_chunks = k_ref.shape[2] // MLA_TK
    n_tiles = qt_ref.shape[3] // MLA_TQ

    def q_tile(t):
        return qt_ref[0, 0, :, t * MLA_TQ:(t + 1) * MLA_TQ]

    def chunk_slice(c):
        start = c * MLA_TK
        return pl.ds(start if isinstance(c, int) else pl.multiple_of(start, MLA_TK), MLA_TK)

    def key_chunk(c):
        return k_ref[0, 0, chunk_slice(c), :]

    def value_chunk(c):
        return vt_ref[0, 0, :, chunk_slice(c)]

    items = [(t, c) for t in range(n_tiles) for c in range(n_chunks)]
    m_use = [None] * n_tiles
    seen = [None] * n_tiles
    acc = [None] * n_tiles
    excess = None

    def issue_scores(i):
        t, c = items[i]
        if c == 0:
            s0 = jnp.dot(k_ref[0, 0, 0:BF16_SUBLANES, :], q_tile(t),
                         preferred_element_type=jnp.float32)
            m_use[t] = jnp.max(s0, axis=0, keepdims=True)
        return jnp.dot(key_chunk(c), q_tile(t), preferred_element_type=jnp.float32)

    pending = [issue_scores(i) for i in range(MLA_AHEAD)]
    for i, (t, c) in enumerate(items):
        if i + MLA_AHEAD < len(items):
            pending.append(issue_scores(i + MLA_AHEAD))
        s = pending.pop(0)
        p = jnp.exp2(s - m_use[t]).astype(jnp.bfloat16)
        pv = jnp.dot(value_chunk(c), p, preferred_element_type=jnp.float32)
        acc[t] = pv if c == 0 else acc[t] + pv
        smax = jnp.max(s.reshape(MLA_TK // SUBLANES, SUBLANES, MLA_TQ), axis=0)
        seen[t] = smax if seen[t] is None else jnp.maximum(seen[t], smax)
        last = c + 1 == n_chunks
        if last or c == 0 or (c + 1) % MLA_REFRESH == 0:
            over = seen[t] - m_use[t]
            excess = over if excess is None else jnp.maximum(excess, over)
            if last:
                acc_ref[t] = acc[t]
            else:
                m_next = jnp.maximum(m_use[t], jnp.max(seen[t], axis=0, keepdims=True))
                acc[t] = acc[t] * jnp.exp2(m_use[t] - m_next)
                m_use[t] = m_next
                seen[t] = None

    @pl.when(jnp.max(excess) > MLA_MAX_EXCESS)
    def _():
        for t in range(n_tiles):
            m_ref[...] = jnp.full(m_ref.shape, -jnp.inf, jnp.float32)
            acc_ref[t] = jnp.zeros(acc_ref.shape[1:], jnp.float32)

            def chunk(c, carry, t=t):
                s = jnp.dot(key_chunk(c), q_tile(t), preferred_element_type=jnp.float32)
                m_prev = m_ref[...]
                m_new = jnp.maximum(m_prev, jnp.max(s, axis=0, keepdims=True))
                p = jnp.exp2(s - m_new).astype(jnp.bfloat16)
                acc_ref[t] = jnp.exp2(m_prev - m_new) * acc_ref[t] + jnp.dot(
                    value_chunk(c), p, preferred_element_type=jnp.float32)
                m_ref[...] = m_new
                return carry

            lax.fori_loop(0, n_chunks, chunk, 0)

    for t in range(n_tiles):
        cols = slice(t * MLA_TQ, (t + 1) * MLA_TQ)
        o = acc_ref[t, 0:MLA_V, :] / acc_ref[t, MLA_V:MLA_V + 1, :]
        o_ref[0, :, cols] = (o * gate_ref[0, :, cols].astype(jnp.float32)).astype(o_ref.dtype)


def _mla_attention(q_t, k, v_t, gate_t):
    B, H, _, S = q_t.shape
    TQ = MLA_TQ
    step_q = MLA_STEP_TILES * TQ
    grid = (B, H, S // step_q)
    return pl.pallas_call(
        _mla_kernel,
        grid=grid,
        in_specs=[
            pl.BlockSpec((1, 1, QK_PAD, step_q), lambda b, h, i: (b, h, 0, i)),
            pl.BlockSpec((1, 1, S, QK_PAD), lambda b, h, i: (b, h, 0, 0)),
            pl.BlockSpec((1, 1, MLA_V_ROWS, S), lambda b, h, i: (b, h, 0, 0)),
            pl.BlockSpec((1, MLA_V, step_q), lambda b, h, i: (b, h, i)),
        ],
        out_specs=pl.BlockSpec((1, MLA_V, step_q), lambda b, h, i: (b, h, i)),
        out_shape=jax.ShapeDtypeStruct((B, D_MLA, S), jnp.bfloat16),
        scratch_shapes=[pltpu.VMEM((1, TQ), jnp.float32),
                        pltpu.VMEM((MLA_STEP_TILES, MLA_V_ROWS, TQ), jnp.float32)],
        compiler_params=pltpu.CompilerParams(
            dimension_semantics=("arbitrary", "arbitrary", "arbitrary"),
            vmem_limit_bytes=VMEM_LIMIT_BYTES),
        name="mla",
    )(q_t, k, v_t, gate_t)


def _bias_kernel(rel_ref, sink_ref, bucket_ref, o_ref):
    g = pl.program_id(0)
    for part in range(3):
        bucket = bucket_ref[part]
        for hh in range(SWA_GROUP):
            head = g * SWA_GROUP + hh
            tile = jnp.full((BLOCK, BLOCK), -jnp.inf, jnp.float32)
            for b in range(N_BUCKETS):
                tile = jnp.where(bucket == b, (rel_ref[b, head] - sink_ref[head]) * LOG2_E, tile)
            o_ref[0, part, :, hh * BLOCK:(hh + 1) * BLOCK] = tile


def _bias_table(rel_bias, sink, bucket):
    return pl.pallas_call(
        _bias_kernel,
        grid=(SWA_KV_HEADS,),
        in_specs=[pl.BlockSpec(memory_space=pltpu.SMEM),
                  pl.BlockSpec(memory_space=pltpu.SMEM),
                  pl.BlockSpec((3, BLOCK, BLOCK), lambda g: (0, 0, 0))],
        out_specs=pl.BlockSpec((1, 3, BLOCK, SWA_GROUP * BLOCK), lambda g: (g, 0, 0, 0)),
        out_shape=jax.ShapeDtypeStruct((SWA_KV_HEADS, 3, BLOCK, SWA_GROUP * BLOCK), jnp.float32),
        compiler_params=pltpu.CompilerParams(dimension_semantics=("arbitrary",)),
        name="bias",
    )(rel_bias, sink, bucket)


def _swa_kernel(qst_ref, ks_ref, vst_ref, bias_ref, gate_ref, o_ref):
    n_total = ks_ref.shape[2] // BLOCK
    n_local = o_ref.shape[2] // BLOCK
    step = pl.program_id(2)
    n_steps = pl.num_programs(2)
    neg_inf = jnp.float32(-jnp.inf)

    def band_starts(j):
        n = step * n_local + j
        return tuple(pl.multiple_of(b * BLOCK, BLOCK)
                     for b in (jnp.maximum(n - 1, 0), n, jnp.minimum(n + 1, n_total - 1)))

    def lane_block(j, width):
        start = j * width
        return pl.ds(start if isinstance(j, int) else pl.multiple_of(start, width), width)

    def band_scores(j):
        q_t = qst_ref[0, 0, :, lane_block(j, SWA_GROUP * BLOCK)]
        n = step * n_local + j
        penalties = [None, None, None]
        if not isinstance(j, int) or j == 0:
            penalties[0] = jnp.where(n > 0, 0.0, neg_inf)
        if not isinstance(j, int) or j == n_local - 1:
            penalties[2] = jnp.where(n < n_total - 1, 0.0, neg_inf)
        scores = []
        for part, start in enumerate(band_starts(j)):
            k = ks_ref[0, 0, pl.ds(start, BLOCK), :]
            s = jnp.dot(k, q_t, preferred_element_type=jnp.float32) + bias_ref[0, part]
            if penalties[part] is not None:
                s = s + penalties[part]
            scores.append(s)
        return scores

    def weighted_values(j, weights):
        acc = None
        for part, start in enumerate(band_starts(j)):
            v_t = vst_ref[0, 0, :, pl.ds(start, BLOCK)]
            pv = jnp.dot(v_t, weights[part].astype(jnp.bfloat16),
                         preferred_element_type=jnp.float32)
            acc = pv if acc is None else acc + pv
        return acc

    def write_block(j, acc, sink_weight):
        o = acc[0:SWA_HEAD_DIM] / (acc[SWA_HEAD_DIM:SWA_HEAD_DIM + 1] + sink_weight)
        cols = lane_block(j, BLOCK)
        for hh in range(SWA_GROUP):
            rows = slice(hh * SWA_HEAD_DIM, (hh + 1) * SWA_HEAD_DIM)
            gate = gate_ref[0, rows, cols].astype(jnp.float32)
            o_ref[0, rows, cols] = (o[:, hh * BLOCK:(hh + 1) * BLOCK] * gate).astype(o_ref.dtype)

    excess = None
    pending = [band_scores(j) for j in range(SWA_AHEAD)]
    for j in range(n_local):
        if j + SWA_AHEAD < n_local:
            pending.append(band_scores(j + SWA_AHEAD))
        scores = pending.pop(0)
        for s in scores:
            smax = jnp.max(s.reshape(BLOCK // SUBLANES, SUBLANES, s.shape[1]), axis=0)
            excess = smax if excess is None else jnp.maximum(excess, smax)
        write_block(j, weighted_values(j, [jnp.exp2(s) for s in scores]), 1.0)

    @pl.when(jnp.max(excess) > SWA_MAX_EXCESS)
    def _():
        def exact_block(j, carry):
            scores = band_scores(j)
            m = jnp.zeros((1, scores[0].shape[1]), jnp.float32)
            for s in scores:
                m = jnp.maximum(m, jnp.max(s, axis=0, keepdims=True))
            write_block(j, weighted_values(j, [jnp.exp2(s - m) for s in scores]), jnp.exp2(-m))
            return carry

        lax.fori_loop(0, n_local, exact_block, 0)


def _swa_attention(qs_t, ks, vs_t, bias_t, gate_t):
    B, G, _, S = vs_t.shape
    T = SWA_TOKENS
    grid = (B, G, S // T)
    rows = SWA_GROUP * SWA_HEAD_DIM
    return pl.pallas_call(
        _swa_kernel,
        grid=grid,
        in_specs=[
            pl.BlockSpec((1, 1, SWA_HEAD_DIM, SWA_GROUP * T), lambda b, g, t: (b, g, 0, t)),
            pl.BlockSpec((1, 1, S, SWA_HEAD_DIM), lambda b, g, t: (b, g, 0, 0)),
            pl.BlockSpec((1, 1, SWA_V_ROWS, S), lambda b, g, t: (b, g, 0, 0)),
            pl.BlockSpec((1, 3, BLOCK, SWA_GROUP * BLOCK), lambda b, g, t: (g, 0, 0, 0)),
            pl.BlockSpec((1, rows, T), lambda b, g, t: (b, g, t)),
        ],
        out_specs=pl.BlockSpec((1, rows, T), lambda b, g, t: (b, g, t)),
        out_shape=jax.ShapeDtypeStruct((B, D_SWA, S), jnp.bfloat16),
        compiler_params=pltpu.CompilerParams(
            dimension_semantics=("arbitrary", "arbitrary", "arbitrary"),
            vmem_limit_bytes=VMEM_LIMIT_BYTES),
        name="swa",
    )(qs_t, ks, vs_t, bias_t, gate_t)


def _out_kernel(x_ref, ma_ref, mb_ref, wa_ref, wb_ref, g_ref, b_ref, y_ref):
    n_chunks = x_ref.shape[1] // OUT_ROWS

    def project(c):
        cols = slice(c * OUT_ROWS, (c + 1) * OUT_ROWS)
        out = lax.dot_general(ma_ref[0, :, cols], wa_ref[...], _TN,
                              preferred_element_type=jnp.float32)
        return out + lax.dot_general(mb_ref[0, :, cols], wb_ref[...], _TN,
                                     preferred_element_type=jnp.float32)

    out_next = project(0)
    for c in range(n_chunks):
        out = out_next
        if c + 1 < n_chunks:
            out_next = project(c + 1)
        rows = slice(c * OUT_ROWS, (c + 1) * OUT_ROWS)
        h = ALPHA * x_ref[0, rows, :] + out
        mu = jnp.mean(h, axis=-1, keepdims=True)
        var = jnp.mean(jnp.square(h - mu), axis=-1, keepdims=True)
        y_ref[0, rows, :] = ((h - mu) * lax.rsqrt(var + 1e-5)) * g_ref[...] + b_ref[...]


def _out_project(x, mixed_a_t, mixed_b_t, w_a, w_b, ln_g, ln_b):
    B, S, D = x.shape
    T = OUT_TOKENS
    grid = (B, S // T)
    const = lambda *shape: pl.BlockSpec(shape, lambda b, t: (0,) * len(shape))
    return pl.pallas_call(
        _out_kernel,
        grid=grid,
        in_specs=[
            pl.BlockSpec((1, T, D), lambda b, t: (b, t, 0)),
            pl.BlockSpec((1, D_MLA, T), lambda b, t: (b, 0, t)),
            pl.BlockSpec((1, D_SWA, T), lambda b, t: (b, 0, t)),
            const(D_MLA, D),
            const(D_SWA, D),
            const(1, D),
            const(1, D),
        ],
        out_specs=pl.BlockSpec((1, T, D), lambda b, t: (b, t, 0)),
        out_shape=jax.ShapeDtypeStruct((B, S, D), x.dtype),
        compiler_params=pltpu.CompilerParams(
            dimension_semantics=("arbitrary", "arbitrary"),
            vmem_limit_bytes=VMEM_LIMIT_BYTES),
        name="outproj",
    )(x, mixed_a_t, mixed_b_t, w_a, w_b, ln_g, ln_b)


def _t5_bucket(rel):
    half = N_BUCKETS // 2
    ret = np.where(rel > 0, half, 0)
    n = np.abs(rel)
    max_exact = half // 2
    large = max_exact + (np.log(np.maximum(n, 1).astype(np.float32) / max_exact)
                         / np.log(MAX_DISTANCE / max_exact) * (half - max_exact)).astype(np.int32)
    large = np.minimum(large, half - 1)
    return (ret + np.where(n < max_exact, n, large)).astype(np.int32)


def _band_geometry():
    q_loc = np.arange(BLOCK)
    k_loc = np.arange(3 * BLOCK) - BLOCK
    rel = k_loc[:, None] - q_loc[None, :]
    return _t5_bucket(rel), np.abs(rel) <= WINDOW


def _col(w_in, i):
    return w_in[:, IN_OFFSETS[i]:IN_OFFSETS[i + 1]]


def kernel(x, w_in, g_q, g_kv, w_uq, w_ukv, sink, rel_bias, w_out, ln_g, ln_b):
    B, S, _ = x.shape
    bf = jnp.bfloat16
    f32 = jnp.float32

    pos = jnp.arange(S, dtype=f32)
    inv_freq = ROPE_BASE ** (-jnp.arange(0, MLA_ROPE, 2, dtype=f32) / MLA_ROPE)
    ang_t = inv_freq[:, None] * pos[None, :]
    cos_t, sin_t = jnp.cos(ang_t), jnp.sin(ang_t)
    cos, sin = cos_t.T, sin_t.T
    half = MLA_ROPE // 2
    lane_pad = lambda a, b_: jnp.concatenate(
        [jnp.zeros((S, MLA_NOPE), f32), a, b_, jnp.zeros((S, LANE - MLA_QK), f32)], axis=1)
    cos_n = lane_pad(cos, cos)
    sin_n = lane_pad(sin, sin)

    c_q, c_kv, k_rope, gate_a, q_s, k_s, v_s, gate_b = (_col(w_in, i) for i in range(8))
    w_tr = jnp.concatenate([c_q, gate_a, q_s, v_s, gate_b], axis=1).T.astype(bf)
    zeros = lambda n: jnp.zeros((D_MODEL, n), f32)
    kr_pad = jnp.concatenate([zeros(MLA_NOPE), k_rope, zeros(LANE - MLA_QK)], axis=1)
    kr_rot = jnp.concatenate([zeros(MLA_NOPE), -k_rope[:, half:], k_rope[:, :half],
                              zeros(LANE - MLA_QK)], axis=1)
    w_nat = jnp.concatenate([c_kv, k_s, kr_pad, kr_rot], axis=1).astype(bf)
    w_ukv3 = w_ukv.reshape(KV_LORA, MLA_HEADS, MLA_NOPE + MLA_V)
    w_uk = jnp.pad(w_ukv3[:, :, :MLA_NOPE], ((0, 0), (0, 0), (0, QK_PAD - MLA_NOPE)))
    w_uk = w_uk.reshape(KV_LORA, MLA_HEADS * QK_PAD).astype(bf)
    w_uv_t = w_ukv3[:, :, MLA_NOPE:].reshape(KV_LORA, D_MLA).T.astype(bf)
    w_uq_t = w_uq.T.astype(bf)

    q_t, k, v_t, gate_a_t, qs_t, ks, vs_t, gate_b_t = _project(
        x, w_tr, w_nat, g_q.reshape(Q_LORA, 1), g_kv.reshape(1, KV_LORA),
        w_uq_t, w_uk, w_uv_t, cos_t, sin_t, cos_n, sin_n)

    mixed_a_t = _mla_attention(q_t, k, v_t, gate_a_t)

    bucket, band = _band_geometry()
    bucket = jnp.asarray(np.where(band, bucket, -1).reshape(3, BLOCK, BLOCK), jnp.int32)
    bias_t = _bias_table(rel_bias.astype(f32), sink.astype(f32), bucket)

    mixed_b_t = _swa_attention(qs_t, ks, vs_t, bias_t, gate_b_t)

    w_out_bf = w_out.astype(bf)
    return _out_project(x, mixed_a_t, mixed_b_t, w_out_bf[:D_MLA], w_out_bf[D_MLA:],
                        ln_g.reshape(1, D_MODEL), ln_b.reshape(1, D_MODEL))
```

```python
import functools
import math

import jax
import jax.numpy as jnp
import numpy as np
from jax import lax
from jax.experimental import pallas as pl
from jax.experimental.pallas import tpu as pltpu

D_MODEL = 1024
MLA_HEADS = 8
MLA_NOPE = 64
MLA_ROPE = 32
MLA_V = 64
MLA_QK = MLA_NOPE + MLA_ROPE
Q_LORA = 256
KV_LORA = 128
D_MLA = MLA_HEADS * MLA_V
MLA_SCALE = 1.0 / math.sqrt(MLA_QK)
LOG2_E = math.log2(math.e)
MLA_Q_SCALE = MLA_SCALE * LOG2_E
BF16_SUBLANES = 16
MLA_V_ROWS = MLA_V + BF16_SUBLANES
ROPE_BASE = 10000.0

SWA_HEADS = 8
SWA_KV_HEADS = 2
SWA_HEAD_DIM = 64
SWA_GROUP = SWA_HEADS // SWA_KV_HEADS
D_SWA = SWA_HEADS * SWA_HEAD_DIM
WINDOW = 128
BLOCK = 128
SWA_SCALE = 1.0 / math.sqrt(SWA_HEAD_DIM)
SWA_Q_SCALE = SWA_SCALE * LOG2_E
SWA_V_ROWS = SWA_HEAD_DIM + BF16_SUBLANES
N_BUCKETS = 32
MAX_DISTANCE = 128

DEPTH = 1
ALPHA = (2.0 * DEPTH) ** 0.25

IN_SPLITS = (Q_LORA, KV_LORA, MLA_ROPE, D_MLA, D_SWA,
             SWA_KV_HEADS * SWA_HEAD_DIM, SWA_KV_HEADS * SWA_HEAD_DIM, D_SWA)
IN_OFFSETS = tuple(int(o) for o in np.cumsum((0,) + IN_SPLITS))

LANE = 128
SUBLANES = 8
QK_PAD = LANE
VMEM_LIMIT_BYTES = 56 * 1024 * 1024

PROJ_TOKENS = 1024
PROJ_PARTS = 4
MLA_TQ = 512
MLA_STEP_TILES = 8
MLA_TK = 256
MLA_AHEAD = 2
MLA_REFRESH = 8
MLA_MAX_EXCESS = 64.0
SWA_TOKENS = 2048
SWA_AHEAD = 1
SWA_MAX_EXCESS = 64.0
OUT_TOKENS = 1024
OUT_ROWS = 256

TR_CQ = 0
TR_GA = TR_CQ + Q_LORA
TR_QS = TR_GA + D_MLA
TR_VS = TR_QS + D_SWA
TR_GB = TR_VS + SWA_KV_HEADS * SWA_HEAD_DIM
TR_KR = TR_GB + D_SWA
TR_ROWS = TR_KR + MLA_ROPE
NAT_CKV = 0
NAT_KS = NAT_CKV + KV_LORA
NAT_COLS = NAT_KS + SWA_KV_HEADS * SWA_HEAD_DIM

_NT = (((1,), (1,)), ((), ()))
_TN = (((0,), (0,)), ((), ()))


def _rsqrt_mean_sq(x, axis, eps):
    return lax.rsqrt(jnp.mean(x * x, axis=axis, keepdims=True) + eps)


def _proj_kernel(x_ref, wtr_ref, wnat_ref, gq_ref, gkv_ref, wuqt_ref, wuk_ref, wuvt_ref,
                 cos_t_ref, sin_t_ref,
                 qt_ref, k_ref, vt_ref, gat_ref, qst_ref, ks_ref, vst_ref, gbt_ref,
                 tr_ref, nat_ref):
    n_tok = x_ref.shape[1] // PROJ_PARTS
    half = MLA_ROPE // 2

    def project(p):
        tok = slice(p * n_tok, (p + 1) * n_tok)
        xb = x_ref[0, tok, :].astype(jnp.bfloat16)
        tr_ref[:, tok] = lax.dot_general(wtr_ref[...], xb, _NT, preferred_element_type=jnp.float32)
        nat_ref[tok, :] = jnp.dot(xb, wnat_ref[...], preferred_element_type=jnp.float32)

    project(0)
    for p in range(PROJ_PARTS):
        if p + 1 < PROJ_PARTS:
            project(p + 1)
        _proj_finish(p, n_tok, tr_ref, nat_ref, gq_ref, gkv_ref, wuqt_ref, wuk_ref, wuvt_ref,
                     cos_t_ref, sin_t_ref,
                     qt_ref, k_ref, vt_ref, gat_ref, qst_ref, ks_ref, vst_ref, gbt_ref)


def _proj_finish(p, n_tok, tr_ref, nat_ref, gq_ref, gkv_ref, wuqt_ref, wuk_ref, wuvt_ref,
                 cos_t_ref, sin_t_ref,
                 qt_ref, k_ref, vt_ref, gat_ref, qst_ref, ks_ref, vst_ref, gbt_ref):
    tok = slice(p * n_tok, (p + 1) * n_tok)
    half = MLA_ROPE // 2

    cq = tr_ref[TR_CQ:TR_CQ + Q_LORA, tok]
    cqn = (cq * _rsqrt_mean_sq(cq, 0, 1e-6)) * gq_ref[...]
    q_t = jnp.dot(wuqt_ref[...], cqn.astype(jnp.bfloat16),
                  preferred_element_type=jnp.float32)
    cos_t = cos_t_ref[:, tok]
    sin_t = sin_t_ref[:, tok]
    zeros_pad = jnp.zeros((QK_PAD - MLA_QK, n_tok), jnp.bfloat16)
    for h in range(MLA_HEADS):
        base = h * MLA_QK
        nope = q_t[base:base + MLA_NOPE]
        r1 = q_t[base + MLA_NOPE:base + MLA_NOPE + half]
        r2 = q_t[base + MLA_NOPE + half:base + MLA_QK]
        qt_ref[0, h, 0:MLA_NOPE, tok] = (nope * MLA_Q_SCALE).astype(jnp.bfloat16)
        qt_ref[0, h, MLA_NOPE:MLA_NOPE + half, tok] = (
            (r1 * cos_t - r2 * sin_t) * MLA_Q_SCALE).astype(jnp.bfloat16)
        qt_ref[0, h, MLA_NOPE + half:MLA_QK, tok] = (
            (r2 * cos_t + r1 * sin_t) * MLA_Q_SCALE).astype(jnp.bfloat16)
        qt_ref[0, h, MLA_QK:QK_PAD, tok] = zeros_pad

    ckv = nat_ref[tok, NAT_CKV:NAT_CKV + KV_LORA]
    kvn = ((ckv * _rsqrt_mean_sq(ckv, 1, 1e-6)) * gkv_ref[...]).astype(jnp.bfloat16)
    k_nope = jnp.dot(kvn, wuk_ref[...], preferred_element_type=jnp.float32)
    v_t = lax.dot_general(wuvt_ref[...], kvn, _NT, preferred_element_type=jnp.float32)
    kr1 = tr_ref[TR_KR:TR_KR + half, tok]
    kr2 = tr_ref[TR_KR + half:TR_KR + MLA_ROPE, tok]
    k_rope = jnp.concatenate(
        [jnp.zeros((MLA_NOPE, n_tok), jnp.float32),
         kr1 * cos_t - kr2 * sin_t,
         kr2 * cos_t + kr1 * sin_t,
         jnp.zeros((QK_PAD - MLA_QK, n_tok), jnp.float32)], axis=0).T
    lane = lax.broadcasted_iota(jnp.int32, k_rope.shape, 1)
    row_id = lax.broadcasted_iota(jnp.int32, (BF16_SUBLANES, n_tok), 0)
    ones_row = jnp.where(row_id == 0, 1.0, 0.0).astype(jnp.bfloat16)
    for h in range(MLA_HEADS):
        pair = k_nope[:, (h // 2) * LANE:(h // 2 + 1) * LANE]
        if h % 2:
            pair = pltpu.roll(pair, MLA_NOPE, axis=1)
        k_ref[0, h, tok, :] = jnp.where(lane < MLA_NOPE, pair, k_rope).astype(jnp.bfloat16)
        vt_ref[0, h, 0:MLA_V, tok] = v_t[h * MLA_V:(h + 1) * MLA_V].astype(jnp.bfloat16)
        vt_ref[0, h, MLA_V:MLA_V_ROWS, tok] = ones_row

    gat_ref[0, :, tok] = jax.nn.silu(tr_ref[TR_GA:TR_GA + D_MLA, tok]).astype(jnp.bfloat16)
    gbt_ref[0, :, tok] = jax.nn.silu(tr_ref[TR_GB:TR_GB + D_SWA, tok]).astype(jnp.bfloat16)

    n_blocks = n_tok // BLOCK
    for g in range(SWA_KV_HEADS):
        ks_ref[0, g, tok, :] = nat_ref[tok, NAT_KS + g * SWA_HEAD_DIM:
                                       NAT_KS + (g + 1) * SWA_HEAD_DIM].astype(jnp.bfloat16)
        vst_ref[0, g, 0:SWA_HEAD_DIM, tok] = tr_ref[TR_VS + g * SWA_HEAD_DIM:
                                                    TR_VS + (g + 1) * SWA_HEAD_DIM, tok].astype(jnp.bfloat16)
        vst_ref[0, g, SWA_HEAD_DIM:SWA_V_ROWS, tok] = ones_row
        for hh in range(SWA_GROUP):
            row = TR_QS + (g * SWA_GROUP + hh) * SWA_HEAD_DIM
            q_h = (tr_ref[row:row + SWA_HEAD_DIM, tok] * SWA_Q_SCALE).astype(jnp.bfloat16)
            for nb in range(n_blocks):
                col = ((p * n_blocks + nb) * SWA_GROUP + hh) * BLOCK
                qst_ref[0, g, :, col:col + BLOCK] = q_h[:, nb * BLOCK:(nb + 1) * BLOCK]


def _project(x, w_tr, w_nat, g_q, g_kv, w_uq_t, w_uk, w_uv_t, cos_t, sin_t):
    B, S, _ = x.shape
    T = PROJ_TOKENS
    grid = (B, S // T)
    const = lambda *shape: pl.BlockSpec(shape, lambda b, t: (0,) * len(shape))
    bf = jnp.bfloat16
    out_shape = (
        jax.ShapeDtypeStruct((B, MLA_HEADS, QK_PAD, S), bf),
        jax.ShapeDtypeStruct((B, MLA_HEADS, S, QK_PAD), bf),
        jax.ShapeDtypeStruct((B, MLA_HEADS, MLA_V_ROWS, S), bf),
        jax.ShapeDtypeStruct((B, D_MLA, S), bf),
        jax.ShapeDtypeStruct((B, SWA_KV_HEADS, SWA_HEAD_DIM, SWA_GROUP * S), bf),
        jax.ShapeDtypeStruct((B, SWA_KV_HEADS, S, SWA_HEAD_DIM), bf),
        jax.ShapeDtypeStruct((B, SWA_KV_HEADS, SWA_V_ROWS, S), bf),
        jax.ShapeDtypeStruct((B, D_SWA, S), bf),
    )
    out_specs = (
        pl.BlockSpec((1, MLA_HEADS, QK_PAD, T), lambda b, t: (b, 0, 0, t)),
        pl.BlockSpec((1, MLA_HEADS, T, QK_PAD), lambda b, t: (b, 0, t, 0)),
        pl.BlockSpec((1, MLA_HEADS, MLA_V_ROWS, T), lambda b, t: (b, 0, 0, t)),
        pl.BlockSpec((1, D_MLA, T), lambda b, t: (b, 0, t)),
        pl.BlockSpec((1, SWA_KV_HEADS, SWA_HEAD_DIM, SWA_GROUP * T), lambda b, t: (b, 0, 0, t)),
        pl.BlockSpec((1, SWA_KV_HEADS, T, SWA_HEAD_DIM), lambda b, t: (b, 0, t, 0)),
        pl.BlockSpec((1, SWA_KV_HEADS, SWA_V_ROWS, T), lambda b, t: (b, 0, 0, t)),
        pl.BlockSpec((1, D_SWA, T), lambda b, t: (b, 0, t)),
    )
    in_specs = [
        pl.BlockSpec((1, T, D_MODEL), lambda b, t: (b, t, 0)),
        const(TR_ROWS, D_MODEL),
        const(D_MODEL, NAT_COLS),
        const(Q_LORA, 1),
        const(1, KV_LORA),
        const(MLA_HEADS * MLA_QK, Q_LORA),
        const(KV_LORA, MLA_HEADS * MLA_NOPE),
        const(D_MLA, KV_LORA),
        pl.BlockSpec((MLA_ROPE // 2, T), lambda b, t: (0, t)),
        pl.BlockSpec((MLA_ROPE // 2, T), lambda b, t: (0, t)),
    ]
    return pl.pallas_call(
        _proj_kernel,
        grid=grid,
        in_specs=in_specs,
        out_specs=out_specs,
        out_shape=out_shape,
        scratch_shapes=[pltpu.VMEM((TR_ROWS, T), jnp.float32),
                        pltpu.VMEM((T, NAT_COLS), jnp.float32)],
        compiler_params=pltpu.CompilerParams(
            dimension_semantics=("arbitrary", "arbitrary"),
            vmem_limit_bytes=VMEM_LIMIT_BYTES),
        name="proj",
    )(x, w_tr, w_nat, g_q, g_kv, w_uq_t, w_uk, w_uv_t, cos_t, sin_t)


def _mla_kernel(qt_ref, k_ref, vt_ref, gate_ref, o_ref, m_ref, acc_ref):
    n_chunks = k_ref.shape[2] // MLA_TK
    n_tiles = qt_ref.shape[3] // MLA_TQ

    def q_tile(t):
        return qt_ref[0, 0, :, t * MLA_TQ:(t + 1) * MLA_TQ]

    def chunk_slice(c):
        start = c * MLA_TK
        return pl.ds(start if isinstance(c, int) else pl.multiple_of(start, MLA_TK), MLA_TK)

    def key_chunk(c):
        return k_ref[0, 0, chunk_slice(c), :]

    def value_chunk(c):
        return vt_ref[0, 0, :, chunk_slice(c)]

    items = [(t, c) for t in range(n_tiles) for c in range(n_chunks)]
    m_use = [None] * n_tiles
    seen = [None] * n_tiles
    acc = [None] * n_tiles
    excess = None

    def issue_scores(i):
        t, c = items[i]
        if c == 0:
            s0 = jnp.dot(k_ref[0, 0, 0:BF16_SUBLANES, :], q_tile(t),
                         preferred_element_type=jnp.float32)
            m_use[t] = jnp.max(s0, axis=0, keepdims=True)
        return jnp.dot(key_chunk(c), q_tile(t), preferred_element_type=jnp.float32)

    pending = [issue_scores(i) for i in range(MLA_AHEAD)]
    for i, (t, c) in enumerate(items):
        if i + MLA_AHEAD < len(items):
            pending.append(issue_scores(i + MLA_AHEAD))
        s = pending.pop(0)
        p = jnp.exp2(s - m_use[t]).astype(jnp.bfloat16)
        pv = jnp.dot(value_chunk(c), p, preferred_element_type=jnp.float32)
        acc[t] = pv if c == 0 else acc[t] + pv
        smax = jnp.max(s.reshape(MLA_TK // SUBLANES, SUBLANES, MLA_TQ), axis=0)
        seen[t] = smax if seen[t] is None else jnp.maximum(seen[t], smax)
        last = c + 1 == n_chunks
        if last or c == 0 or (c + 1) % MLA_REFRESH == 0:
            over = seen[t] - m_use[t]
            excess = over if excess is None else jnp.maximum(excess, over)
            if last:
                acc_ref[t] = acc[t]
            else:
                m_next = jnp.maximum(m_use[t], jnp.max(seen[t], axis=0, keepdims=True))
                acc[t] = acc[t] * jnp.exp2(m_use[t] - m_next)
                m_use[t] = m_next
                seen[t] = None

    @pl.when(jnp.max(excess) > MLA_MAX_EXCESS)
    def _():
        for t in range(n_tiles):
            m_ref[...] = jnp.full(m_ref.shape, -jnp.inf, jnp.float32)
            acc_ref[t] = jnp.zeros(acc_ref.shape[1:], jnp.float32)

            def chunk(c, carry, t=t):
                s = jnp.dot(key_chunk(c), q_tile(t), preferred_element_type=jnp.float32)
                m_prev = m_ref[...]
                m_new = jnp.maximum(m_prev, jnp.max(s, axis=0, keepdims=True))
                p = jnp.exp2(s - m_new).astype(jnp.bfloat16)
                acc_ref[t] = jnp.exp2(m_prev - m_new) * acc_ref[t] + jnp.dot(
                    value_chunk(c), p, preferred_element_type=jnp.float32)
                m_ref[...] = m_new
                return carry

            lax.fori_loop(0, n_chunks, chunk, 0)

    for t in range(n_tiles):
        cols = slice(t * MLA_TQ, (t + 1) * MLA_TQ)
        o = acc_ref[t, 0:MLA_V, :] / acc_ref[t, MLA_V:MLA_V + 1, :]
        o_ref[0, :, cols] = (o * gate_ref[0, :, cols].astype(jnp.float32)).astype(o_ref.dtype)


def _mla_attention(q_t, k, v_t, gate_t):
    B, H, _, S = q_t.shape
    TQ = MLA_TQ
    step_q = MLA_STEP_TILES * TQ
    grid = (B, H, S // step_q)
    return pl.pallas_call(
        _mla_kernel,
        grid=grid,
        in_specs=[
            pl.BlockSpec((1, 1, QK_PAD, step_q), lambda b, h, i: (b, h, 0, i)),
            pl.BlockSpec((1, 1, S, QK_PAD), lambda b, h, i: (b, h, 0, 0)),
            pl.BlockSpec((1, 1, MLA_V_ROWS, S), lambda b, h, i: (b, h, 0, 0)),
            pl.BlockSpec((1, MLA_V, step_q), lambda b, h, i: (b, h, i)),
        ],
        out_specs=pl.BlockSpec((1, MLA_V, step_q), lambda b, h, i: (b, h, i)),
        out_shape=jax.ShapeDtypeStruct((B, D_MLA, S), jnp.bfloat16),
        scratch_shapes=[pltpu.VMEM((1, TQ), jnp.float32),
                        pltpu.VMEM((MLA_STEP_TILES, MLA_V_ROWS, TQ), jnp.float32)],
        compiler_params=pltpu.CompilerParams(
            dimension_semantics=("arbitrary", "arbitrary", "arbitrary"),
            vmem_limit_bytes=VMEM_LIMIT_BYTES),
        name="mla",
    )(q_t, k, v_t, gate_t)


def _bias_kernel(rel_ref, sink_ref, bucket_ref, o_ref):
    g = pl.program_id(0)
    for part in range(3):
        bucket = bucket_ref[part]
        for hh in range(SWA_GROUP):
            head = g * SWA_GROUP + hh
            tile = jnp.full((BLOCK, BLOCK), -jnp.inf, jnp.float32)
            for b in range(N_BUCKETS):
                tile = jnp.where(bucket == b, (rel_ref[b, head] - sink_ref[head]) * LOG2_E, tile)
            o_ref[0, part, :, hh * BLOCK:(hh + 1) * BLOCK] = tile


def _bias_table(rel_bias, sink, bucket):
    return pl.pallas_call(
        _bias_kernel,
        grid=(SWA_KV_HEADS,),
        in_specs=[pl.BlockSpec(memory_space=pltpu.SMEM),
                  pl.BlockSpec(memory_space=pltpu.SMEM),
                  pl.BlockSpec((3, BLOCK, BLOCK), lambda g: (0, 0, 0))],
        out_specs=pl.BlockSpec((1, 3, BLOCK, SWA_GROUP * BLOCK), lambda g: (g, 0, 0, 0)),
        out_shape=jax.ShapeDtypeStruct((SWA_KV_HEADS, 3, BLOCK, SWA_GROUP * BLOCK), jnp.float32),
        compiler_params=pltpu.CompilerParams(dimension_semantics=("arbitrary",)),
        name="bias",
    )(rel_bias, sink, bucket)


def _swa_kernel(qst_ref, ks_ref, vst_ref, bias_ref, gate_ref, o_ref):
    n_total = ks_ref.shape[2] // BLOCK
    n_local = o_ref.shape[2] // BLOCK
    step = pl.program_id(2)
    n_steps = pl.num_programs(2)
    neg_inf = jnp.float32(-jnp.inf)

    def band_starts(j):
        n = step * n_local + j
        return tuple(pl.multiple_of(b * BLOCK, BLOCK)
                     for b in (jnp.maximum(n - 1, 0), n, jnp.minimum(n + 1, n_total - 1)))

    def lane_block(j, width):
        start = j * width
        return pl.ds(start if isinstance(j, int) else pl.multiple_of(start, width), width)

    def band_scores(j):
        q_t = qst_ref[0, 0, :, lane_block(j, SWA_GROUP * BLOCK)]
        n = step * n_local + j
        penalties = [None, None, None]
        if not isinstance(j, int) or j == 0:
            penalties[0] = jnp.where(n > 0, 0.0, neg_inf)
        if not isinstance(j, int) or j == n_local - 1:
            penalties[2] = jnp.where(n < n_total - 1, 0.0, neg_inf)
        scores = []
        for part, start in enumerate(band_starts(j)):
            k = ks_ref[0, 0, pl.ds(start, BLOCK), :]
            s = jnp.dot(k, q_t, preferred_element_type=jnp.float32) + bias_ref[0, part]
            if penalties[part] is not None:
                s = s + penalties[part]
            scores.append(s)
        return scores

    def weighted_values(j, weights):
        acc = None
        for part, start in enumerate(band_starts(j)):
            v_t = vst_ref[0, 0, :, pl.ds(start, BLOCK)]
            pv = jnp.dot(v_t, weights[part].astype(jnp.bfloat16),
                         preferred_element_type=jnp.float32)
            acc = pv if acc is None else acc + pv
        return acc

    def write_block(j, acc, sink_weight):
        o = acc[0:SWA_HEAD_DIM] / (acc[SWA_HEAD_DIM:SWA_HEAD_DIM + 1] + sink_weight)
        cols = lane_block(j, BLOCK)
        for hh in range(SWA_GROUP):
            rows = slice(hh * SWA_HEAD_DIM, (hh + 1) * SWA_HEAD_DIM)
            gate = gate_ref[0, rows, cols].astype(jnp.float32)
            o_ref[0, rows, cols] = (o[:, hh * BLOCK:(hh + 1) * BLOCK] * gate).astype(o_ref.dtype)

    excess = None
    pending = [band_scores(j) for j in range(SWA_AHEAD)]
    for j in range(n_local):
        if j + SWA_AHEAD < n_local:
            pending.append(band_scores(j + SWA_AHEAD))
        scores = pending.pop(0)
        for s in scores:
            smax = jnp.max(s.reshape(BLOCK // SUBLANES, SUBLANES, s.shape[1]), axis=0)
            excess = smax if excess is None else jnp.maximum(excess, smax)
        write_block(j, weighted_values(j, [jnp.exp2(s) for s in scores]), 1.0)

    @pl.when(jnp.max(excess) > SWA_MAX_EXCESS)
    def _():
        def exact_block(j, carry):
            scores = band_scores(j)
            m = jnp.zeros((1, scores[0].shape[1]), jnp.float32)
            for s in scores:
                m = jnp.maximum(m, jnp.max(s, axis=0, keepdims=True))
            write_block(j, weighted_values(j, [jnp.exp2(s - m) for s in scores]), jnp.exp2(-m))
            return carry

        lax.fori_loop(0, n_local, exact_block, 0)


def _swa_attention(qs_t, ks, vs_t, bias_t, gate_t):
    B, G, _, S = vs_t.shape
    T = SWA_TOKENS
    grid = (B, G, S // T)
    rows = SWA_GROUP * SWA_HEAD_DIM
    return pl.pallas_call(
        _swa_kernel,
        grid=grid,
        in_specs=[
            pl.BlockSpec((1, 1, SWA_HEAD_DIM, SWA_GROUP * T), lambda b, g, t: (b, g, 0, t)),
            pl.BlockSpec((1, 1, S, SWA_HEAD_DIM), lambda b, g, t: (b, g, 0, 0)),
            pl.BlockSpec((1, 1, SWA_V_ROWS, S), lambda b, g, t: (b, g, 0, 0)),
            pl.BlockSpec((1, 3, BLOCK, SWA_GROUP * BLOCK), lambda b, g, t: (g, 0, 0, 0)),
            pl.BlockSpec((1, rows, T), lambda b, g, t: (b, g, t)),
        ],
        out_specs=pl.BlockSpec((1, rows, T), lambda b, g, t: (b, g, t)),
        out_shape=jax.ShapeDtypeStruct((B, D_SWA, S), jnp.bfloat16),
        compiler_params=pltpu.CompilerParams(
            dimension_semantics=("arbitrary", "arbitrary", "arbitrary"),
            vmem_limit_bytes=VMEM_LIMIT_BYTES),
        name="swa",
    )(qs_t, ks, vs_t, bias_t, gate_t)


def _out_kernel(x_ref, ma_ref, mb_ref, wa_ref, wb_ref, g_ref, b_ref, y_ref):
    n_chunks = x_ref.shape[1] // OUT_ROWS

    def project(c):
        cols = slice(c * OUT_ROWS, (c + 1) * OUT_ROWS)
        out = lax.dot_general(ma_ref[0, :, cols], wa_ref[...], _TN,
                              preferred_element_type=jnp.float32)
        return out + lax.dot_general(mb_ref[0, :, cols], wb_ref[...], _TN,
                                     preferred_element_type=jnp.float32)

    out_next = project(0)
    for c in range(n_chunks):
        out = out_next
        if c + 1 < n_chunks:
            out_next = project(c + 1)
        rows = slice(c * OUT_ROWS, (c + 1) * OUT_ROWS)
        h = ALPHA * x_ref[0, rows, :] + out
        mu = jnp.mean(h, axis=-1, keepdims=True)
        var = jnp.mean(jnp.square(h - mu), axis=-1, keepdims=True)
        y_ref[0, rows, :] = ((h - mu) * lax.rsqrt(var + 1e-5)) * g_ref[...] + b_ref[...]


def _out_project(x, mixed_a_t, mixed_b_t, w_a, w_b, ln_g, ln_b):
    B, S, D = x.shape
    T = OUT_TOKENS
    grid = (B, S // T)
    const = lambda *shape: pl.BlockSpec(shape, lambda b, t: (0,) * len(shape))
    return pl.pallas_call(
        _out_kernel,
        grid=grid,
        in_specs=[
            pl.BlockSpec((1, T, D), lambda b, t: (b, t, 0)),
            pl.BlockSpec((1, D_MLA, T), lambda b, t: (b, 0, t)),
            pl.BlockSpec((1, D_SWA, T), lambda b, t: (b, 0, t)),
            const(D_MLA, D),
            const(D_SWA, D),
            const(1, D),
            const(1, D),
        ],
        out_specs=pl.BlockSpec((1, T, D), lambda b, t: (b, t, 0)),
        out_shape=jax.ShapeDtypeStruct((B, S, D), x.dtype),
        compiler_params=pltpu.CompilerParams(
            dimension_semantics=("arbitrary", "arbitrary"),
            vmem_limit_bytes=VMEM_LIMIT_BYTES),
        name="outproj",
    )(x, mixed_a_t, mixed_b_t, w_a, w_b, ln_g, ln_b)


def _t5_bucket(rel):
    half = N_BUCKETS // 2
    ret = np.where(rel > 0, half, 0)
    n = np.abs(rel)
    max_exact = half // 2
    large = max_exact + (np.log(np.maximum(n, 1).astype(np.float32) / max_exact)
                         / np.log(MAX_DISTANCE / max_exact) * (half - max_exact)).astype(np.int32)
    large = np.minimum(large, half - 1)
    return (ret + np.where(n < max_exact, n, large)).astype(np.int32)


def _band_geometry():
    q_loc = np.arange(BLOCK)
    k_loc = np.arange(3 * BLOCK) - BLOCK
    rel = k_loc[:, None] - q_loc[None, :]
    return _t5_bucket(rel), np.abs(rel) <= WINDOW


def _col(w_in, i):
    return w_in[:, IN_OFFSETS[i]:IN_OFFSETS[i + 1]]


def kernel(x, w_in, g_q, g_kv, w_uq, w_ukv, sink, rel_bias, w_out, ln_g, ln_b):
    B, S, _ = x.shape
    bf = jnp.bfloat16
    f32 = jnp.float32

    pos = jnp.arange(S, dtype=f32)
    inv_freq = ROPE_BASE ** (-jnp.arange(0, MLA_ROPE, 2, dtype=f32) / MLA_ROPE)
    ang_t = inv_freq[:, None] * pos[None, :]
    cos_t, sin_t = jnp.cos(ang_t), jnp.sin(ang_t)

    c_q, c_kv, k_rope, gate_a, q_s, k_s, v_s, gate_b = (_col(w_in, i) for i in range(8))
    w_tr = jnp.concatenate([c_q, gate_a, q_s, v_s, gate_b, k_rope], axis=1).T.astype(bf)
    w_nat = jnp.concatenate([c_kv, k_s], axis=1).astype(bf)
    w_ukv3 = w_ukv.reshape(KV_LORA, MLA_HEADS, MLA_NOPE + MLA_V)
    w_uk = w_ukv3[:, :, :MLA_NOPE].reshape(KV_LORA, MLA_HEADS * MLA_NOPE).astype(bf)
    w_uv_t = w_ukv3[:, :, MLA_NOPE:].reshape(KV_LORA, D_MLA).T.astype(bf)
    w_uq_t = w_uq.T.astype(bf)

    q_t, k, v_t, gate_a_t, qs_t, ks, vs_t, gate_b_t = _project(
        x, w_tr, w_nat, g_q.reshape(Q_LORA, 1), g_kv.reshape(1, KV_LORA),
        w_uq_t, w_uk, w_uv_t, cos_t, sin_t)

    mixed_a_t = _mla_attention(q_t, k, v_t, gate_a_t)

    bucket, band = _band_geometry()
    bucket = jnp.asarray(np.where(band, bucket, -1).reshape(3, BLOCK, BLOCK), jnp.int32)
    bias_t = _bias_table(rel_bias.astype(f32), sink.astype(f32), bucket)

    mixed_b_t = _swa_attention(qs_t, ks, vs_t, bias_t, gate_b_t)

    w_out_bf = w_out.astype(bf)
    return _out_project(x, mixed_a_t, mixed_b_t, w_out_bf[:D_MLA], w_out_bf[D_MLA:],
                        ln_g.reshape(1, D_MODEL), ln_b.reshape(1, D_MODEL))
```

```python
import functools
import math

import jax
import jax.numpy as jnp
import numpy as np
from jax import lax
from jax.experimental import pallas as pl
from jax.experimental.pallas import tpu as pltpu

D_MODEL = 1024
MLA_HEADS = 8
MLA_NOPE = 64
MLA_ROPE = 32
MLA_V = 64
MLA_QK = MLA_NOPE + MLA_ROPE
Q_LORA = 256
KV_LORA = 128
D_MLA = MLA_HEADS * MLA_V
MLA_SCALE = 1.0 / math.sqrt(MLA_QK)
LOG2_E = math.log2(math.e)
MLA_Q_SCALE = MLA_SCALE * LOG2_E
BF16_SUBLANES = 16
MLA_V_ROWS = MLA_V + BF16_SUBLANES
ROPE_BASE = 10000.0

SWA_HEADS = 8
SWA_KV_HEADS = 2
SWA_HEAD_DIM = 64
SWA_GROUP = SWA_HEADS // SWA_KV_HEADS
D_SWA = SWA_HEADS * SWA_HEAD_DIM
WINDOW = 128
BLOCK = 128
SWA_SCALE = 1.0 / math.sqrt(SWA_HEAD_DIM)
SWA_Q_SCALE = SWA_SCALE * LOG2_E
SWA_V_ROWS = SWA_HEAD_DIM + BF16_SUBLANES
N_BUCKETS = 32
MAX_DISTANCE = 128

DEPTH = 1
ALPHA = (2.0 * DEPTH) ** 0.25

IN_SPLITS = (Q_LORA, KV_LORA, MLA_ROPE, D_MLA, D_SWA,
             SWA_KV_HEADS * SWA_HEAD_DIM, SWA_KV_HEADS * SWA_HEAD_DIM, D_SWA)
IN_OFFSETS = tuple(int(o) for o in np.cumsum((0,) + IN_SPLITS))

LANE = 128
SUBLANES = 8
QK_PAD = LANE
VMEM_LIMIT_BYTES = 56 * 1024 * 1024

PROJ_TOKENS = 1024
PROJ_PARTS = 4
MLA_TQ = 512
MLA_STEP_TILES = 4
MLA_TK = 256
MLA_AHEAD = 2
MLA_REFRESH = 8
MLA_MAX_EXCESS = 64.0
SWA_TOKENS = 2048
SWA_AHEAD = 1
SWA_MAX_EXCESS = 64.0
OUT_TOKENS = 1024
OUT_ROWS = 512

TR_CQ = 0
TR_QS = TR_CQ + Q_LORA
TR_VS = TR_QS + D_SWA
TR_KR = TR_VS + SWA_KV_HEADS * SWA_HEAD_DIM
TR_ROWS = TR_KR + MLA_ROPE
NAT_CKV = 0
NAT_KS = NAT_CKV + KV_LORA
NAT_COLS = NAT_KS + SWA_KV_HEADS * SWA_HEAD_DIM

_NT = (((1,), (1,)), ((), ()))
_TN = (((0,), (0,)), ((), ()))


def _rsqrt_mean_sq(x, axis, eps):
    return lax.rsqrt(jnp.mean(x * x, axis=axis, keepdims=True) + eps)


def _proj_kernel(x_ref, wtr_ref, wnat_ref, gq_ref, gkv_ref, wuqt_ref, wuk_ref, wuvt_ref,
                 cos_t_ref, sin_t_ref,
                 qt_ref, k_ref, vt_ref, qst_ref, ks_ref, vst_ref,
                 tr_ref, nat_ref):
    n_tok = x_ref.shape[1] // PROJ_PARTS
    half = MLA_ROPE // 2

    def project(p):
        tok = slice(p * n_tok, (p + 1) * n_tok)
        xb = x_ref[0, tok, :].astype(jnp.bfloat16)
        tr_ref[:, tok] = lax.dot_general(wtr_ref[...], xb, _NT, preferred_element_type=jnp.float32)
        nat_ref[tok, :] = jnp.dot(xb, wnat_ref[...], preferred_element_type=jnp.float32)

    project(0)
    for p in range(PROJ_PARTS):
        if p + 1 < PROJ_PARTS:
            project(p + 1)
        _proj_finish(p, n_tok, tr_ref, nat_ref, gq_ref, gkv_ref, wuqt_ref, wuk_ref, wuvt_ref,
                     cos_t_ref, sin_t_ref,
                     qt_ref, k_ref, vt_ref, qst_ref, ks_ref, vst_ref)


def _proj_finish(p, n_tok, tr_ref, nat_ref, gq_ref, gkv_ref, wuqt_ref, wuk_ref, wuvt_ref,
                 cos_t_ref, sin_t_ref,
                 qt_ref, k_ref, vt_ref, qst_ref, ks_ref, vst_ref):
    tok = slice(p * n_tok, (p + 1) * n_tok)
    half = MLA_ROPE // 2

    cq = tr_ref[TR_CQ:TR_CQ + Q_LORA, tok]
    cqn = (cq * _rsqrt_mean_sq(cq, 0, 1e-6)) * gq_ref[...]
    q_t = jnp.dot(wuqt_ref[...], cqn.astype(jnp.bfloat16),
                  preferred_element_type=jnp.float32)
    cos_t = cos_t_ref[:, tok]
    sin_t = sin_t_ref[:, tok]
    zeros_pad = jnp.zeros((QK_PAD - MLA_QK, n_tok), jnp.bfloat16)
    for h in range(MLA_HEADS):
        base = h * MLA_QK
        nope = q_t[base:base + MLA_NOPE]
        r1 = q_t[base + MLA_NOPE:base + MLA_NOPE + half]
        r2 = q_t[base + MLA_NOPE + half:base + MLA_QK]
        qt_ref[0, h, 0:MLA_NOPE, tok] = (nope * MLA_Q_SCALE).astype(jnp.bfloat16)
        qt_ref[0, h, MLA_NOPE:MLA_NOPE + half, tok] = (
            (r1 * cos_t - r2 * sin_t) * MLA_Q_SCALE).astype(jnp.bfloat16)
        qt_ref[0, h, MLA_NOPE + half:MLA_QK, tok] = (
            (r2 * cos_t + r1 * sin_t) * MLA_Q_SCALE).astype(jnp.bfloat16)
        qt_ref[0, h, MLA_QK:QK_PAD, tok] = zeros_pad

    ckv = nat_ref[tok, NAT_CKV:NAT_CKV + KV_LORA]
    kvn = ((ckv * _rsqrt_mean_sq(ckv, 1, 1e-6)) * gkv_ref[...]).astype(jnp.bfloat16)
    k_nope = jnp.dot(kvn, wuk_ref[...], preferred_element_type=jnp.float32)
    v_t = lax.dot_general(wuvt_ref[...], kvn, _NT, preferred_element_type=jnp.float32)
    kr1 = tr_ref[TR_KR:TR_KR + half, tok]
    kr2 = tr_ref[TR_KR + half:TR_KR + MLA_ROPE, tok]
    k_rope = jnp.concatenate(
        [jnp.zeros((MLA_NOPE, n_tok), jnp.float32),
         kr1 * cos_t - kr2 * sin_t,
         kr2 * cos_t + kr1 * sin_t,
         jnp.zeros((QK_PAD - MLA_QK, n_tok), jnp.float32)], axis=0).T
    lane = lax.broadcasted_iota(jnp.int32, k_rope.shape, 1)
    row_id = lax.broadcasted_iota(jnp.int32, (BF16_SUBLANES, n_tok), 0)
    ones_row = jnp.where(row_id == 0, 1.0, 0.0).astype(jnp.bfloat16)
    for h in range(MLA_HEADS):
        pair = k_nope[:, (h // 2) * LANE:(h // 2 + 1) * LANE]
        if h % 2:
            pair = pltpu.roll(pair, MLA_NOPE, axis=1)
        k_ref[0, h, tok, :] = jnp.where(lane < MLA_NOPE, pair, k_rope).astype(jnp.bfloat16)
        vt_ref[0, h, 0:MLA_V, tok] = v_t[h * MLA_V:(h + 1) * MLA_V].astype(jnp.bfloat16)
        vt_ref[0, h, MLA_V:MLA_V_ROWS, tok] = ones_row

    n_blocks = n_tok // BLOCK
    for g in range(SWA_KV_HEADS):
        ks_ref[0, g, tok, :] = nat_ref[tok, NAT_KS + g * SWA_HEAD_DIM:
                                       NAT_KS + (g + 1) * SWA_HEAD_DIM].astype(jnp.bfloat16)
        vst_ref[0, g, 0:SWA_HEAD_DIM, tok] = tr_ref[TR_VS + g * SWA_HEAD_DIM:
                                                    TR_VS + (g + 1) * SWA_HEAD_DIM, tok].astype(jnp.bfloat16)
        vst_ref[0, g, SWA_HEAD_DIM:SWA_V_ROWS, tok] = ones_row
        for hh in range(SWA_GROUP):
            row = TR_QS + (g * SWA_GROUP + hh) * SWA_HEAD_DIM
            q_h = (tr_ref[row:row + SWA_HEAD_DIM, tok] * SWA_Q_SCALE).astype(jnp.bfloat16)
            for nb in range(n_blocks):
                col = ((p * n_blocks + nb) * SWA_GROUP + hh) * BLOCK
                qst_ref[0, g, :, col:col + BLOCK] = q_h[:, nb * BLOCK:(nb + 1) * BLOCK]


def _project(x, w_tr, w_nat, g_q, g_kv, w_uq_t, w_uk, w_uv_t, cos_t, sin_t):
    B, S, _ = x.shape
    T = PROJ_TOKENS
    grid = (B, S // T)
    const = lambda *shape: pl.BlockSpec(shape, lambda b, t: (0,) * len(shape))
    bf = jnp.bfloat16
    out_shape = (
        jax.ShapeDtypeStruct((B, MLA_HEADS, QK_PAD, S), bf),
        jax.ShapeDtypeStruct((B, MLA_HEADS, S, QK_PAD), bf),
        jax.ShapeDtypeStruct((B, MLA_HEADS, MLA_V_ROWS, S), bf),
        jax.ShapeDtypeStruct((B, SWA_KV_HEADS, SWA_HEAD_DIM, SWA_GROUP * S), bf),
        jax.ShapeDtypeStruct((B, SWA_KV_HEADS, S, SWA_HEAD_DIM), bf),
        jax.ShapeDtypeStruct((B, SWA_KV_HEADS, SWA_V_ROWS, S), bf),
    )
    out_specs = (
        pl.BlockSpec((1, MLA_HEADS, QK_PAD, T), lambda b, t: (b, 0, 0, t)),
        pl.BlockSpec((1, MLA_HEADS, T, QK_PAD), lambda b, t: (b, 0, t, 0)),
        pl.BlockSpec((1, MLA_HEADS, MLA_V_ROWS, T), lambda b, t: (b, 0, 0, t)),
        pl.BlockSpec((1, SWA_KV_HEADS, SWA_HEAD_DIM, SWA_GROUP * T), lambda b, t: (b, 0, 0, t)),
        pl.BlockSpec((1, SWA_KV_HEADS, T, SWA_HEAD_DIM), lambda b, t: (b, 0, t, 0)),
        pl.BlockSpec((1, SWA_KV_HEADS, SWA_V_ROWS, T), lambda b, t: (b, 0, 0, t)),
    )
    in_specs = [
        pl.BlockSpec((1, T, D_MODEL), lambda b, t: (b, t, 0)),
        const(TR_ROWS, D_MODEL),
        const(D_MODEL, NAT_COLS),
        const(Q_LORA, 1),
        const(1, KV_LORA),
        const(MLA_HEADS * MLA_QK, Q_LORA),
        const(KV_LORA, MLA_HEADS * MLA_NOPE),
        const(D_MLA, KV_LORA),
        pl.BlockSpec((MLA_ROPE // 2, T), lambda b, t: (0, t)),
        pl.BlockSpec((MLA_ROPE // 2, T), lambda b, t: (0, t)),
    ]
    return pl.pallas_call(
        _proj_kernel,
        grid=grid,
        in_specs=in_specs,
        out_specs=out_specs,
        out_shape=out_shape,
        scratch_shapes=[pltpu.VMEM((TR_ROWS, T), jnp.float32),
                        pltpu.VMEM((T, NAT_COLS), jnp.float32)],
        compiler_params=pltpu.CompilerParams(
            dimension_semantics=("arbitrary", "arbitrary"),
            vmem_limit_bytes=VMEM_LIMIT_BYTES),
        name="proj",
    )(x, w_tr, w_nat, g_q, g_kv, w_uq_t, w_uk, w_uv_t, cos_t, sin_t)


def _mla_kernel(qt_ref, k_ref, vt_ref, o_ref, m_ref, acc_ref):
    n_chunks = k_ref.shape[2] // MLA_TK
    n_tiles = qt_ref.shape[3] // MLA_TQ

    def q_tile(t):
        return qt_ref[0, 0, :, t * MLA_TQ:(t + 1) * MLA_TQ]

    def chunk_slice(c):
        start = c * MLA_TK
        return pl.ds(start if isinstance(c, int) else pl.multiple_of(start, MLA_TK), MLA_TK)

    def key_chunk(c):
        return k_ref[0, 0, chunk_slice(c), :]

    def value_chunk(c):
        return vt_ref[0, 0, :, chunk_slice(c)]

    items = [(t, c) for t in range(n_tiles) for c in range(n_chunks)]
    m_use = [None] * n_tiles
    seen = [None] * n_tiles
    acc = [None] * n_tiles
    excess = None

    def issue_scores(i):
        t, c = items[i]
        if c == 0:
            s0 = jnp.dot(k_ref[0, 0, 0:BF16_SUBLANES, :], q_tile(t),
                         preferred_element_type=jnp.float32)
            m_use[t] = jnp.max(s0, axis=0, keepdims=True)
        return jnp.dot(key_chunk(c), q_tile(t), preferred_element_type=jnp.float32)

    pending = [issue_scores(i) for i in range(MLA_AHEAD)]
    for i, (t, c) in enumerate(items):
        if i + MLA_AHEAD < len(items):
            pending.append(issue_scores(i + MLA_AHEAD))
        s = pending.pop(0)
        p = jnp.exp2(s - m_use[t]).astype(jnp.bfloat16)
        pv = jnp.dot(value_chunk(c), p, preferred_element_type=jnp.float32)
        acc[t] = pv if c == 0 else acc[t] + pv
        smax = jnp.max(s.reshape(MLA_TK // SUBLANES, SUBLANES, MLA_TQ), axis=0)
        seen[t] = smax if seen[t] is None else jnp.maximum(seen[t], smax)
        last = c + 1 == n_chunks
        if last or c == 0 or (c + 1) % MLA_REFRESH == 0:
            over = seen[t] - m_use[t]
            excess = over if excess is None else jnp.maximum(excess, over)
            if last:
                acc_ref[t] = acc[t]
            else:
                m_next = jnp.maximum(m_use[t], jnp.max(seen[t], axis=0, keepdims=True))
                acc[t] = acc[t] * jnp.exp2(m_use[t] - m_next)
                m_use[t] = m_next
                seen[t] = None

    @pl.when(jnp.max(excess) > MLA_MAX_EXCESS)
    def _():
        for t in range(n_tiles):
            m_ref[...] = jnp.full(m_ref.shape, -jnp.inf, jnp.float32)
            acc_ref[t] = jnp.zeros(acc_ref.shape[1:], jnp.float32)

            def chunk(c, carry, t=t):
                s = jnp.dot(key_chunk(c), q_tile(t), preferred_element_type=jnp.float32)
                m_prev = m_ref[...]
                m_new = jnp.maximum(m_prev, jnp.max(s, axis=0, keepdims=True))
                p = jnp.exp2(s - m_new).astype(jnp.bfloat16)
                acc_ref[t] = jnp.exp2(m_prev - m_new) * acc_ref[t] + jnp.dot(
                    value_chunk(c), p, preferred_element_type=jnp.float32)
                m_ref[...] = m_new
                return carry

            lax.fori_loop(0, n_chunks, chunk, 0)

    for t in range(n_tiles):
        cols = slice(t * MLA_TQ, (t + 1) * MLA_TQ)
        o = acc_ref[t, 0:MLA_V, :] / acc_ref[t, MLA_V:MLA_V + 1, :]
        o_ref[0, :, cols] = o.astype(o_ref.dtype)


def _mla_attention(q_t, k, v_t):
    B, H, _, S = q_t.shape
    TQ = MLA_TQ
    step_q = MLA_STEP_TILES * TQ
    grid = (B, H, S // step_q)
    return pl.pallas_call(
        _mla_kernel,
        grid=grid,
        in_specs=[
            pl.BlockSpec((1, 1, QK_PAD, step_q), lambda b, h, i: (b, h, 0, i)),
            pl.BlockSpec((1, 1, S, QK_PAD), lambda b, h, i: (b, h, 0, 0)),
            pl.BlockSpec((1, 1, MLA_V_ROWS, S), lambda b, h, i: (b, h, 0, 0)),
        ],
        out_specs=pl.BlockSpec((1, MLA_V, step_q), lambda b, h, i: (b, h, i)),
        out_shape=jax.ShapeDtypeStruct((B, D_MLA, S), jnp.bfloat16),
        scratch_shapes=[pltpu.VMEM((1, TQ), jnp.float32),
                        pltpu.VMEM((MLA_STEP_TILES, MLA_V_ROWS, TQ), jnp.float32)],
        compiler_params=pltpu.CompilerParams(
            dimension_semantics=("arbitrary", "arbitrary", "arbitrary"),
            vmem_limit_bytes=VMEM_LIMIT_BYTES),
        name="mla",
    )(q_t, k, v_t)


def _bias_kernel(rel_ref, sink_ref, bucket_ref, o_ref):
    g = pl.program_id(0)
    for part in range(3):
        bucket = bucket_ref[part]
        for hh in range(SWA_GROUP):
            head = g * SWA_GROUP + hh
            tile = jnp.full((BLOCK, BLOCK), -jnp.inf, jnp.float32)
            for b in range(N_BUCKETS):
                tile = jnp.where(bucket == b, (rel_ref[b, head] - sink_ref[head]) * LOG2_E, tile)
            o_ref[0, part, :, hh * BLOCK:(hh + 1) * BLOCK] = tile


def _bias_table(rel_bias, sink, bucket):
    return pl.pallas_call(
        _bias_kernel,
        grid=(SWA_KV_HEADS,),
        in_specs=[pl.BlockSpec(memory_space=pltpu.SMEM),
                  pl.BlockSpec(memory_space=pltpu.SMEM),
                  pl.BlockSpec((3, BLOCK, BLOCK), lambda g: (0, 0, 0))],
        out_specs=pl.BlockSpec((1, 3, BLOCK, SWA_GROUP * BLOCK), lambda g: (g, 0, 0, 0)),
        out_shape=jax.ShapeDtypeStruct((SWA_KV_HEADS, 3, BLOCK, SWA_GROUP * BLOCK), jnp.float32),
        compiler_params=pltpu.CompilerParams(dimension_semantics=("arbitrary",)),
        name="bias",
    )(rel_bias, sink, bucket)


def _swa_kernel(qst_ref, ks_ref, vst_ref, bias_ref, o_ref):
    n_total = ks_ref.shape[2] // BLOCK
    n_local = o_ref.shape[2] // BLOCK
    step = pl.program_id(2)
    n_steps = pl.num_programs(2)
    neg_inf = jnp.float32(-jnp.inf)

    def band_starts(j):
        n = step * n_local + j
        return tuple(pl.multiple_of(b * BLOCK, BLOCK)
                     for b in (jnp.maximum(n - 1, 0), n, jnp.minimum(n + 1, n_total - 1)))

    def lane_block(j, width):
        start = j * width
        return pl.ds(start if isinstance(j, int) else pl.multiple_of(start, width), width)

    def band_scores(j):
        q_t = qst_ref[0, 0, :, lane_block(j, SWA_GROUP * BLOCK)]
        n = step * n_local + j
        penalties = [None, None, None]
        if not isinstance(j, int) or j == 0:
            penalties[0] = jnp.where(n > 0, 0.0, neg_inf)
        if not isinstance(j, int) or j == n_local - 1:
            penalties[2] = jnp.where(n < n_total - 1, 0.0, neg_inf)
        scores = []
        for part, start in enumerate(band_starts(j)):
            k = ks_ref[0, 0, pl.ds(start, BLOCK), :]
            s = jnp.dot(k, q_t, preferred_element_type=jnp.float32) + bias_ref[0, part]
            if penalties[part] is not None:
                s = s + penalties[part]
            scores.append(s)
        return scores

    def weighted_values(j, weights):
        acc = None
        for part, start in enumerate(band_starts(j)):
            v_t = vst_ref[0, 0, :, pl.ds(start, BLOCK)]
            pv = jnp.dot(v_t, weights[part].astype(jnp.bfloat16),
                         preferred_element_type=jnp.float32)
            acc = pv if acc is None else acc + pv
        return acc

    def write_block(j, acc, sink_weight):
        o = acc[0:SWA_HEAD_DIM] / (acc[SWA_HEAD_DIM:SWA_HEAD_DIM + 1] + sink_weight)
        cols = lane_block(j, BLOCK)
        for hh in range(SWA_GROUP):
            rows = slice(hh * SWA_HEAD_DIM, (hh + 1) * SWA_HEAD_DIM)
            o_ref[0, rows, cols] = o[:, hh * BLOCK:(hh + 1) * BLOCK].astype(o_ref.dtype)

    excess = None
    pending = [band_scores(j) for j in range(SWA_AHEAD)]
    for j in range(n_local):
        if j + SWA_AHEAD < n_local:
            pending.append(band_scores(j + SWA_AHEAD))
        scores = pending.pop(0)
        for s in scores:
            smax = jnp.max(s.reshape(BLOCK // SUBLANES, SUBLANES, s.shape[1]), axis=0)
            excess = smax if excess is None else jnp.maximum(excess, smax)
        write_block(j, weighted_values(j, [jnp.exp2(s) for s in scores]), 1.0)

    @pl.when(jnp.max(excess) > SWA_MAX_EXCESS)
    def _():
        def exact_block(j, carry):
            scores = band_scores(j)
            m = jnp.zeros((1, scores[0].shape[1]), jnp.float32)
            for s in scores:
                m = jnp.maximum(m, jnp.max(s, axis=0, keepdims=True))
            write_block(j, weighted_values(j, [jnp.exp2(s - m) for s in scores]), jnp.exp2(-m))
            return carry

        lax.fori_loop(0, n_local, exact_block, 0)


def _swa_attention(qs_t, ks, vs_t, bias_t):
    B, G, _, S = vs_t.shape
    T = SWA_TOKENS
    grid = (B, G, S // T)
    rows = SWA_GROUP * SWA_HEAD_DIM
    return pl.pallas_call(
        _swa_kernel,
        grid=grid,
        in_specs=[
            pl.BlockSpec((1, 1, SWA_HEAD_DIM, SWA_GROUP * T), lambda b, g, t: (b, g, 0, t)),
            pl.BlockSpec((1, 1, S, SWA_HEAD_DIM), lambda b, g, t: (b, g, 0, 0)),
            pl.BlockSpec((1, 1, SWA_V_ROWS, S), lambda b, g, t: (b, g, 0, 0)),
            pl.BlockSpec((1, 3, BLOCK, SWA_GROUP * BLOCK), lambda b, g, t: (g, 0, 0, 0)),
        ],
        out_specs=pl.BlockSpec((1, rows, T), lambda b, g, t: (b, g, t)),
        out_shape=jax.ShapeDtypeStruct((B, D_SWA, S), jnp.bfloat16),
        compiler_params=pltpu.CompilerParams(
            dimension_semantics=("arbitrary", "arbitrary", "arbitrary"),
            vmem_limit_bytes=VMEM_LIMIT_BYTES),
        name="swa",
    )(qs_t, ks, vs_t, bias_t)


def _out_kernel(x_ref, oa_ref, ob_ref, wga_ref, wgb_ref, wa_ref, wb_ref, g_ref, b_ref, y_ref):
    n_chunks = x_ref.shape[1] // OUT_ROWS

    def gate_paths(c):
        xb = x_ref[0, c * OUT_ROWS:(c + 1) * OUT_ROWS, :].astype(jnp.bfloat16)
        return [lax.dot_general(wg_ref[...], xb, _NT, preferred_element_type=jnp.float32)
                for wg_ref in (wga_ref, wgb_ref)]

    def project(c, gates):
        toks = slice(c * OUT_ROWS, (c + 1) * OUT_ROWS)
        out = None
        for o_ref_, gate, w_ref in ((oa_ref, gates[0], wa_ref), (ob_ref, gates[1], wb_ref)):
            mixed = (o_ref_[0, :, toks].astype(jnp.float32) * jax.nn.silu(gate)).astype(jnp.bfloat16)
            part = lax.dot_general(mixed, w_ref[...], _TN, preferred_element_type=jnp.float32)
            out = part if out is None else out + part
        return out

    def residual_norm(c, out):
        rows = slice(c * OUT_ROWS, (c + 1) * OUT_ROWS)
        h = ALPHA * x_ref[0, rows, :] + out
        mu = jnp.mean(h, axis=-1, keepdims=True)
        var = jnp.mean(jnp.square(h - mu), axis=-1, keepdims=True)
        y_ref[0, rows, :] = ((h - mu) * lax.rsqrt(var + 1e-5)) * g_ref[...] + b_ref[...]

    gates_next = gate_paths(0)
    out_prev = None
    for c in range(n_chunks):
        gates = gates_next
        if c + 1 < n_chunks:
            gates_next = gate_paths(c + 1)
        out = project(c, gates)
        if out_prev is not None:
            residual_norm(c - 1, out_prev)
        out_prev = out
    residual_norm(n_chunks - 1, out_prev)


def _out_project(x, o_a_t, o_b_t, w_gate_a_t, w_gate_b_t, w_a, w_b, ln_g, ln_b):
    B, S, D = x.shape
    T = OUT_TOKENS
    grid = (B, S // T)
    const = lambda *shape: pl.BlockSpec(shape, lambda b, t: (0,) * len(shape))
    return pl.pallas_call(
        _out_kernel,
        grid=grid,
        in_specs=[
            pl.BlockSpec((1, T, D), lambda b, t: (b, t, 0)),
            pl.BlockSpec((1, D_MLA, T), lambda b, t: (b, 0, t)),
            pl.BlockSpec((1, D_SWA, T), lambda b, t: (b, 0, t)),
            const(D_MLA, D),
            const(D_SWA, D),
            const(D_MLA, D),
            const(D_SWA, D),
            const(1, D),
            const(1, D),
        ],
        out_specs=pl.BlockSpec((1, T, D), lambda b, t: (b, t, 0)),
        out_shape=jax.ShapeDtypeStruct((B, S, D), x.dtype),
        compiler_params=pltpu.CompilerParams(
            dimension_semantics=("arbitrary", "arbitrary"),
            vmem_limit_bytes=VMEM_LIMIT_BYTES),
        name="outproj",
    )(x, o_a_t, o_b_t, w_gate_a_t, w_gate_b_t, w_a, w_b, ln_g, ln_b)


def _t5_bucket(rel):
    half = N_BUCKETS // 2
    ret = np.where(rel > 0, half, 0)
    n = np.abs(rel)
    max_exact = half // 2
    large = max_exact + (np.log(np.maximum(n, 1).astype(np.float32) / max_exact)
                         / np.log(MAX_DISTANCE / max_exact) * (half - max_exact)).astype(np.int32)
    large = np.minimum(large, half - 1)
    return (ret + np.where(n < max_exact, n, large)).astype(np.int32)


def _band_geometry():
    q_loc = np.arange(BLOCK)
    k_loc = np.arange(3 * BLOCK) - BLOCK
    rel = k_loc[:, None] - q_loc[None, :]
    return _t5_bucket(rel), np.abs(rel) <= WINDOW


def _col(w_in, i):
    return w_in[:, IN_OFFSETS[i]:IN_OFFSETS[i + 1]]


def kernel(x, w_in, g_q, g_kv, w_uq, w_ukv, sink, rel_bias, w_out, ln_g, ln_b):
    B, S, _ = x.shape
    bf = jnp.bfloat16
    f32 = jnp.float32

    pos = jnp.arange(S, dtype=f32)
    inv_freq = ROPE_BASE ** (-jnp.arange(0, MLA_ROPE, 2, dtype=f32) / MLA_ROPE)
    ang_t = inv_freq[:, None] * pos[None, :]
    cos_t, sin_t = jnp.cos(ang_t), jnp.sin(ang_t)

    c_q, c_kv, k_rope, gate_a, q_s, k_s, v_s, gate_b = (_col(w_in, i) for i in range(8))
    w_tr = jnp.concatenate([c_q, q_s, v_s, k_rope], axis=1).T.astype(bf)
    w_nat = jnp.concatenate([c_kv, k_s], axis=1).astype(bf)
    w_ukv3 = w_ukv.reshape(KV_LORA, MLA_HEADS, MLA_NOPE + MLA_V)
    w_uk = w_ukv3[:, :, :MLA_NOPE].reshape(KV_LORA, MLA_HEADS * MLA_NOPE).astype(bf)
    w_uv_t = w_ukv3[:, :, MLA_NOPE:].reshape(KV_LORA, D_MLA).T.astype(bf)
    w_uq_t = w_uq.T.astype(bf)

    q_t, k, v_t, qs_t, ks, vs_t = _project(
        x, w_tr, w_nat, g_q.reshape(Q_LORA, 1), g_kv.reshape(1, KV_LORA),
        w_uq_t, w_uk, w_uv_t, cos_t, sin_t)

    o_a_t = _mla_attention(q_t, k, v_t)

    bucket, band = _band_geometry()
    bucket = jnp.asarray(np.where(band, bucket, -1).reshape(3, BLOCK, BLOCK), jnp.int32)
    bias_t = _bias_table(rel_bias.astype(f32), sink.astype(f32), bucket)

    o_b_t = _swa_attention(qs_t, ks, vs_t, bias_t)

    w_out_bf = w_out.astype(bf)
    return _out_project(x, o_a_t, o_b_t, gate_a.T.astype(bf), gate_b.T.astype(bf),
                        w_out_bf[:D_MLA], w_out_bf[D_MLA:],
                        ln_g.reshape(1, D_MODEL), ln_b.reshape(1, D_MODEL))
```

```python
import functools
import math

import jax
import jax.numpy as jnp
import numpy as np
from jax import lax
from jax.experimental import pallas as pl
from jax.experimental.pallas import tpu as pltpu

D_MODEL = 1024
MLA_HEADS = 8
MLA_NOPE = 64
MLA_ROPE = 32
MLA_V = 64
MLA_QK = MLA_NOPE + MLA_ROPE
Q_LORA = 256
KV_LORA = 128
D_MLA = MLA_HEADS * MLA_V
MLA_SCALE = 1.0 / math.sqrt(MLA_QK)
LOG2_E = math.log2(math.e)
MLA_Q_SCALE = MLA_SCALE * LOG2_E
BF16_SUBLANES = 16
MLA_V_ROWS = MLA_V + BF16_SUBLANES
ROPE_BASE = 10000.0

SWA_HEADS = 8
SWA_KV_HEADS = 2
SWA_HEAD_DIM = 64
SWA_GROUP = SWA_HEADS // SWA_KV_HEADS
D_SWA = SWA_HEADS * SWA_HEAD_DIM
WINDOW = 128
BLOCK = 128
SWA_SCALE = 1.0 / math.sqrt(SWA_HEAD_DIM)
SWA_Q_SCALE = SWA_SCALE * LOG2_E
SWA_V_ROWS = SWA_HEAD_DIM + BF16_SUBLANES
N_BUCKETS = 32
MAX_DISTANCE = 128

DEPTH = 1
ALPHA = (2.0 * DEPTH) ** 0.25

IN_SPLITS = (Q_LORA, KV_LORA, MLA_ROPE, D_MLA, D_SWA,
             SWA_KV_HEADS * SWA_HEAD_DIM, SWA_KV_HEADS * SWA_HEAD_DIM, D_SWA)
IN_OFFSETS = tuple(int(o) for o in np.cumsum((0,) + IN_SPLITS))

LANE = 128
SUBLANES = 8
QK_PAD = LANE
VMEM_LIMIT_BYTES = 56 * 1024 * 1024

PROJ_TOKENS = 1024
PROJ_PARTS = 4
MLA_TQ = 512
MLA_STEP_TILES = 4
MLA_TK = 256
MLA_AHEAD = 2
MLA_REFRESH = 8
MLA_MAX_EXCESS = 64.0
SWA_TOKENS = 2048
SWA_AHEAD = 1
SWA_MAX_EXCESS = 64.0
OUT_TOKENS = 1024
OUT_ROWS = 256

TR_CQ = 0
TR_QS = TR_CQ + Q_LORA
TR_VS = TR_QS + D_SWA
TR_KR = TR_VS + SWA_KV_HEADS * SWA_HEAD_DIM
TR_ROWS = TR_KR + MLA_ROPE
NAT_CKV = 0
NAT_KS = NAT_CKV + KV_LORA
NAT_COLS = NAT_KS + SWA_KV_HEADS * SWA_HEAD_DIM

_NT = (((1,), (1,)), ((), ()))
_TN = (((0,), (0,)), ((), ()))


def _rsqrt_mean_sq(x, axis, eps):
    return lax.rsqrt(jnp.mean(x * x, axis=axis, keepdims=True) + eps)


def _proj_kernel(x_ref, wtr_ref, wnat_ref, gq_ref, gkv_ref, wuqt_ref, wuk_ref, wuvt_ref,
                 cos_t_ref, sin_t_ref,
                 qt_ref, k_ref, vt_ref, qst_ref, ks_ref, vst_ref,
                 tr_ref, nat_ref):
    n_tok = x_ref.shape[1] // PROJ_PARTS
    half = MLA_ROPE // 2

    def project(p):
        tok = slice(p * n_tok, (p + 1) * n_tok)
        xb = x_ref[0, tok, :].astype(jnp.bfloat16)
        tr_ref[:, tok] = lax.dot_general(wtr_ref[...], xb, _NT, preferred_element_type=jnp.float32)
        nat_ref[tok, :] = jnp.dot(xb, wnat_ref[...], preferred_element_type=jnp.float32)

    project(0)
    for p in range(PROJ_PARTS):
        if p + 1 < PROJ_PARTS:
            project(p + 1)
        _proj_finish(p, n_tok, tr_ref, nat_ref, gq_ref, gkv_ref, wuqt_ref, wuk_ref, wuvt_ref,
                     cos_t_ref, sin_t_ref,
                     qt_ref, k_ref, vt_ref, qst_ref, ks_ref, vst_ref)


def _proj_finish(p, n_tok, tr_ref, nat_ref, gq_ref, gkv_ref, wuqt_ref, wuk_ref, wuvt_ref,
                 cos_t_ref, sin_t_ref,
                 qt_ref, k_ref, vt_ref, qst_ref, ks_ref, vst_ref):
    tok = slice(p * n_tok, (p + 1) * n_tok)
    half = MLA_ROPE // 2

    cq = tr_ref[TR_CQ:TR_CQ + Q_LORA, tok]
    cqn = (cq * _rsqrt_mean_sq(cq, 0, 1e-6)) * gq_ref[...]
    q_t = jnp.dot(wuqt_ref[...], cqn.astype(jnp.bfloat16),
                  preferred_element_type=jnp.float32)
    cos_t = cos_t_ref[:, tok]
    sin_t = sin_t_ref[:, tok]
    zeros_pad = jnp.zeros((QK_PAD - MLA_QK, n_tok), jnp.bfloat16)
    for h in range(MLA_HEADS):
        base = h * MLA_QK
        nope = q_t[base:base + MLA_NOPE]
        r1 = q_t[base + MLA_NOPE:base + MLA_NOPE + half]
        r2 = q_t[base + MLA_NOPE + half:base + MLA_QK]
        qt_ref[0, h, 0:MLA_NOPE, tok] = (nope * MLA_Q_SCALE).astype(jnp.bfloat16)
        qt_ref[0, h, MLA_NOPE:MLA_NOPE + half, tok] = (
            (r1 * cos_t - r2 * sin_t) * MLA_Q_SCALE).astype(jnp.bfloat16)
        qt_ref[0, h, MLA_NOPE + half:MLA_QK, tok] = (
            (r2 * cos_t + r1 * sin_t) * MLA_Q_SCALE).astype(jnp.bfloat16)
        qt_ref[0, h, MLA_QK:QK_PAD, tok] = zeros_pad

    ckv = nat_ref[tok, NAT_CKV:NAT_CKV + KV_LORA]
    kvn = ((ckv * _rsqrt_mean_sq(ckv, 1, 1e-6)) * gkv_ref[...]).astype(jnp.bfloat16)
    k_nope = jnp.dot(kvn, wuk_ref[...], preferred_element_type=jnp.float32)
    v_t = lax.dot_general(wuvt_ref[...], kvn, _NT, preferred_element_type=jnp.float32)
    kr1 = tr_ref[TR_KR:TR_KR + half, tok]
    kr2 = tr_ref[TR_KR + half:TR_KR + MLA_ROPE, tok]
    k_rope = jnp.concatenate(
        [jnp.zeros((MLA_NOPE, n_tok), jnp.float32),
         kr1 * cos_t - kr2 * sin_t,
         kr2 * cos_t + kr1 * sin_t,
         jnp.zeros((QK_PAD - MLA_QK, n_tok), jnp.float32)], axis=0).T
    lane = lax.broadcasted_iota(jnp.int32, k_rope.shape, 1)
    row_id = lax.broadcasted_iota(jnp.int32, (BF16_SUBLANES, n_tok), 0)
    ones_row = jnp.where(row_id == 0, 1.0, 0.0).astype(jnp.bfloat16)
    for h in range(MLA_HEADS):
        pair = k_nope[:, (h // 2) * LANE:(h // 2 + 1) * LANE]
        if h % 2:
            pair = pltpu.roll(pair, MLA_NOPE, axis=1)
        k_ref[0, h, tok, :] = jnp.where(lane < MLA_NOPE, pair, k_rope).astype(jnp.bfloat16)
        vt_ref[0, h, 0:MLA_V, tok] = v_t[h * MLA_V:(h + 1) * MLA_V].astype(jnp.bfloat16)
        vt_ref[0, h, MLA_V:MLA_V_ROWS, tok] = ones_row

    n_blocks = n_tok // BLOCK
    for g in range(SWA_KV_HEADS):
        ks_ref[0, g, tok, :] = nat_ref[tok, NAT_KS + g * SWA_HEAD_DIM:
                                       NAT_KS + (g + 1) * SWA_HEAD_DIM].astype(jnp.bfloat16)
        vst_ref[0, g, 0:SWA_HEAD_DIM, tok] = tr_ref[TR_VS + g * SWA_HEAD_DIM:
                                                    TR_VS + (g + 1) * SWA_HEAD_DIM, tok].astype(jnp.bfloat16)
        vst_ref[0, g, SWA_HEAD_DIM:SWA_V_ROWS, tok] = ones_row
        for hh in range(SWA_GROUP):
            row = TR_QS + (g * SWA_GROUP + hh) * SWA_HEAD_DIM
            q_h = (tr_ref[row:row + SWA_HEAD_DIM, tok] * SWA_Q_SCALE).astype(jnp.bfloat16)
            for nb in range(n_blocks):
                col = ((p * n_blocks + nb) * SWA_GROUP + hh) * BLOCK
                qst_ref[0, g, :, col:col + BLOCK] = q_h[:, nb * BLOCK:(nb + 1) * BLOCK]


def _project(x, w_tr, w_nat, g_q, g_kv, w_uq_t, w_uk, w_uv_t, cos_t, sin_t):
    B, S, _ = x.shape
    T = PROJ_TOKENS
    grid = (B, S // T)
    const = lambda *shape: pl.BlockSpec(shape, lambda b, t: (0,) * len(shape))
    bf = jnp.bfloat16
    out_shape = (
        jax.ShapeDtypeStruct((B, MLA_HEADS, QK_PAD, S), bf),
        jax.ShapeDtypeStruct((B, MLA_HEADS, S, QK_PAD), bf),
        jax.ShapeDtypeStruct((B, MLA_HEADS, MLA_V_ROWS, S), bf),
        jax.ShapeDtypeStruct((B, SWA_KV_HEADS, SWA_HEAD_DIM, SWA_GROUP * S), bf),
        jax.ShapeDtypeStruct((B, SWA_KV_HEADS, S, SWA_HEAD_DIM), bf),
        jax.ShapeDtypeStruct((B, SWA_KV_HEADS, SWA_V_ROWS, S), bf),
    )
    out_specs = (
        pl.BlockSpec((1, MLA_HEADS, QK_PAD, T), lambda b, t: (b, 0, 0, t)),
        pl.BlockSpec((1, MLA_HEADS, T, QK_PAD), lambda b, t: (b, 0, t, 0)),
        pl.BlockSpec((1, MLA_HEADS, MLA_V_ROWS, T), lambda b, t: (b, 0, 0, t)),
        pl.BlockSpec((1, SWA_KV_HEADS, SWA_HEAD_DIM, SWA_GROUP * T), lambda b, t: (b, 0, 0, t)),
        pl.BlockSpec((1, SWA_KV_HEADS, T, SWA_HEAD_DIM), lambda b, t: (b, 0, t, 0)),
        pl.BlockSpec((1, SWA_KV_HEADS, SWA_V_ROWS, T), lambda b, t: (b, 0, 0, t)),
    )
    in_specs = [
        pl.BlockSpec((1, T, D_MODEL), lambda b, t: (b, t, 0)),
        const(TR_ROWS, D_MODEL),
        const(D_MODEL, NAT_COLS),
        const(Q_LORA, 1),
        const(1, KV_LORA),
        const(MLA_HEADS * MLA_QK, Q_LORA),
        const(KV_LORA, MLA_HEADS * MLA_NOPE),
        const(D_MLA, KV_LORA),
        pl.BlockSpec((MLA_ROPE // 2, T), lambda b, t: (0, t)),
        pl.BlockSpec((MLA_ROPE // 2, T), lambda b, t: (0, t)),
    ]
    return pl.pallas_call(
        _proj_kernel,
        grid=grid,
        in_specs=in_specs,
        out_specs=out_specs,
        out_shape=out_shape,
        scratch_shapes=[pltpu.VMEM((TR_ROWS, T), jnp.float32),
                        pltpu.VMEM((T, NAT_COLS), jnp.float32)],
        compiler_params=pltpu.CompilerParams(
            dimension_semantics=("arbitrary", "arbitrary"),
            vmem_limit_bytes=VMEM_LIMIT_BYTES),
        name="proj",
    )(x, w_tr, w_nat, g_q, g_kv, w_uq_t, w_uk, w_uv_t, cos_t, sin_t)


def _mla_kernel(qt_ref, k_ref, vt_ref, o_ref, m_ref, acc_ref):
    n_chunks = k_ref.shape[2] // MLA_TK
    n_tiles = qt_ref.shape[3] // MLA_TQ

    def q_tile(t):
        return qt_ref[0, 0, :, t * MLA_TQ:(t + 1) * MLA_TQ]

    def chunk_slice(c):
        start = c * MLA_TK
        return pl.ds(start if isinstance(c, int) else pl.multiple_of(start, MLA_TK), MLA_TK)

    def key_chunk(c):
        return k_ref[0, 0, chunk_slice(c), :]

    def value_chunk(c):
        return vt_ref[0, 0, :, chunk_slice(c)]

    items = [(t, c) for t in range(n_tiles) for c in range(n_chunks)]
    m_use = [None] * n_tiles
    seen = [None] * n_tiles
    acc = [None] * n_tiles
    excess = None

    def issue_scores(i):
        t, c = items[i]
        if c == 0:
            s0 = jnp.dot(k_ref[0, 0, 0:BF16_SUBLANES, :], q_tile(t),
                         preferred_element_type=jnp.float32)
            m_use[t] = jnp.max(s0, axis=0, keepdims=True)
        return jnp.dot(key_chunk(c), q_tile(t), preferred_element_type=jnp.float32)

    pending = [issue_scores(i) for i in range(MLA_AHEAD)]
    for i, (t, c) in enumerate(items):
        if i + MLA_AHEAD < len(items):
            pending.append(issue_scores(i + MLA_AHEAD))
        s = pending.pop(0)
        p = jnp.exp2(s - m_use[t]).astype(jnp.bfloat16)
        pv = jnp.dot(value_chunk(c), p, preferred_element_type=jnp.float32)
        acc[t] = pv if c == 0 else acc[t] + pv
        smax = jnp.max(s.reshape(MLA_TK // SUBLANES, SUBLANES, MLA_TQ), axis=0)
        seen[t] = smax if seen[t] is None else jnp.maximum(seen[t], smax)
        last = c + 1 == n_chunks
        if last or c == 0 or (c + 1) % MLA_REFRESH == 0:
            over = seen[t] - m_use[t]
            excess = over if excess is None else jnp.maximum(excess, over)
            if last:
                acc_ref[t] = acc[t]
            else:
                m_next = jnp.maximum(m_use[t], jnp.max(seen[t], axis=0, keepdims=True))
                acc[t] = acc[t] * jnp.exp2(m_use[t] - m_next)
                m_use[t] = m_next
                seen[t] = None

    @pl.when(jnp.max(excess) > MLA_MAX_EXCESS)
    def _():
        for t in range(n_tiles):
            m_ref[...] = jnp.full(m_ref.shape, -jnp.inf, jnp.float32)
            acc_ref[t] = jnp.zeros(acc_ref.shape[1:], jnp.float32)

            def chunk(c, carry, t=t):
                s = jnp.dot(key_chunk(c), q_tile(t), preferred_element_type=jnp.float32)
                m_prev = m_ref[...]
                m_new = jnp.maximum(m_prev, jnp.max(s, axis=0, keepdims=True))
                p = jnp.exp2(s - m_new).astype(jnp.bfloat16)
                acc_ref[t] = jnp.exp2(m_prev - m_new) * acc_ref[t] + jnp.dot(
                    value_chunk(c), p, preferred_element_type=jnp.float32)
                m_ref[...] = m_new
                return carry

            lax.fori_loop(0, n_chunks, chunk, 0)

    for t in range(n_tiles):
        cols = slice(t * MLA_TQ, (t + 1) * MLA_TQ)
        o = acc_ref[t, 0:MLA_V, :] / acc_ref[t, MLA_V:MLA_V + 1, :]
        o_ref[0, :, cols] = o.astype(o_ref.dtype)


def _mla_attention(q_t, k, v_t):
    B, H, _, S = q_t.shape
    TQ = MLA_TQ
    step_q = MLA_STEP_TILES * TQ
    grid = (B, H, S // step_q)
    return pl.pallas_call(
        _mla_kernel,
        grid=grid,
        in_specs=[
            pl.BlockSpec((1, 1, QK_PAD, step_q), lambda b, h, i: (b, h, 0, i)),
            pl.BlockSpec((1, 1, S, QK_PAD), lambda b, h, i: (b, h, 0, 0)),
            pl.BlockSpec((1, 1, MLA_V_ROWS, S), lambda b, h, i: (b, h, 0, 0)),
        ],
        out_specs=pl.BlockSpec((1, MLA_V, step_q), lambda b, h, i: (b, h, i)),
        out_shape=jax.ShapeDtypeStruct((B, D_MLA, S), jnp.bfloat16),
        scratch_shapes=[pltpu.VMEM((1, TQ), jnp.float32),
                        pltpu.VMEM((MLA_STEP_TILES, MLA_V_ROWS, TQ), jnp.float32)],
        compiler_params=pltpu.CompilerParams(
            dimension_semantics=("arbitrary", "arbitrary", "arbitrary"),
            vmem_limit_bytes=VMEM_LIMIT_BYTES),
        name="mla",
    )(q_t, k, v_t)


def _bias_kernel(rel_ref, sink_ref, bucket_ref, o_ref):
    g = pl.program_id(0)
    for part in range(3):
        bucket = bucket_ref[part]
        for hh in range(SWA_GROUP):
            head = g * SWA_GROUP + hh
            tile = jnp.full((BLOCK, BLOCK), -jnp.inf, jnp.float32)
            for b in range(N_BUCKETS):
                tile = jnp.where(bucket == b, (rel_ref[b, head] - sink_ref[head]) * LOG2_E, tile)
            o_ref[0, part, :, hh * BLOCK:(hh + 1) * BLOCK] = tile


def _bias_table(rel_bias, sink, bucket):
    return pl.pallas_call(
        _bias_kernel,
        grid=(SWA_KV_HEADS,),
        in_specs=[pl.BlockSpec(memory_space=pltpu.SMEM),
                  pl.BlockSpec(memory_space=pltpu.SMEM),
                  pl.BlockSpec((3, BLOCK, BLOCK), lambda g: (0, 0, 0))],
        out_specs=pl.BlockSpec((1, 3, BLOCK, SWA_GROUP * BLOCK), lambda g: (g, 0, 0, 0)),
        out_shape=jax.ShapeDtypeStruct((SWA_KV_HEADS, 3, BLOCK, SWA_GROUP * BLOCK), jnp.float32),
        compiler_params=pltpu.CompilerParams(dimension_semantics=("arbitrary",)),
        name="bias",
    )(rel_bias, sink, bucket)


def _swa_kernel(qst_ref, ks_ref, vst_ref, bias_ref, o_ref):
    n_total = ks_ref.shape[2] // BLOCK
    n_local = o_ref.shape[2] // BLOCK
    step = pl.program_id(2)
    n_steps = pl.num_programs(2)
    neg_inf = jnp.float32(-jnp.inf)

    def band_starts(j):
        n = step * n_local + j
        return tuple(pl.multiple_of(b * BLOCK, BLOCK)
                     for b in (jnp.maximum(n - 1, 0), n, jnp.minimum(n + 1, n_total - 1)))

    def lane_block(j, width):
        start = j * width
        return pl.ds(start if isinstance(j, int) else pl.multiple_of(start, width), width)

    def band_scores(j):
        q_t = qst_ref[0, 0, :, lane_block(j, SWA_GROUP * BLOCK)]
        n = step * n_local + j
        penalties = [None, None, None]
        if not isinstance(j, int) or j == 0:
            penalties[0] = jnp.where(n > 0, 0.0, neg_inf)
        if not isinstance(j, int) or j == n_local - 1:
            penalties[2] = jnp.where(n < n_total - 1, 0.0, neg_inf)
        scores = []
        for part, start in enumerate(band_starts(j)):
            k = ks_ref[0, 0, pl.ds(start, BLOCK), :]
            s = jnp.dot(k, q_t, preferred_element_type=jnp.float32) + bias_ref[0, part]
            if penalties[part] is not None:
                s = s + penalties[part]
            scores.append(s)
        return scores

    def weighted_values(j, weights):
        acc = None
        for part, start in enumerate(band_starts(j)):
            v_t = vst_ref[0, 0, :, pl.ds(start, BLOCK)]
            pv = jnp.dot(v_t, weights[part].astype(jnp.bfloat16),
                         preferred_element_type=jnp.float32)
            acc = pv if acc is None else acc + pv
        return acc

    def write_block(j, acc, sink_weight):
        o = acc[0:SWA_HEAD_DIM] / (acc[SWA_HEAD_DIM:SWA_HEAD_DIM + 1] + sink_weight)
        cols = lane_block(j, BLOCK)
        for hh in range(SWA_GROUP):
            rows = slice(hh * SWA_HEAD_DIM, (hh + 1) * SWA_HEAD_DIM)
            o_ref[0, rows, cols] = o[:, hh * BLOCK:(hh + 1) * BLOCK].astype(o_ref.dtype)

    excess = None
    pending = [band_scores(j) for j in range(SWA_AHEAD)]
    for j in range(n_local):
        if j + SWA_AHEAD < n_local:
            pending.append(band_scores(j + SWA_AHEAD))
        scores = pending.pop(0)
        for s in scores:
            smax = jnp.max(s.reshape(BLOCK // SUBLANES, SUBLANES, s.shape[1]), axis=0)
            excess = smax if excess is None else jnp.maximum(excess, smax)
        write_block(j, weighted_values(j, [jnp.exp2(s) for s in scores]), 1.0)

    @pl.when(jnp.max(excess) > SWA_MAX_EXCESS)
    def _():
        def exact_block(j, carry):
            scores = band_scores(j)
            m = jnp.zeros((1, scores[0].shape[1]), jnp.float32)
            for s in scores:
                m = jnp.maximum(m, jnp.max(s, axis=0, keepdims=True))
            write_block(j, weighted_values(j, [jnp.exp2(s - m) for s in scores]), jnp.exp2(-m))
            return carry

        lax.fori_loop(0, n_local, exact_block, 0)


def _swa_attention(qs_t, ks, vs_t, bias_t):
    B, G, _, S = vs_t.shape
    T = SWA_TOKENS
    grid = (B, G, S // T)
    rows = SWA_GROUP * SWA_HEAD_DIM
    return pl.pallas_call(
        _swa_kernel,
        grid=grid,
        in_specs=[
            pl.BlockSpec((1, 1, SWA_HEAD_DIM, SWA_GROUP * T), lambda b, g, t: (b, g, 0, t)),
            pl.BlockSpec((1, 1, S, SWA_HEAD_DIM), lambda b, g, t: (b, g, 0, 0)),
            pl.BlockSpec((1, 1, SWA_V_ROWS, S), lambda b, g, t: (b, g, 0, 0)),
            pl.BlockSpec((1, 3, BLOCK, SWA_GROUP * BLOCK), lambda b, g, t: (g, 0, 0, 0)),
        ],
        out_specs=pl.BlockSpec((1, rows, T), lambda b, g, t: (b, g, t)),
        out_shape=jax.ShapeDtypeStruct((B, D_SWA, S), jnp.bfloat16),
        compiler_params=pltpu.CompilerParams(
            dimension_semantics=("arbitrary", "arbitrary", "arbitrary"),
            vmem_limit_bytes=VMEM_LIMIT_BYTES),
        name="swa",
    )(qs_t, ks, vs_t, bias_t)


def _out_kernel(x_ref, oa_ref, ob_ref, wga_ref, wgb_ref, wa_ref, wb_ref, g_ref, b_ref, y_ref):
    n_chunks = x_ref.shape[1] // OUT_ROWS

    def gate_paths(c):
        xb = x_ref[0, c * OUT_ROWS:(c + 1) * OUT_ROWS, :].astype(jnp.bfloat16)
        return [jnp.dot(xb, wg_ref[...], preferred_element_type=jnp.float32)
                for wg_ref in (wga_ref, wgb_ref)]

    def project(c, gates):
        toks = slice(c * OUT_ROWS, (c + 1) * OUT_ROWS)
        out = None
        for o_ref_, gate, w_ref in ((oa_ref, gates[0], wa_ref), (ob_ref, gates[1], wb_ref)):
            o_nat = o_ref_[0, :, toks].astype(jnp.float32).T
            mixed = (o_nat * jax.nn.silu(gate)).astype(jnp.bfloat16)
            part = jnp.dot(mixed, w_ref[...], preferred_element_type=jnp.float32)
            out = part if out is None else out + part
        return out

    def residual_norm(c, out):
        rows = slice(c * OUT_ROWS, (c + 1) * OUT_ROWS)
        h = ALPHA * x_ref[0, rows, :] + out
        mu = jnp.mean(h, axis=-1, keepdims=True)
        var = jnp.mean(jnp.square(h - mu), axis=-1, keepdims=True)
        y_ref[0, rows, :] = ((h - mu) * lax.rsqrt(var + 1e-5)) * g_ref[...] + b_ref[...]

    gates_next = gate_paths(0)
    out_prev = None
    for c in range(n_chunks):
        gates = gates_next
        if c + 1 < n_chunks:
            gates_next = gate_paths(c + 1)
        out = project(c, gates)
        if out_prev is not None:
            residual_norm(c - 1, out_prev)
        out_prev = out
    residual_norm(n_chunks - 1, out_prev)


def _out_project(x, o_a_t, o_b_t, w_gate_a, w_gate_b, w_a, w_b, ln_g, ln_b):
    B, S, D = x.shape
    T = OUT_TOKENS
    grid = (B, S // T)
    const = lambda *shape: pl.BlockSpec(shape, lambda b, t: (0,) * len(shape))
    return pl.pallas_call(
        _out_kernel,
        grid=grid,
        in_specs=[
            pl.BlockSpec((1, T, D), lambda b, t: (b, t, 0)),
            pl.BlockSpec((1, D_MLA, T), lambda b, t: (b, 0, t)),
            pl.BlockSpec((1, D_SWA, T), lambda b, t: (b, 0, t)),
            const(D, D_MLA),
            const(D, D_SWA),
            const(D_MLA, D),
            const(D_SWA, D),
            const(1, D),
            const(1, D),
        ],
        out_specs=pl.BlockSpec((1, T, D), lambda b, t: (b, t, 0)),
        out_shape=jax.ShapeDtypeStruct((B, S, D), x.dtype),
        compiler_params=pltpu.CompilerParams(
            dimension_semantics=("arbitrary", "arbitrary"),
            vmem_limit_bytes=VMEM_LIMIT_BYTES),
        name="outproj",
    )(x, o_a_t, o_b_t, w_gate_a, w_gate_b, w_a, w_b, ln_g, ln_b)


def _t5_bucket(rel):
    half = N_BUCKETS // 2
    ret = np.where(rel > 0, half, 0)
    n = np.abs(rel)
    max_exact = half // 2
    large = max_exact + (np.log(np.maximum(n, 1).astype(np.float32) / max_exact)
                         / np.log(MAX_DISTANCE / max_exact) * (half - max_exact)).astype(np.int32)
    large = np.minimum(large, half - 1)
    return (ret + np.where(n < max_exact, n, large)).astype(np.int32)


def _band_geometry():
    q_loc = np.arange(BLOCK)
    k_loc = np.arange(3 * BLOCK) - BLOCK
    rel = k_loc[:, None] - q_loc[None, :]
    return _t5_bucket(rel), np.abs(rel) <= WINDOW


def _col(w_in, i):
    return w_in[:, IN_OFFSETS[i]:IN_OFFSETS[i + 1]]


def kernel(x, w_in, g_q, g_kv, w_uq, w_ukv, sink, rel_bias, w_out, ln_g, ln_b):
    B, S, _ = x.shape
    bf = jnp.bfloat16
    f32 = jnp.float32

    pos = jnp.arange(S, dtype=f32)
    inv_freq = ROPE_BASE ** (-jnp.arange(0, MLA_ROPE, 2, dtype=f32) / MLA_ROPE)
    ang_t = inv_freq[:, None] * pos[None, :]
    cos_t, sin_t = jnp.cos(ang_t), jnp.sin(ang_t)

    c_q, c_kv, k_rope, gate_a, q_s, k_s, v_s, gate_b = (_col(w_in, i) for i in range(8))
    w_tr = jnp.concatenate([c_q, q_s, v_s, k_rope], axis=1).T.astype(bf)
    w_nat = jnp.concatenate([c_kv, k_s], axis=1).astype(bf)
    w_ukv3 = w_ukv.reshape(KV_LORA, MLA_HEADS, MLA_NOPE + MLA_V)
    w_uk = w_ukv3[:, :, :MLA_NOPE].reshape(KV_LORA, MLA_HEADS * MLA_NOPE).astype(bf)
    w_uv_t = w_ukv3[:, :, MLA_NOPE:].reshape(KV_LORA, D_MLA).T.astype(bf)
    w_uq_t = w_uq.T.astype(bf)

    q_t, k, v_t, qs_t, ks, vs_t = _project(
        x, w_tr, w_nat, g_q.reshape(Q_LORA, 1), g_kv.reshape(1, KV_LORA),
        w_uq_t, w_uk, w_uv_t, cos_t, sin_t)

    o_a_t = _mla_attention(q_t, k, v_t)

    bucket, band = _band_geometry()
    bucket = jnp.asarray(np.where(band, bucket, -1).reshape(3, BLOCK, BLOCK), jnp.int32)
    bias_t = _bias_table(rel_bias.astype(f32), sink.astype(f32), bucket)

    o_b_t = _swa_attention(qs_t, ks, vs_t, bias_t)

    w_out_bf = w_out.astype(bf)
    return _out_project(x, o_a_t, o_b_t, gate_a.astype(bf), gate_b.astype(bf),
                        w_out_bf[:D_MLA], w_out_bf[D_MLA:],
                        ln_g.reshape(1, D_MODEL), ln_b.reshape(1, D_MODEL))
```

```python
import functools
import math

import jax
import jax.numpy as jnp
import numpy as np
from jax import lax
from jax.experimental import pallas as pl
from jax.experimental.pallas import tpu as pltpu

D_MODEL = 1024
MLA_HEADS = 8
MLA_NOPE = 64
MLA_ROPE = 32
MLA_V = 64
MLA_QK = MLA_NOPE + MLA_ROPE
Q_LORA = 256
KV_LORA = 128
D_MLA = MLA_HEADS * MLA_V
MLA_SCALE = 1.0 / math.sqrt(MLA_QK)
LOG2_E = math.log2(math.e)
MLA_Q_SCALE = MLA_SCALE * LOG2_E
BF16_SUBLANES = 16
MLA_V_ROWS = MLA_V + BF16_SUBLANES
ROPE_BASE = 10000.0

SWA_HEADS = 8
SWA_KV_HEADS = 2
SWA_HEAD_DIM = 64
SWA_GROUP = SWA_HEADS // SWA_KV_HEADS
D_SWA = SWA_HEADS * SWA_HEAD_DIM
WINDOW = 128
BLOCK = 128
SWA_SCALE = 1.0 / math.sqrt(SWA_HEAD_DIM)
SWA_Q_SCALE = SWA_SCALE * LOG2_E
SWA_V_ROWS = SWA_HEAD_DIM + BF16_SUBLANES
N_BUCKETS = 32
MAX_DISTANCE = 128

DEPTH = 1
ALPHA = (2.0 * DEPTH) ** 0.25

IN_SPLITS = (Q_LORA, KV_LORA, MLA_ROPE, D_MLA, D_SWA,
             SWA_KV_HEADS * SWA_HEAD_DIM, SWA_KV_HEADS * SWA_HEAD_DIM, D_SWA)
IN_OFFSETS = tuple(int(o) for o in np.cumsum((0,) + IN_SPLITS))

LANE = 128
SUBLANES = 8
QK_PAD = LANE
VMEM_LIMIT_BYTES = 56 * 1024 * 1024

PROJ_TOKENS = 1024
PROJ_PARTS = 4
MLA_TQ = 512
MLA_STEP_TILES = 4
MLA_TK = 256
MLA_AHEAD = 2
MLA_REFRESH = 8
MLA_MAX_EXCESS = 64.0
SWA_TOKENS = 4096
SWA_AHEAD = 1
SWA_MAX_EXCESS = 64.0
OUT_TOKENS = 1024
OUT_ROWS = 256

TR_CQ = 0
TR_QS = TR_CQ + Q_LORA
TR_VS = TR_QS + D_SWA
TR_KR = TR_VS + SWA_KV_HEADS * SWA_HEAD_DIM
TR_ROWS = TR_KR + MLA_ROPE
NAT_CKV = 0
NAT_KS = NAT_CKV + KV_LORA
NAT_COLS = NAT_KS + SWA_KV_HEADS * SWA_HEAD_DIM

_NT = (((1,), (1,)), ((), ()))
_TN = (((0,), (0,)), ((), ()))


def _rsqrt_mean_sq(x, axis, eps):
    return lax.rsqrt(jnp.mean(x * x, axis=axis, keepdims=True) + eps)


def _proj_kernel(x_ref, wtr_ref, wnat_ref, gq_ref, gkv_ref, wuqt_ref, wuk_ref, wuvt_ref,
                 cos_t_ref, sin_t_ref,
                 qt_ref, k_ref, vt_ref, qst_ref, ks_ref, vst_ref,
                 tr_ref, nat_ref):
    n_tok = x_ref.shape[1] // PROJ_PARTS
    half = MLA_ROPE // 2

    def project(p):
        tok = slice(p * n_tok, (p + 1) * n_tok)
        xb = x_ref[0, tok, :].astype(jnp.bfloat16)
        tr_ref[:, tok] = lax.dot_general(wtr_ref[...], xb, _NT, preferred_element_type=jnp.float32)
        nat_ref[tok, :] = jnp.dot(xb, wnat_ref[...], preferred_element_type=jnp.float32)

    project(0)
    for p in range(PROJ_PARTS):
        if p + 1 < PROJ_PARTS:
            project(p + 1)
        _proj_finish(p, n_tok, tr_ref, nat_ref, gq_ref, gkv_ref, wuqt_ref, wuk_ref, wuvt_ref,
                     cos_t_ref, sin_t_ref,
                     qt_ref, k_ref, vt_ref, qst_ref, ks_ref, vst_ref)


def _proj_finish(p, n_tok, tr_ref, nat_ref, gq_ref, gkv_ref, wuqt_ref, wuk_ref, wuvt_ref,
                 cos_t_ref, sin_t_ref,
                 qt_ref, k_ref, vt_ref, qst_ref, ks_ref, vst_ref):
    tok = slice(p * n_tok, (p + 1) * n_tok)
    half = MLA_ROPE // 2

    cq = tr_ref[TR_CQ:TR_CQ + Q_LORA, tok]
    cqn = (cq * _rsqrt_mean_sq(cq, 0, 1e-6)) * gq_ref[...]
    q_t = jnp.dot(wuqt_ref[...], cqn.astype(jnp.bfloat16),
                  preferred_element_type=jnp.float32)
    cos_t = cos_t_ref[:, tok]
    sin_t = sin_t_ref[:, tok]
    zeros_pad = jnp.zeros((QK_PAD - MLA_QK, n_tok), jnp.bfloat16)
    for h in range(MLA_HEADS):
        base = h * MLA_QK
        nope = q_t[base:base + MLA_NOPE]
        r1 = q_t[base + MLA_NOPE:base + MLA_NOPE + half]
        r2 = q_t[base + MLA_NOPE + half:base + MLA_QK]
        qt_ref[0, h, 0:MLA_NOPE, tok] = (nope * MLA_Q_SCALE).astype(jnp.bfloat16)
        qt_ref[0, h, MLA_NOPE:MLA_NOPE + half, tok] = (
            (r1 * cos_t - r2 * sin_t) * MLA_Q_SCALE).astype(jnp.bfloat16)
        qt_ref[0, h, MLA_NOPE + half:MLA_QK, tok] = (
            (r2 * cos_t + r1 * sin_t) * MLA_Q_SCALE).astype(jnp.bfloat16)
        qt_ref[0, h, MLA_QK:QK_PAD, tok] = zeros_pad

    ckv = nat_ref[tok, NAT_CKV:NAT_CKV + KV_LORA]
    kvn = ((ckv * _rsqrt_mean_sq(ckv, 1, 1e-6)) * gkv_ref[...]).astype(jnp.bfloat16)
    k_nope = jnp.dot(kvn, wuk_ref[...], preferred_element_type=jnp.float32)
    v_t = lax.dot_general(wuvt_ref[...], kvn, _NT, preferred_element_type=jnp.float32)
    kr1 = tr_ref[TR_KR:TR_KR + half, tok]
    kr2 = tr_ref[TR_KR + half:TR_KR + MLA_ROPE, tok]
    k_rope = jnp.concatenate(
        [jnp.zeros((MLA_NOPE, n_tok), jnp.float32),
         kr1 * cos_t - kr2 * sin_t,
         kr2 * cos_t + kr1 * sin_t,
         jnp.zeros((QK_PAD - MLA_QK, n_tok), jnp.float32)], axis=0).T
    lane = lax.broadcasted_iota(jnp.int32, k_rope.shape, 1)
    row_id = lax.broadcasted_iota(jnp.int32, (BF16_SUBLANES, n_tok), 0)
    ones_row = jnp.where(row_id == 0, 1.0, 0.0).astype(jnp.bfloat16)
    for h in range(MLA_HEADS):
        pair = k_nope[:, (h // 2) * LANE:(h // 2 + 1) * LANE]
        if h % 2:
            pair = pltpu.roll(pair, MLA_NOPE, axis=1)
        k_ref[0, h, tok, :] = jnp.where(lane < MLA_NOPE, pair, k_rope).astype(jnp.bfloat16)
        vt_ref[0, h, 0:MLA_V, tok] = v_t[h * MLA_V:(h + 1) * MLA_V].astype(jnp.bfloat16)
        vt_ref[0, h, MLA_V:MLA_V_ROWS, tok] = ones_row

    n_blocks = n_tok // BLOCK
    for g in range(SWA_KV_HEADS):
        ks_ref[0, g, tok, :] = nat_ref[tok, NAT_KS + g * SWA_HEAD_DIM:
                                       NAT_KS + (g + 1) * SWA_HEAD_DIM].astype(jnp.bfloat16)
        vst_ref[0, g, 0:SWA_HEAD_DIM, tok] = tr_ref[TR_VS + g * SWA_HEAD_DIM:
                                                    TR_VS + (g + 1) * SWA_HEAD_DIM, tok].astype(jnp.bfloat16)
        vst_ref[0, g, SWA_HEAD_DIM:SWA_V_ROWS, tok] = ones_row
        for hh in range(SWA_GROUP):
            row = TR_QS + (g * SWA_GROUP + hh) * SWA_HEAD_DIM
            q_h = (tr_ref[row:row + SWA_HEAD_DIM, tok] * SWA_Q_SCALE).astype(jnp.bfloat16)
            for nb in range(n_blocks):
                col = ((p * n_blocks + nb) * SWA_GROUP + hh) * BLOCK
                qst_ref[0, g, :, col:col + BLOCK] = q_h[:, nb * BLOCK:(nb + 1) * BLOCK]


def _project(x, w_tr, w_nat, g_q, g_kv, w_uq_t, w_uk, w_uv_t, cos_t, sin_t):
    B, S, _ = x.shape
    T = PROJ_TOKENS
    grid = (B, S // T)
    const = lambda *shape: pl.BlockSpec(shape, lambda b, t: (0,) * len(shape))
    bf = jnp.bfloat16
    out_shape = (
        jax.ShapeDtypeStruct((B, MLA_HEADS, QK_PAD, S), bf),
        jax.ShapeDtypeStruct((B, MLA_HEADS, S, QK_PAD), bf),
        jax.ShapeDtypeStruct((B, MLA_HEADS, MLA_V_ROWS, S), bf),
        jax.ShapeDtypeStruct((B, SWA_KV_HEADS, SWA_HEAD_DIM, SWA_GROUP * S), bf),
        jax.ShapeDtypeStruct((B, SWA_KV_HEADS, S, SWA_HEAD_DIM), bf),
        jax.ShapeDtypeStruct((B, SWA_KV_HEADS, SWA_V_ROWS, S), bf),
    )
    out_specs = (
        pl.BlockSpec((1, MLA_HEADS, QK_PAD, T), lambda b, t: (b, 0, 0, t)),
        pl.BlockSpec((1, MLA_HEADS, T, QK_PAD), lambda b, t: (b, 0, t, 0)),
        pl.BlockSpec((1, MLA_HEADS, MLA_V_ROWS, T), lambda b, t: (b, 0, 0, t)),
        pl.BlockSpec((1, SWA_KV_HEADS, SWA_HEAD_DIM, SWA_GROUP * T), lambda b, t: (b, 0, 0, t)),
        pl.BlockSpec((1, SWA_KV_HEADS, T, SWA_HEAD_DIM), lambda b, t: (b, 0, t, 0)),
        pl.BlockSpec((1, SWA_KV_HEADS, SWA_V_ROWS, T), lambda b, t: (b, 0, 0, t)),
    )
    in_specs = [
        pl.BlockSpec((1, T, D_MODEL), lambda b, t: (b, t, 0)),
        const(TR_ROWS, D_MODEL),
        const(D_MODEL, NAT_COLS),
        const(Q_LORA, 1),
        const(1, KV_LORA),
        const(MLA_HEADS * MLA_QK, Q_LORA),
        const(KV_LORA, MLA_HEADS * MLA_NOPE),
        const(D_MLA, KV_LORA),
        pl.BlockSpec((MLA_ROPE // 2, T), lambda b, t: (0, t)),
        pl.BlockSpec((MLA_ROPE // 2, T), lambda b, t: (0, t)),
    ]
    return pl.pallas_call(
        _proj_kernel,
        grid=grid,
        in_specs=in_specs,
        out_specs=out_specs,
        out_shape=out_shape,
        scratch_shapes=[pltpu.VMEM((TR_ROWS, T), jnp.float32),
                        pltpu.VMEM((T, NAT_COLS), jnp.float32)],
        compiler_params=pltpu.CompilerParams(
            dimension_semantics=("arbitrary", "arbitrary"),
            vmem_limit_bytes=VMEM_LIMIT_BYTES),
        name="proj",
    )(x, w_tr, w_nat, g_q, g_kv, w_uq_t, w_uk, w_uv_t, cos_t, sin_t)


def _mla_kernel(qt_ref, k_ref, vt_ref, o_ref, m_ref, acc_ref):
    n_chunks = k_ref.shape[2] // MLA_TK
    n_tiles = qt_ref.shape[3] // MLA_TQ

    def q_tile(t):
        return qt_ref[0, 0, :, t * MLA_TQ:(t + 1) * MLA_TQ]

    def chunk_slice(c):
        start = c * MLA_TK
        return pl.ds(start if isinstance(c, int) else pl.multiple_of(start, MLA_TK), MLA_TK)

    def key_chunk(c):
        return k_ref[0, 0, chunk_slice(c), :]

    def value_chunk(c):
        return vt_ref[0, 0, :, chunk_slice(c)]

    items = [(t, c) for t in range(n_tiles) for c in range(n_chunks)]
    m_use = [None] * n_tiles
    seen = [None] * n_tiles
    acc = [None] * n_tiles
    excess = None

    def issue_scores(i):
        t, c = items[i]
        if c == 0:
            s0 = jnp.dot(k_ref[0, 0, 0:BF16_SUBLANES, :], q_tile(t),
                         preferred_element_type=jnp.float32)
            m_use[t] = jnp.max(s0, axis=0, keepdims=True)
        return jnp.dot(key_chunk(c), q_tile(t), preferred_element_type=jnp.float32)

    pending = [issue_scores(i) for i in range(MLA_AHEAD)]
    for i, (t, c) in enumerate(items):
        if i + MLA_AHEAD < len(items):
            pending.append(issue_scores(i + MLA_AHEAD))
        s = pending.pop(0)
        p = jnp.exp2(s - m_use[t]).astype(jnp.bfloat16)
        pv = jnp.dot(value_chunk(c), p, preferred_element_type=jnp.float32)
        acc[t] = pv if c == 0 else acc[t] + pv
        smax = jnp.max(s.reshape(MLA_TK // SUBLANES, SUBLANES, MLA_TQ), axis=0)
        seen[t] = smax if seen[t] is None else jnp.maximum(seen[t], smax)
        last = c + 1 == n_chunks
        if last or c == 0 or (c + 1) % MLA_REFRESH == 0:
            over = seen[t] - m_use[t]
            excess = over if excess is None else jnp.maximum(excess, over)
            if last:
                acc_ref[t] = acc[t]
            else:
                m_next = jnp.maximum(m_use[t], jnp.max(seen[t], axis=0, keepdims=True))
                acc[t] = acc[t] * jnp.exp2(m_use[t] - m_next)
                m_use[t] = m_next
                seen[t] = None

    @pl.when(jnp.max(excess) > MLA_MAX_EXCESS)
    def _():
        for t in range(n_tiles):
            m_ref[...] = jnp.full(m_ref.shape, -jnp.inf, jnp.float32)
            acc_ref[t] = jnp.zeros(acc_ref.shape[1:], jnp.float32)

            def chunk(c, carry, t=t):
                s = jnp.dot(key_chunk(c), q_tile(t), preferred_element_type=jnp.float32)
                m_prev = m_ref[...]
                m_new = jnp.maximum(m_prev, jnp.max(s, axis=0, keepdims=True))
                p = jnp.exp2(s - m_new).astype(jnp.bfloat16)
                acc_ref[t] = jnp.exp2(m_prev - m_new) * acc_ref[t] + jnp.dot(
                    value_chunk(c), p, preferred_element_type=jnp.float32)
                m_ref[...] = m_new
                return carry

            lax.fori_loop(0, n_chunks, chunk, 0)

    for t in range(n_tiles):
        cols = slice(t * MLA_TQ, (t + 1) * MLA_TQ)
        o = acc_ref[t, 0:MLA_V, :] / acc_ref[t, MLA_V:MLA_V + 1, :]
        o_ref[0, :, cols] = o.astype(o_ref.dtype)


def _mla_attention(q_t, k, v_t):
    B, H, _, S = q_t.shape
    TQ = MLA_TQ
    step_q = MLA_STEP_TILES * TQ
    grid = (B, H, S // step_q)
    return pl.pallas_call(
        _mla_kernel,
        grid=grid,
        in_specs=[
            pl.BlockSpec((1, 1, QK_PAD, step_q), lambda b, h, i: (b, h, 0, i)),
            pl.BlockSpec((1, 1, S, QK_PAD), lambda b, h, i: (b, h, 0, 0)),
            pl.BlockSpec((1, 1, MLA_V_ROWS, S), lambda b, h, i: (b, h, 0, 0)),
        ],
        out_specs=pl.BlockSpec((1, MLA_V, step_q), lambda b, h, i: (b, h, i)),
        out_shape=jax.ShapeDtypeStruct((B, D_MLA, S), jnp.bfloat16),
        scratch_shapes=[pltpu.VMEM((1, TQ), jnp.float32),
                        pltpu.VMEM((MLA_STEP_TILES, MLA_V_ROWS, TQ), jnp.float32)],
        compiler_params=pltpu.CompilerParams(
            dimension_semantics=("arbitrary", "arbitrary", "arbitrary"),
            vmem_limit_bytes=VMEM_LIMIT_BYTES),
        name="mla",
    )(q_t, k, v_t)


def _bias_kernel(rel_ref, sink_ref, bucket_ref, o_ref):
    g = pl.program_id(0)
    for part in range(3):
        bucket = bucket_ref[part]
        for hh in range(SWA_GROUP):
            head = g * SWA_GROUP + hh
            tile = jnp.full((BLOCK, BLOCK), -jnp.inf, jnp.float32)
            for b in range(N_BUCKETS):
                tile = jnp.where(bucket == b, (rel_ref[b, head] - sink_ref[head]) * LOG2_E, tile)
            o_ref[0, part, :, hh * BLOCK:(hh + 1) * BLOCK] = tile


def _bias_table(rel_bias, sink, bucket):
    return pl.pallas_call(
        _bias_kernel,
        grid=(SWA_KV_HEADS,),
        in_specs=[pl.BlockSpec(memory_space=pltpu.SMEM),
                  pl.BlockSpec(memory_space=pltpu.SMEM),
                  pl.BlockSpec((3, BLOCK, BLOCK), lambda g: (0, 0, 0))],
        out_specs=pl.BlockSpec((1, 3, BLOCK, SWA_GROUP * BLOCK), lambda g: (g, 0, 0, 0)),
        out_shape=jax.ShapeDtypeStruct((SWA_KV_HEADS, 3, BLOCK, SWA_GROUP * BLOCK), jnp.float32),
        compiler_params=pltpu.CompilerParams(dimension_semantics=("arbitrary",)),
        name="bias",
    )(rel_bias, sink, bucket)


def _swa_kernel(qst_ref, ks_ref, vst_ref, bias_ref, o_ref):
    n_total = ks_ref.shape[2] // BLOCK
    n_local = o_ref.shape[2] // BLOCK
    step = pl.program_id(2)
    n_steps = pl.num_programs(2)
    neg_inf = jnp.float32(-jnp.inf)

    def band_starts(j):
        n = step * n_local + j
        return tuple(pl.multiple_of(b * BLOCK, BLOCK)
                     for b in (jnp.maximum(n - 1, 0), n, jnp.minimum(n + 1, n_total - 1)))

    def lane_block(j, width):
        start = j * width
        return pl.ds(start if isinstance(j, int) else pl.multiple_of(start, width), width)

    def band_scores(j):
        q_t = qst_ref[0, 0, :, lane_block(j, SWA_GROUP * BLOCK)]
        n = step * n_local + j
        penalties = [None, None, None]
        if not isinstance(j, int) or j == 0:
            penalties[0] = jnp.where(n > 0, 0.0, neg_inf)
        if not isinstance(j, int) or j == n_local - 1:
            penalties[2] = jnp.where(n < n_total - 1, 0.0, neg_inf)
        scores = []
        for part, start in enumerate(band_starts(j)):
            k = ks_ref[0, 0, pl.ds(start, BLOCK), :]
            s = jnp.dot(k, q_t, preferred_element_type=jnp.float32) + bias_ref[0, part]
            if penalties[part] is not None:
                s = s + penalties[part]
            scores.append(s)
        return scores

    def weighted_values(j, weights):
        acc = None
        for part, start in enumerate(band_starts(j)):
            v_t = vst_ref[0, 0, :, pl.ds(start, BLOCK)]
            pv = jnp.dot(v_t, weights[part].astype(jnp.bfloat16),
                         preferred_element_type=jnp.float32)
            acc = pv if acc is None else acc + pv
        return acc

    def write_block(j, acc, sink_weight):
        o = acc[0:SWA_HEAD_DIM] / (acc[SWA_HEAD_DIM:SWA_HEAD_DIM + 1] + sink_weight)
        cols = lane_block(j, BLOCK)
        for hh in range(SWA_GROUP):
            rows = slice(hh * SWA_HEAD_DIM, (hh + 1) * SWA_HEAD_DIM)
            o_ref[0, rows, cols] = o[:, hh * BLOCK:(hh + 1) * BLOCK].astype(o_ref.dtype)

    excess = None
    pending = [band_scores(j) for j in range(SWA_AHEAD)]
    for j in range(n_local):
        if j + SWA_AHEAD < n_local:
            pending.append(band_scores(j + SWA_AHEAD))
        scores = pending.pop(0)
        for s in scores:
            smax = jnp.max(s.reshape(BLOCK // SUBLANES, SUBLANES, s.shape[1]), axis=0)
            excess = smax if excess is None else jnp.maximum(excess, smax)
        write_block(j, weighted_values(j, [jnp.exp2(s) for s in scores]), 1.0)

    @pl.when(jnp.max(excess) > SWA_MAX_EXCESS)
    def _():
        def exact_block(j, carry):
            scores = band_scores(j)
            m = jnp.zeros((1, scores[0].shape[1]), jnp.float32)
            for s in scores:
                m = jnp.maximum(m, jnp.max(s, axis=0, keepdims=True))
            write_block(j, weighted_values(j, [jnp.exp2(s - m) for s in scores]), jnp.exp2(-m))
            return carry

        lax.fori_loop(0, n_local, exact_block, 0)


def _swa_attention(qs_t, ks, vs_t, bias_t):
    B, G, _, S = vs_t.shape
    T = SWA_TOKENS
    grid = (B, G, S // T)
    rows = SWA_GROUP * SWA_HEAD_DIM
    return pl.pallas_call(
        _swa_kernel,
        grid=grid,
        in_specs=[
            pl.BlockSpec((1, 1, SWA_HEAD_DIM, SWA_GROUP * T), lambda b, g, t: (b, g, 0, t)),
            pl.BlockSpec((1, 1, S, SWA_HEAD_DIM), lambda b, g, t: (b, g, 0, 0)),
            pl.BlockSpec((1, 1, SWA_V_ROWS, S), lambda b, g, t: (b, g, 0, 0)),
            pl.BlockSpec((1, 3, BLOCK, SWA_GROUP * BLOCK), lambda b, g, t: (g, 0, 0, 0)),
        ],
        out_specs=pl.BlockSpec((1, rows, T), lambda b, g, t: (b, g, t)),
        out_shape=jax.ShapeDtypeStruct((B, D_SWA, S), jnp.bfloat16),
        compiler_params=pltpu.CompilerParams(
            dimension_semantics=("arbitrary", "arbitrary", "arbitrary"),
            vmem_limit_bytes=VMEM_LIMIT_BYTES),
        name="swa",
    )(qs_t, ks, vs_t, bias_t)


def _out_kernel(x_ref, oa_ref, ob_ref, wga_ref, wgb_ref, wa_ref, wb_ref, g_ref, b_ref, y_ref):
    n_chunks = x_ref.shape[1] // OUT_ROWS

    def gate_paths(c):
        xb = x_ref[0, c * OUT_ROWS:(c + 1) * OUT_ROWS, :].astype(jnp.bfloat16)
        return [jnp.dot(xb, wg_ref[...], preferred_element_type=jnp.float32)
                for wg_ref in (wga_ref, wgb_ref)]

    def project(c, gates):
        toks = slice(c * OUT_ROWS, (c + 1) * OUT_ROWS)
        out = None
        for o_ref_, gate, w_ref in ((oa_ref, gates[0], wa_ref), (ob_ref, gates[1], wb_ref)):
            o_nat = o_ref_[0, :, toks].astype(jnp.float32).T
            mixed = (o_nat * jax.nn.silu(gate)).astype(jnp.bfloat16)
            part = jnp.dot(mixed, w_ref[...], preferred_element_type=jnp.float32)
            out = part if out is None else out + part
        return out

    def residual_norm(c, out):
        rows = slice(c * OUT_ROWS, (c + 1) * OUT_ROWS)
        h = ALPHA * x_ref[0, rows, :] + out
        mu = jnp.mean(h, axis=-1, keepdims=True)
        var = jnp.mean(jnp.square(h - mu), axis=-1, keepdims=True)
        y_ref[0, rows, :] = ((h - mu) * lax.rsqrt(var + 1e-5)) * g_ref[...] + b_ref[...]

    gates_next = gate_paths(0)
    out_prev = None
    for c in range(n_chunks):
        gates = gates_next
        if c + 1 < n_chunks:
            gates_next = gate_paths(c + 1)
        out = project(c, gates)
        if out_prev is not None:
            residual_norm(c - 1, out_prev)
        out_prev = out
    residual_norm(n_chunks - 1, out_prev)


def _out_project(x, o_a_t, o_b_t, w_gate_a, w_gate_b, w_a, w_b, ln_g, ln_b):
    B, S, D = x.shape
    T = OUT_TOKENS
    grid = (B, S // T)
    const = lambda *shape: pl.BlockSpec(shape, lambda b, t: (0,) * len(shape))
    return pl.pallas_call(
        _out_kernel,
        grid=grid,
        in_specs=[
            pl.BlockSpec((1, T, D), lambda b, t: (b, t, 0)),
            pl.BlockSpec((1, D_MLA, T), lambda b, t: (b, 0, t)),
            pl.BlockSpec((1, D_SWA, T), lambda b, t: (b, 0, t)),
            const(D, D_MLA),
            const(D, D_SWA),
            const(D_MLA, D),
            const(D_SWA, D),
            const(1, D),
            const(1, D),
        ],
        out_specs=pl.BlockSpec((1, T, D), lambda b, t: (b, t, 0)),
        out_shape=jax.ShapeDtypeStruct((B, S, D), x.dtype),
        compiler_params=pltpu.CompilerParams(
            dimension_semantics=("arbitrary", "arbitrary"),
            vmem_limit_bytes=VMEM_LIMIT_BYTES),
        name="outproj",
    )(x, o_a_t, o_b_t, w_gate_a, w_gate_b, w_a, w_b, ln_g, ln_b)


def _t5_bucket(rel):
    half = N_BUCKETS // 2
    ret = np.where(rel > 0, half, 0)
    n = np.abs(rel)
    max_exact = half // 2
    large = max_exact + (np.log(np.maximum(n, 1).astype(np.float32) / max_exact)
                         / np.log(MAX_DISTANCE / max_exact) * (half - max_exact)).astype(np.int32)
    large = np.minimum(large, half - 1)
    return (ret + np.where(n < max_exact, n, large)).astype(np.int32)


def _band_geometry():
    q_loc = np.arange(BLOCK)
    k_loc = np.arange(3 * BLOCK) - BLOCK
    rel = k_loc[:, None] - q_loc[None, :]
    return _t5_bucket(rel), np.abs(rel) <= WINDOW


def _col(w_in, i):
    return w_in[:, IN_OFFSETS[i]:IN_OFFSETS[i + 1]]


def kernel(x, w_in, g_q, g_kv, w_uq, w_ukv, sink, rel_bias, w_out, ln_g, ln_b):
    B, S, _ = x.shape
    bf = jnp.bfloat16
    f32 = jnp.float32

    pos = jnp.arange(S, dtype=f32)
    inv_freq = ROPE_BASE ** (-jnp.arange(0, MLA_ROPE, 2, dtype=f32) / MLA_ROPE)
    ang_t = inv_freq[:, None] * pos[None, :]
    cos_t, sin_t = jnp.cos(ang_t), jnp.sin(ang_t)

    c_q, c_kv, k_rope, gate_a, q_s, k_s, v_s, gate_b = (_col(w_in, i) for i in range(8))
    w_tr = jnp.concatenate([c_q, q_s, v_s, k_rope], axis=1).T.astype(bf)
    w_nat = jnp.concatenate([c_kv, k_s], axis=1).astype(bf)
    w_ukv3 = w_ukv.reshape(KV_LORA, MLA_HEADS, MLA_NOPE + MLA_V)
    w_uk = w_ukv3[:, :, :MLA_NOPE].reshape(KV_LORA, MLA_HEADS * MLA_NOPE).astype(bf)
    w_uv_t = w_ukv3[:, :, MLA_NOPE:].reshape(KV_LORA, D_MLA).T.astype(bf)
    w_uq_t = w_uq.T.astype(bf)

    q_t, k, v_t, qs_t, ks, vs_t = _project(
        x, w_tr, w_nat, g_q.reshape(Q_LORA, 1), g_kv.reshape(1, KV_LORA),
        w_uq_t, w_uk, w_uv_t, cos_t, sin_t)

    o_a_t = _mla_attention(q_t, k, v_t)

    bucket, band = _band_geometry()
    bucket = jnp.asarray(np.where(band, bucket, -1).reshape(3, BLOCK, BLOCK), jnp.int32)
    bias_t = _bias_table(rel_bias.astype(f32), sink.astype(f32), bucket)

    o_b_t = _swa_attention(qs_t, ks, vs_t, bias_t)

    w_out_bf = w_out.astype(bf)
    return _out_project(x, o_a_t, o_b_t, gate_a.astype(bf), gate_b.astype(bf),
                        w_out_bf[:D_MLA], w_out_bf[D_MLA:],
                        ln_g.reshape(1, D_MODEL), ln_b.reshape(1, D_MODEL))
```

```python
import functools
import math

import jax
import jax.numpy as jnp
import numpy as np
from jax import lax
from jax.experimental import pallas as pl
from jax.experimental.pallas import tpu as pltpu

D_MODEL = 1024
MLA_HEADS = 8
MLA_NOPE = 64
MLA_ROPE = 32
MLA_V = 64
MLA_QK = MLA_NOPE + MLA_ROPE
Q_LORA = 256
KV_LORA = 128
D_MLA = MLA_HEADS * MLA_V
MLA_SCALE = 1.0 / math.sqrt(MLA_QK)
LOG2_E = math.log2(math.e)
MLA_Q_SCALE = MLA_SCALE * LOG2_E
BF16_SUBLANES = 16
MLA_V_ROWS = MLA_V + BF16_SUBLANES
ROPE_BASE = 10000.0

SWA_HEADS = 8
SWA_KV_HEADS = 2
SWA_HEAD_DIM = 64
SWA_GROUP = SWA_HEADS // SWA_KV_HEADS
D_SWA = SWA_HEADS * SWA_HEAD_DIM
WINDOW = 128
BLOCK = 128
SWA_SCALE = 1.0 / math.sqrt(SWA_HEAD_DIM)
SWA_Q_SCALE = SWA_SCALE * LOG2_E
SWA_V_ROWS = SWA_HEAD_DIM + BF16_SUBLANES
N_BUCKETS = 32
MAX_DISTANCE = 128

DEPTH = 1
ALPHA = (2.0 * DEPTH) ** 0.25

IN_SPLITS = (Q_LORA, KV_LORA, MLA_ROPE, D_MLA, D_SWA,
             SWA_KV_HEADS * SWA_HEAD_DIM, SWA_KV_HEADS * SWA_HEAD_DIM, D_SWA)
IN_OFFSETS = tuple(int(o) for o in np.cumsum((0,) + IN_SPLITS))

LANE = 128
SUBLANES = 8
QK_PAD = LANE
VMEM_LIMIT_BYTES = 56 * 1024 * 1024

PROJ_TOKENS = 1024
PROJ_PARTS = 4
MLA_TQ = 512
MLA_STEP_TILES = 4
MLA_TK = 256
MLA_AHEAD = 2
MLA_REFRESH = 8
MLA_MAX_EXCESS = 64.0
SWA_TOKENS = 4096
SWA_AHEAD = 1
SWA_MAX_EXCESS = 64.0
OUT_TOKENS = 1024
OUT_ROWS = 256

TR_CQ = 0
TR_QS = TR_CQ + Q_LORA
TR_VS = TR_QS + D_SWA
TR_KR = TR_VS + SWA_KV_HEADS * SWA_HEAD_DIM
TR_ROWS = TR_KR + MLA_ROPE
NAT_CKV = 0
NAT_KS = NAT_CKV + KV_LORA
NAT_COLS = NAT_KS + SWA_KV_HEADS * SWA_HEAD_DIM

_NT = (((1,), (1,)), ((), ()))
_TN = (((0,), (0,)), ((), ()))


def _rsqrt_mean_sq(x, axis, eps):
    return lax.rsqrt(jnp.mean(x * x, axis=axis, keepdims=True) + eps)


def _proj_kernel(x_ref, wtr_ref, wnat_ref, gq_ref, gkv_ref, wuqt_ref, wuk_ref, wuvt_ref,
                 cos_t_ref, sin_t_ref,
                 qt_ref, k_ref, vt_ref, qst_ref, ks_ref, vst_ref,
                 tr_ref, nat_ref):
    n_tok = x_ref.shape[1] // PROJ_PARTS
    half = MLA_ROPE // 2

    def project(p):
        tok = slice(p * n_tok, (p + 1) * n_tok)
        xb = x_ref[0, tok, :].astype(jnp.bfloat16)
        tr_ref[:, tok] = lax.dot_general(wtr_ref[...], xb, _NT, preferred_element_type=jnp.float32)
        nat_ref[tok, :] = jnp.dot(xb, wnat_ref[...], preferred_element_type=jnp.float32)

    project(0)
    for p in range(PROJ_PARTS):
        if p + 1 < PROJ_PARTS:
            project(p + 1)
        _proj_finish(p, n_tok, tr_ref, nat_ref, gq_ref, gkv_ref, wuqt_ref, wuk_ref, wuvt_ref,
                     cos_t_ref, sin_t_ref,
                     qt_ref, k_ref, vt_ref, qst_ref, ks_ref, vst_ref)


def _proj_finish(p, n_tok, tr_ref, nat_ref, gq_ref, gkv_ref, wuqt_ref, wuk_ref, wuvt_ref,
                 cos_t_ref, sin_t_ref,
                 qt_ref, k_ref, vt_ref, qst_ref, ks_ref, vst_ref):
    tok = slice(p * n_tok, (p + 1) * n_tok)
    half = MLA_ROPE // 2

    cq = tr_ref[TR_CQ:TR_CQ + Q_LORA, tok]
    cqn = (cq * _rsqrt_mean_sq(cq, 0, 1e-6)) * gq_ref[...]
    q_t = jnp.dot(wuqt_ref[...], cqn.astype(jnp.bfloat16),
                  preferred_element_type=jnp.float32)
    cos_t = cos_t_ref[:, tok]
    sin_t = sin_t_ref[:, tok]
    zeros_pad = jnp.zeros((QK_PAD - MLA_QK, n_tok), jnp.bfloat16)
    for h in range(MLA_HEADS):
        base = h * MLA_QK
        nope = q_t[base:base + MLA_NOPE]
        r1 = q_t[base + MLA_NOPE:base + MLA_NOPE + half]
        r2 = q_t[base + MLA_NOPE + half:base + MLA_QK]
        qt_ref[0, h, 0:MLA_NOPE, tok] = (nope * MLA_Q_SCALE).astype(jnp.bfloat16)
        qt_ref[0, h, MLA_NOPE:MLA_NOPE + half, tok] = (
            (r1 * cos_t - r2 * sin_t) * MLA_Q_SCALE).astype(jnp.bfloat16)
        qt_ref[0, h, MLA_NOPE + half:MLA_QK, tok] = (
            (r2 * cos_t + r1 * sin_t) * MLA_Q_SCALE).astype(jnp.bfloat16)
        qt_ref[0, h, MLA_QK:QK_PAD, tok] = zeros_pad

    ckv = nat_ref[tok, NAT_CKV:NAT_CKV + KV_LORA]
    kvn = ((ckv * _rsqrt_mean_sq(ckv, 1, 1e-6)) * gkv_ref[...]).astype(jnp.bfloat16)
    k_nope = jnp.dot(kvn, wuk_ref[...], preferred_element_type=jnp.float32)
    v_t = lax.dot_general(wuvt_ref[...], kvn, _NT, preferred_element_type=jnp.float32)
    kr1 = tr_ref[TR_KR:TR_KR + half, tok]
    kr2 = tr_ref[TR_KR + half:TR_KR + MLA_ROPE, tok]
    k_rope = jnp.concatenate(
        [jnp.zeros((MLA_NOPE, n_tok), jnp.float32),
         kr1 * cos_t - kr2 * sin_t,
         kr2 * cos_t + kr1 * sin_t,
         jnp.zeros((QK_PAD - MLA_QK, n_tok), jnp.float32)], axis=0).T
    lane = lax.broadcasted_iota(jnp.int32, k_rope.shape, 1)
    row_id = lax.broadcasted_iota(jnp.int32, (BF16_SUBLANES, n_tok), 0)
    ones_row = jnp.where(row_id == 0, 1.0, 0.0).astype(jnp.bfloat16)
    for h in range(MLA_HEADS):
        pair = k_nope[:, (h // 2) * LANE:(h // 2 + 1) * LANE]
        if h % 2:
            pair = pltpu.roll(pair, MLA_NOPE, axis=1)
        k_ref[0, h, tok, :] = jnp.where(lane < MLA_NOPE, pair, k_rope).astype(jnp.bfloat16)
        vt_ref[0, h, 0:MLA_V, tok] = v_t[h * MLA_V:(h + 1) * MLA_V].astype(jnp.bfloat16)
        vt_ref[0, h, MLA_V:MLA_V_ROWS, tok] = ones_row

    n_blocks = n_tok // BLOCK
    for g in range(SWA_KV_HEADS):
        ks_ref[0, g, tok, :] = nat_ref[tok, NAT_KS + g * SWA_HEAD_DIM:
                                       NAT_KS + (g + 1) * SWA_HEAD_DIM].astype(jnp.bfloat16)
        vst_ref[0, g, 0:SWA_HEAD_DIM, tok] = tr_ref[TR_VS + g * SWA_HEAD_DIM:
                                                    TR_VS + (g + 1) * SWA_HEAD_DIM, tok].astype(jnp.bfloat16)
        vst_ref[0, g, SWA_HEAD_DIM:SWA_V_ROWS, tok] = ones_row
        for hh in range(SWA_GROUP):
            row = TR_QS + (g * SWA_GROUP + hh) * SWA_HEAD_DIM
            q_h = (tr_ref[row:row + SWA_HEAD_DIM, tok] * SWA_Q_SCALE).astype(jnp.bfloat16)
            for nb in range(n_blocks):
                col = ((p * n_blocks + nb) * SWA_GROUP + hh) * BLOCK
                qst_ref[0, g, :, col:col + BLOCK] = q_h[:, nb * BLOCK:(nb + 1) * BLOCK]


def _project(x, w_tr, w_nat, g_q, g_kv, w_uq_t, w_uk, w_uv_t, cos_t, sin_t):
    B, S, _ = x.shape
    T = PROJ_TOKENS
    grid = (B, S // T)
    const = lambda *shape: pl.BlockSpec(shape, lambda b, t: (0,) * len(shape))
    bf = jnp.bfloat16
    out_shape = (
        jax.ShapeDtypeStruct((B, MLA_HEADS, QK_PAD, S), bf),
        jax.ShapeDtypeStruct((B, MLA_HEADS, S, QK_PAD), bf),
        jax.ShapeDtypeStruct((B, MLA_HEADS, MLA_V_ROWS, S), bf),
        jax.ShapeDtypeStruct((B, SWA_KV_HEADS, SWA_HEAD_DIM, SWA_GROUP * S), bf),
        jax.ShapeDtypeStruct((B, SWA_KV_HEADS, S, SWA_HEAD_DIM), bf),
        jax.ShapeDtypeStruct((B, SWA_KV_HEADS, SWA_V_ROWS, S), bf),
    )
    out_specs = (
        pl.BlockSpec((1, MLA_HEADS, QK_PAD, T), lambda b, t: (b, 0, 0, t)),
        pl.BlockSpec((1, MLA_HEADS, T, QK_PAD), lambda b, t: (b, 0, t, 0)),
        pl.BlockSpec((1, MLA_HEADS, MLA_V_ROWS, T), lambda b, t: (b, 0, 0, t)),
        pl.BlockSpec((1, SWA_KV_HEADS, SWA_HEAD_DIM, SWA_GROUP * T), lambda b, t: (b, 0, 0, t)),
        pl.BlockSpec((1, SWA_KV_HEADS, T, SWA_HEAD_DIM), lambda b, t: (b, 0, t, 0)),
        pl.BlockSpec((1, SWA_KV_HEADS, SWA_V_ROWS, T), lambda b, t: (b, 0, 0, t)),
    )
    in_specs = [
        pl.BlockSpec((1, T, D_MODEL), lambda b, t: (b, t, 0)),
        const(TR_ROWS, D_MODEL),
        const(D_MODEL, NAT_COLS),
        const(Q_LORA, 1),
        const(1, KV_LORA),
        const(MLA_HEADS * MLA_QK, Q_LORA),
        const(KV_LORA, MLA_HEADS * MLA_NOPE),
        const(D_MLA, KV_LORA),
        pl.BlockSpec((MLA_ROPE // 2, T), lambda b, t: (0, t)),
        pl.BlockSpec((MLA_ROPE // 2, T), lambda b, t: (0, t)),
    ]
    return pl.pallas_call(
        _proj_kernel,
        grid=grid,
        in_specs=in_specs,
        out_specs=out_specs,
        out_shape=out_shape,
        scratch_shapes=[pltpu.VMEM((TR_ROWS, T), jnp.float32),
                        pltpu.VMEM((T, NAT_COLS), jnp.float32)],
        compiler_params=pltpu.CompilerParams(
            dimension_semantics=("arbitrary", "arbitrary"),
            vmem_limit_bytes=VMEM_LIMIT_BYTES),
        name="proj",
    )(x, w_tr, w_nat, g_q, g_kv, w_uq_t, w_uk, w_uv_t, cos_t, sin_t)


def _mla_kernel(qt_ref, k_ref, vt_ref, o_ref, m_ref, acc_ref):
    n_chunks = k_ref.shape[2] // MLA_TK
    n_tiles = qt_ref.shape[3] // MLA_TQ

    def q_tile(t):
        return qt_ref[0, 0, :, t * MLA_TQ:(t + 1) * MLA_TQ]

    def chunk_slice(c):
        start = c * MLA_TK
        return pl.ds(start if isinstance(c, int) else pl.multiple_of(start, MLA_TK), MLA_TK)

    def key_chunk(c):
        return k_ref[0, 0, chunk_slice(c), :]

    def value_chunk(c):
        return vt_ref[0, 0, :, chunk_slice(c)]

    items = [(t, c) for t in range(n_tiles) for c in range(n_chunks)]
    m_use = [None] * n_tiles
    seen = [None] * n_tiles
    acc = [None] * n_tiles
    den = [None] * n_tiles
    excess = None

    def issue_scores(i):
        t, c = items[i]
        if c == 0:
            s0 = jnp.dot(k_ref[0, 0, 0:BF16_SUBLANES, :], q_tile(t),
                         preferred_element_type=jnp.float32)
            m_use[t] = jnp.max(s0, axis=0, keepdims=True)
        return jnp.dot(key_chunk(c), q_tile(t), preferred_element_type=jnp.float32)

    pending = [issue_scores(i) for i in range(MLA_AHEAD)]
    for i, (t, c) in enumerate(items):
        if i + MLA_AHEAD < len(items):
            pending.append(issue_scores(i + MLA_AHEAD))
        s = pending.pop(0)
        p = jnp.exp2(s - m_use[t])
        psum = jnp.sum(p.reshape(MLA_TK // SUBLANES, SUBLANES, MLA_TQ), axis=0)
        den[t] = psum if c == 0 else den[t] + psum
        p = p.astype(jnp.bfloat16)
        pv = jnp.dot(vt_ref[0, 0, 0:MLA_V, chunk_slice(c)], p,
                     preferred_element_type=jnp.float32)
        acc[t] = pv if c == 0 else acc[t] + pv
        pmax = jnp.max(p.reshape(MLA_TK // BF16_SUBLANES, BF16_SUBLANES, MLA_TQ), axis=0)
        seen[t] = pmax if seen[t] is None else jnp.maximum(seen[t], pmax)
        last = c + 1 == n_chunks
        if last or c == 0 or (c + 1) % MLA_REFRESH == 0:
            excess = seen[t] if excess is None else jnp.maximum(excess, seen[t])
            if last:
                acc_ref[t, 0:MLA_V, :] = acc[t]
                acc_ref[t, MLA_V:MLA_V + 1, :] = jnp.sum(den[t], axis=0, keepdims=True)
            else:
                top = jnp.max(seen[t].astype(jnp.float32), axis=0, keepdims=True)
                rise = jnp.maximum(jnp.log2(top), 0.0)
                alpha = jnp.exp2(-rise)
                acc[t] = acc[t] * alpha
                den[t] = den[t] * alpha
                m_use[t] = m_use[t] + rise
                seen[t] = None

    @pl.when(jnp.max(excess.astype(jnp.float32)) > 2.0 ** MLA_MAX_EXCESS)
    def _():
        for t in range(n_tiles):
            m_ref[...] = jnp.full(m_ref.shape, -jnp.inf, jnp.float32)
            acc_ref[t] = jnp.zeros(acc_ref.shape[1:], jnp.float32)

            def chunk(c, carry, t=t):
                s = jnp.dot(key_chunk(c), q_tile(t), preferred_element_type=jnp.float32)
                m_prev = m_ref[...]
                m_new = jnp.maximum(m_prev, jnp.max(s, axis=0, keepdims=True))
                p = jnp.exp2(s - m_new).astype(jnp.bfloat16)
                acc_ref[t] = jnp.exp2(m_prev - m_new) * acc_ref[t] + jnp.dot(
                    value_chunk(c), p, preferred_element_type=jnp.float32)
                m_ref[...] = m_new
                return carry

            lax.fori_loop(0, n_chunks, chunk, 0)

    for t in range(n_tiles):
        cols = slice(t * MLA_TQ, (t + 1) * MLA_TQ)
        o = acc_ref[t, 0:MLA_V, :] / acc_ref[t, MLA_V:MLA_V + 1, :]
        o_ref[0, :, cols] = o.astype(o_ref.dtype)


def _mla_attention(q_t, k, v_t):
    B, H, _, S = q_t.shape
    TQ = MLA_TQ
    step_q = MLA_STEP_TILES * TQ
    grid = (B, H, S // step_q)
    return pl.pallas_call(
        _mla_kernel,
        grid=grid,
        in_specs=[
            pl.BlockSpec((1, 1, QK_PAD, step_q), lambda b, h, i: (b, h, 0, i)),
            pl.BlockSpec((1, 1, S, QK_PAD), lambda b, h, i: (b, h, 0, 0)),
            pl.BlockSpec((1, 1, MLA_V_ROWS, S), lambda b, h, i: (b, h, 0, 0)),
        ],
        out_specs=pl.BlockSpec((1, MLA_V, step_q), lambda b, h, i: (b, h, i)),
        out_shape=jax.ShapeDtypeStruct((B, D_MLA, S), jnp.bfloat16),
        scratch_shapes=[pltpu.VMEM((1, TQ), jnp.float32),
                        pltpu.VMEM((MLA_STEP_TILES, MLA_V_ROWS, TQ), jnp.float32)],
        compiler_params=pltpu.CompilerParams(
            dimension_semantics=("arbitrary", "arbitrary", "arbitrary"),
            vmem_limit_bytes=VMEM_LIMIT_BYTES),
        name="mla",
    )(q_t, k, v_t)


def _bias_kernel(rel_ref, sink_ref, bucket_ref, o_ref):
    g = pl.program_id(0)
    for part in range(3):
        bucket = bucket_ref[part]
        for hh in range(SWA_GROUP):
            head = g * SWA_GROUP + hh
            tile = jnp.full((BLOCK, BLOCK), -jnp.inf, jnp.float32)
            for b in range(N_BUCKETS):
                tile = jnp.where(bucket == b, (rel_ref[b, head] - sink_ref[head]) * LOG2_E, tile)
            o_ref[0, part, :, hh * BLOCK:(hh + 1) * BLOCK] = tile


def _bias_table(rel_bias, sink, bucket):
    return pl.pallas_call(
        _bias_kernel,
        grid=(SWA_KV_HEADS,),
        in_specs=[pl.BlockSpec(memory_space=pltpu.SMEM),
                  pl.BlockSpec(memory_space=pltpu.SMEM),
                  pl.BlockSpec((3, BLOCK, BLOCK), lambda g: (0, 0, 0))],
        out_specs=pl.BlockSpec((1, 3, BLOCK, SWA_GROUP * BLOCK), lambda g: (g, 0, 0, 0)),
        out_shape=jax.ShapeDtypeStruct((SWA_KV_HEADS, 3, BLOCK, SWA_GROUP * BLOCK), jnp.float32),
        compiler_params=pltpu.CompilerParams(dimension_semantics=("arbitrary",)),
        name="bias",
    )(rel_bias, sink, bucket)


def _swa_kernel(qst_ref, ks_ref, vst_ref, bias_ref, o_ref):
    n_total = ks_ref.shape[2] // BLOCK
    n_local = o_ref.shape[2] // BLOCK
    step = pl.program_id(2)
    n_steps = pl.num_programs(2)
    neg_inf = jnp.float32(-jnp.inf)

    def band_starts(j):
        n = step * n_local + j
        return tuple(pl.multiple_of(b * BLOCK, BLOCK)
                     for b in (jnp.maximum(n - 1, 0), n, jnp.minimum(n + 1, n_total - 1)))

    def lane_block(j, width):
        start = j * width
        return pl.ds(start if isinstance(j, int) else pl.multiple_of(start, width), width)

    def band_scores(j):
        q_t = qst_ref[0, 0, :, lane_block(j, SWA_GROUP * BLOCK)]
        n = step * n_local + j
        penalties = [None, None, None]
        if not isinstance(j, int) or j == 0:
            penalties[0] = jnp.where(n > 0, 0.0, neg_inf)
        if not isinstance(j, int) or j == n_local - 1:
            penalties[2] = jnp.where(n < n_total - 1, 0.0, neg_inf)
        scores = []
        for part, start in enumerate(band_starts(j)):
            k = ks_ref[0, 0, pl.ds(start, BLOCK), :]
            s = jnp.dot(k, q_t, preferred_element_type=jnp.float32) + bias_ref[0, part]
            if penalties[part] is not None:
                s = s + penalties[part]
            scores.append(s)
        return scores

    def weighted_values(j, weights):
        acc = None
        for part, start in enumerate(band_starts(j)):
            v_t = vst_ref[0, 0, :, pl.ds(start, BLOCK)]
            pv = jnp.dot(v_t, weights[part].astype(jnp.bfloat16),
                         preferred_element_type=jnp.float32)
            acc = pv if acc is None else acc + pv
        return acc

    def write_block(j, acc, sink_weight):
        o = acc[0:SWA_HEAD_DIM] / (acc[SWA_HEAD_DIM:SWA_HEAD_DIM + 1] + sink_weight)
        cols = lane_block(j, BLOCK)
        for hh in range(SWA_GROUP):
            rows = slice(hh * SWA_HEAD_DIM, (hh + 1) * SWA_HEAD_DIM)
            o_ref[0, rows, cols] = o[:, hh * BLOCK:(hh + 1) * BLOCK].astype(o_ref.dtype)

    excess = None
    pending = [band_scores(j) for j in range(SWA_AHEAD)]
    for j in range(n_local):
        if j + SWA_AHEAD < n_local:
            pending.append(band_scores(j + SWA_AHEAD))
        scores = pending.pop(0)
        for s in scores:
            smax = jnp.max(s.reshape(BLOCK // SUBLANES, SUBLANES, s.shape[1]), axis=0)
            excess = smax if excess is None else jnp.maximum(excess, smax)
        write_block(j, weighted_values(j, [jnp.exp2(s) for s in scores]), 1.0)

    @pl.when(jnp.max(excess) > SWA_MAX_EXCESS)
    def _():
        def exact_block(j, carry):
            scores = band_scores(j)
            m = jnp.zeros((1, scores[0].shape[1]), jnp.float32)
            for s in scores:
                m = jnp.maximum(m, jnp.max(s, axis=0, keepdims=True))
            write_block(j, weighted_values(j, [jnp.exp2(s - m) for s in scores]), jnp.exp2(-m))
            return carry

        lax.fori_loop(0, n_local, exact_block, 0)


def _swa_attention(qs_t, ks, vs_t, bias_t):
    B, G, _, S = vs_t.shape
    T = SWA_TOKENS
    grid = (B, G, S // T)
    rows = SWA_GROUP * SWA_HEAD_DIM
    return pl.pallas_call(
        _swa_kernel,
        grid=grid,
        in_specs=[
            pl.BlockSpec((1, 1, SWA_HEAD_DIM, SWA_GROUP * T), lambda b, g, t: (b, g, 0, t)),
            pl.BlockSpec((1, 1, S, SWA_HEAD_DIM), lambda b, g, t: (b, g, 0, 0)),
            pl.BlockSpec((1, 1, SWA_V_ROWS, S), lambda b, g, t: (b, g, 0, 0)),
            pl.BlockSpec((1, 3, BLOCK, SWA_GROUP * BLOCK), lambda b, g, t: (g, 0, 0, 0)),
        ],
        out_specs=pl.BlockSpec((1, rows, T), lambda b, g, t: (b, g, t)),
        out_shape=jax.ShapeDtypeStruct((B, D_SWA, S), jnp.bfloat16),
        compiler_params=pltpu.CompilerParams(
            dimension_semantics=("arbitrary", "arbitrary", "arbitrary"),
            vmem_limit_bytes=VMEM_LIMIT_BYTES),
        name="swa",
    )(qs_t, ks, vs_t, bias_t)


def _out_kernel(x_ref, oa_ref, ob_ref, wga_ref, wgb_ref, wa_ref, wb_ref, g_ref, b_ref, y_ref):
    n_chunks = x_ref.shape[1] // OUT_ROWS

    def gate_paths(c):
        xb = x_ref[0, c * OUT_ROWS:(c + 1) * OUT_ROWS, :].astype(jnp.bfloat16)
        return [jnp.dot(xb, wg_ref[...], preferred_element_type=jnp.float32)
                for wg_ref in (wga_ref, wgb_ref)]

    def project(c, gates):
        toks = slice(c * OUT_ROWS, (c + 1) * OUT_ROWS)
        out = None
        for o_ref_, gate, w_ref in ((oa_ref, gates[0], wa_ref), (ob_ref, gates[1], wb_ref)):
            o_nat = o_ref_[0, :, toks].astype(jnp.float32).T
            mixed = (o_nat * jax.nn.silu(gate)).astype(jnp.bfloat16)
            part = jnp.dot(mixed, w_ref[...], preferred_element_type=jnp.float32)
            out = part if out is None else out + part
        return out

    def residual_norm(c, out):
        rows = slice(c * OUT_ROWS, (c + 1) * OUT_ROWS)
        h = ALPHA * x_ref[0, rows, :] + out
        mu = jnp.mean(h, axis=-1, keepdims=True)
        var = jnp.mean(jnp.square(h - mu), axis=-1, keepdims=True)
        y_ref[0, rows, :] = ((h - mu) * lax.rsqrt(var + 1e-5)) * g_ref[...] + b_ref[...]

    gates_next = gate_paths(0)
    out_prev = None
    for c in range(n_chunks):
        gates = gates_next
        if c + 1 < n_chunks:
            gates_next = gate_paths(c + 1)
        out = project(c, gates)
        if out_prev is not None:
            residual_norm(c - 1, out_prev)
        out_prev = out
    residual_norm(n_chunks - 1, out_prev)


def _out_project(x, o_a_t, o_b_t, w_gate_a, w_gate_b, w_a, w_b, ln_g, ln_b):
    B, S, D = x.shape
    T = OUT_TOKENS
    grid = (B, S // T)
    const = lambda *shape: pl.BlockSpec(shape, lambda b, t: (0,) * len(shape))
    return pl.pallas_call(
        _out_kernel,
        grid=grid,
        in_specs=[
            pl.BlockSpec((1, T, D), lambda b, t: (b, t, 0)),
            pl.BlockSpec((1, D_MLA, T), lambda b, t: (b, 0, t)),
            pl.BlockSpec((1, D_SWA, T), lambda b, t: (b, 0, t)),
            const(D, D_MLA),
            const(D, D_SWA),
            const(D_MLA, D),
            const(D_SWA, D),
            const(1, D),
            const(1, D),
        ],
        out_specs=pl.BlockSpec((1, T, D), lambda b, t: (b, t, 0)),
        out_shape=jax.ShapeDtypeStruct((B, S, D), x.dtype),
        compiler_params=pltpu.CompilerParams(
            dimension_semantics=("arbitrary", "arbitrary"),
            vmem_limit_bytes=VMEM_LIMIT_BYTES),
        name="outproj",
    )(x, o_a_t, o_b_t, w_gate_a, w_gate_b, w_a, w_b, ln_g, ln_b)


def _t5_bucket(rel):
    half = N_BUCKETS // 2
    ret = np.where(rel > 0, half, 0)
    n = np.abs(rel)
    max_exact = half // 2
    large = max_exact + (np.log(np.maximum(n, 1).astype(np.float32) / max_exact)
                         / np.log(MAX_DISTANCE / max_exact) * (half - max_exact)).astype(np.int32)
    large = np.minimum(large, half - 1)
    return (ret + np.where(n < max_exact, n, large)).astype(np.int32)


def _band_geometry():
    q_loc = np.arange(BLOCK)
    k_loc = np.arange(3 * BLOCK) - BLOCK
    rel = k_loc[:, None] - q_loc[None, :]
    return _t5_bucket(rel), np.abs(rel) <= WINDOW


def _col(w_in, i):
    return w_in[:, IN_OFFSETS[i]:IN_OFFSETS[i + 1]]


def kernel(x, w_in, g_q, g_kv, w_uq, w_ukv, sink, rel_bias, w_out, ln_g, ln_b):
    B, S, _ = x.shape
    bf = jnp.bfloat16
    f32 = jnp.float32

    pos = jnp.arange(S, dtype=f32)
    inv_freq = ROPE_BASE ** (-jnp.arange(0, MLA_ROPE, 2, dtype=f32) / MLA_ROPE)
    ang_t = inv_freq[:, None] * pos[None, :]
    cos_t, sin_t = jnp.cos(ang_t), jnp.sin(ang_t)

    c_q, c_kv, k_rope, gate_a, q_s, k_s, v_s, gate_b = (_col(w_in, i) for i in range(8))
    w_tr = jnp.concatenate([c_q, q_s, v_s, k_rope], axis=1).T.astype(bf)
    w_nat = jnp.concatenate([c_kv, k_s], axis=1).astype(bf)
    w_ukv3 = w_ukv.reshape(KV_LORA, MLA_HEADS, MLA_NOPE + MLA_V)
    w_uk = w_ukv3[:, :, :MLA_NOPE].reshape(KV_LORA, MLA_HEADS * MLA_NOPE).astype(bf)
    w_uv_t = w_ukv3[:, :, MLA_NOPE:].reshape(KV_LORA, D_MLA).T.astype(bf)
    w_uq_t = w_uq.T.astype(bf)

    q_t, k, v_t, qs_t, ks, vs_t = _project(
        x, w_tr, w_nat, g_q.reshape(Q_LORA, 1), g_kv.reshape(1, KV_LORA),
        w_uq_t, w_uk, w_uv_t, cos_t, sin_t)

    o_a_t = _mla_attention(q_t, k, v_t)

    bucket, band = _band_geometry()
    bucket = jnp.asarray(np.where(band, bucket, -1).reshape(3, BLOCK, BLOCK), jnp.int32)
    bias_t = _bias_table(rel_bias.astype(f32), sink.astype(f32), bucket)

    o_b_t = _swa_attention(qs_t, ks, vs_t, bias_t)

    w_out_bf = w_out.astype(bf)
    return _out_project(x, o_a_t, o_b_t, gate_a.astype(bf), gate_b.astype(bf),
                        w_out_bf[:D_MLA], w_out_bf[D_MLA:],
                        ln_g.reshape(1, D_MODEL), ln_b.reshape(1, D_MODEL))
```

```python
import functools
import math

import jax
import jax.numpy as jnp
import numpy as np
from jax import lax
from jax.experimental import pallas as pl
from jax.experimental.pallas import tpu as pltpu

D_MODEL = 1024
MLA_HEADS = 8
MLA_NOPE = 64
MLA_ROPE = 32
MLA_V = 64
MLA_QK = MLA_NOPE + MLA_ROPE
Q_LORA = 256
KV_LORA = 128
D_MLA = MLA_HEADS * MLA_V
MLA_SCALE = 1.0 / math.sqrt(MLA_QK)
LOG2_E = math.log2(math.e)
MLA_Q_SCALE = MLA_SCALE * LOG2_E
BF16_SUBLANES = 16
MLA_V_ROWS = MLA_V + BF16_SUBLANES
ROPE_BASE = 10000.0

SWA_HEADS = 8
SWA_KV_HEADS = 2
SWA_HEAD_DIM = 64
SWA_GROUP = SWA_HEADS // SWA_KV_HEADS
D_SWA = SWA_HEADS * SWA_HEAD_DIM
WINDOW = 128
BLOCK = 128
SWA_SCALE = 1.0 / math.sqrt(SWA_HEAD_DIM)
SWA_Q_SCALE = SWA_SCALE * LOG2_E
SWA_V_ROWS = SWA_HEAD_DIM + BF16_SUBLANES
N_BUCKETS = 32
MAX_DISTANCE = 128

DEPTH = 1
ALPHA = (2.0 * DEPTH) ** 0.25

IN_SPLITS = (Q_LORA, KV_LORA, MLA_ROPE, D_MLA, D_SWA,
             SWA_KV_HEADS * SWA_HEAD_DIM, SWA_KV_HEADS * SWA_HEAD_DIM, D_SWA)
IN_OFFSETS = tuple(int(o) for o in np.cumsum((0,) + IN_SPLITS))

LANE = 128
SUBLANES = 8
QK_PAD = LANE
VMEM_LIMIT_BYTES = 56 * 1024 * 1024

PROJ_TOKENS = 1024
PROJ_PARTS = 4
MLA_TQ = 512
MLA_STEP_TILES = 8
MLA_TK = 256
MLA_AHEAD = 2
MLA_REFRESH = 8
MLA_MAX_EXCESS = 64.0
SWA_TOKENS = 4096
SWA_AHEAD = 1
SWA_MAX_EXCESS = 64.0
OUT_TOKENS = 1024
OUT_ROWS = 256

TR_CQ = 0
TR_QS = TR_CQ + Q_LORA
TR_VS = TR_QS + D_SWA
TR_KR = TR_VS + SWA_KV_HEADS * SWA_HEAD_DIM
TR_ROWS = TR_KR + MLA_ROPE
NAT_CKV = 0
NAT_KS = NAT_CKV + KV_LORA
NAT_COLS = NAT_KS + SWA_KV_HEADS * SWA_HEAD_DIM

_NT = (((1,), (1,)), ((), ()))
_TN = (((0,), (0,)), ((), ()))


def _rsqrt_mean_sq(x, axis, eps):
    return lax.rsqrt(jnp.mean(x * x, axis=axis, keepdims=True) + eps)


def _proj_kernel(x_ref, wtr_ref, wnat_ref, gq_ref, gkv_ref, wuqt_ref, wuk_ref, wuvt_ref,
                 cos_t_ref, sin_t_ref,
                 qt_ref, k_ref, vt_ref, qst_ref, ks_ref, vst_ref,
                 tr_ref, nat_ref):
    n_tok = x_ref.shape[1] // PROJ_PARTS
    half = MLA_ROPE // 2

    def project(p):
        tok = slice(p * n_tok, (p + 1) * n_tok)
        xb = x_ref[0, tok, :].astype(jnp.bfloat16)
        tr_ref[:, tok] = lax.dot_general(wtr_ref[...], xb, _NT, preferred_element_type=jnp.float32)
        nat_ref[tok, :] = jnp.dot(xb, wnat_ref[...], preferred_element_type=jnp.float32)

    project(0)
    for p in range(PROJ_PARTS):
        if p + 1 < PROJ_PARTS:
            project(p + 1)
        _proj_finish(p, n_tok, tr_ref, nat_ref, gq_ref, gkv_ref, wuqt_ref, wuk_ref, wuvt_ref,
                     cos_t_ref, sin_t_ref,
                     qt_ref, k_ref, vt_ref, qst_ref, ks_ref, vst_ref)


def _proj_finish(p, n_tok, tr_ref, nat_ref, gq_ref, gkv_ref, wuqt_ref, wuk_ref, wuvt_ref,
                 cos_t_ref, sin_t_ref,
                 qt_ref, k_ref, vt_ref, qst_ref, ks_ref, vst_ref):
    tok = slice(p * n_tok, (p + 1) * n_tok)
    half = MLA_ROPE // 2

    cq = tr_ref[TR_CQ:TR_CQ + Q_LORA, tok]
    cqn = (cq * _rsqrt_mean_sq(cq, 0, 1e-6)) * gq_ref[...]
    q_t = jnp.dot(wuqt_ref[...], cqn.astype(jnp.bfloat16),
                  preferred_element_type=jnp.float32)
    cos_t = cos_t_ref[:, tok]
    sin_t = sin_t_ref[:, tok]
    zeros_pad = jnp.zeros((QK_PAD - MLA_QK, n_tok), jnp.bfloat16)
    for h in range(MLA_HEADS):
        base = h * MLA_QK
        nope = q_t[base:base + MLA_NOPE]
        r1 = q_t[base + MLA_NOPE:base + MLA_NOPE + half]
        r2 = q_t[base + MLA_NOPE + half:base + MLA_QK]
        qt_ref[0, h, 0:MLA_NOPE, tok] = (nope * MLA_Q_SCALE).astype(jnp.bfloat16)
        qt_ref[0, h, MLA_NOPE:MLA_NOPE + half, tok] = (
            (r1 * cos_t - r2 * sin_t) * MLA_Q_SCALE).astype(jnp.bfloat16)
        qt_ref[0, h, MLA_NOPE + half:MLA_QK, tok] = (
            (r2 * cos_t + r1 * sin_t) * MLA_Q_SCALE).astype(jnp.bfloat16)
        qt_ref[0, h, MLA_QK:QK_PAD, tok] = zeros_pad

    ckv = nat_ref[tok, NAT_CKV:NAT_CKV + KV_LORA]
    kvn = ((ckv * _rsqrt_mean_sq(ckv, 1, 1e-6)) * gkv_ref[...]).astype(jnp.bfloat16)
    k_nope = jnp.dot(kvn, wuk_ref[...], preferred_element_type=jnp.float32)
    v_t = lax.dot_general(wuvt_ref[...], kvn, _NT, preferred_element_type=jnp.float32)
    kr1 = tr_ref[TR_KR:TR_KR + half, tok]
    kr2 = tr_ref[TR_KR + half:TR_KR + MLA_ROPE, tok]
    k_rope = jnp.concatenate(
        [jnp.zeros((MLA_NOPE, n_tok), jnp.float32),
         kr1 * cos_t - kr2 * sin_t,
         kr2 * cos_t + kr1 * sin_t,
         jnp.zeros((QK_PAD - MLA_QK, n_tok), jnp.float32)], axis=0).T
    lane = lax.broadcasted_iota(jnp.int32, k_rope.shape, 1)
    row_id = lax.broadcasted_iota(jnp.int32, (BF16_SUBLANES, n_tok), 0)
    ones_row = jnp.where(row_id == 0, 1.0, 0.0).astype(jnp.bfloat16)
    for h in range(MLA_HEADS):
        pair = k_nope[:, (h // 2) * LANE:(h // 2 + 1) * LANE]
        if h % 2:
            pair = pltpu.roll(pair, MLA_NOPE, axis=1)
        k_ref[0, h, tok, :] = jnp.where(lane < MLA_NOPE, pair, k_rope).astype(jnp.bfloat16)
        vt_ref[0, h, 0:MLA_V, tok] = v_t[h * MLA_V:(h + 1) * MLA_V].astype(jnp.bfloat16)
        vt_ref[0, h, MLA_V:MLA_V_ROWS, tok] = ones_row

    n_blocks = n_tok // BLOCK
    ks_ref[0, tok, :] = nat_ref[tok, NAT_KS:NAT_KS + SWA_KV_HEADS * SWA_HEAD_DIM].astype(jnp.bfloat16)
    for g in range(SWA_KV_HEADS):
        vst_ref[0, g, 0:SWA_HEAD_DIM, tok] = tr_ref[TR_VS + g * SWA_HEAD_DIM:
                                                    TR_VS + (g + 1) * SWA_HEAD_DIM, tok].astype(jnp.bfloat16)
        vst_ref[0, g, SWA_HEAD_DIM:SWA_V_ROWS, tok] = ones_row
        for hh in range(SWA_GROUP):
            row = TR_QS + (g * SWA_GROUP + hh) * SWA_HEAD_DIM
            q_h = (tr_ref[row:row + SWA_HEAD_DIM, tok] * SWA_Q_SCALE).astype(jnp.bfloat16)
            for nb in range(n_blocks):
                col = ((p * n_blocks + nb) * SWA_GROUP + hh) * BLOCK
                qst_ref[0, g, :, col:col + BLOCK] = q_h[:, nb * BLOCK:(nb + 1) * BLOCK]


def _project(x, w_tr, w_nat, g_q, g_kv, w_uq_t, w_uk, w_uv_t, cos_t, sin_t):
    B, S, _ = x.shape
    T = PROJ_TOKENS
    grid = (B, S // T)
    const = lambda *shape: pl.BlockSpec(shape, lambda b, t: (0,) * len(shape))
    bf = jnp.bfloat16
    out_shape = (
        jax.ShapeDtypeStruct((B, MLA_HEADS, QK_PAD, S), bf),
        jax.ShapeDtypeStruct((B, MLA_HEADS, S, QK_PAD), bf),
        jax.ShapeDtypeStruct((B, MLA_HEADS, MLA_V_ROWS, S), bf),
        jax.ShapeDtypeStruct((B, SWA_KV_HEADS, SWA_HEAD_DIM, SWA_GROUP * S), bf),
        jax.ShapeDtypeStruct((B, S, SWA_KV_HEADS * SWA_HEAD_DIM), bf),
        jax.ShapeDtypeStruct((B, SWA_KV_HEADS, SWA_V_ROWS, S), bf),
    )
    out_specs = (
        pl.BlockSpec((1, MLA_HEADS, QK_PAD, T), lambda b, t: (b, 0, 0, t)),
        pl.BlockSpec((1, MLA_HEADS, T, QK_PAD), lambda b, t: (b, 0, t, 0)),
        pl.BlockSpec((1, MLA_HEADS, MLA_V_ROWS, T), lambda b, t: (b, 0, 0, t)),
        pl.BlockSpec((1, SWA_KV_HEADS, SWA_HEAD_DIM, SWA_GROUP * T), lambda b, t: (b, 0, 0, t)),
        pl.BlockSpec((1, T, SWA_KV_HEADS * SWA_HEAD_DIM), lambda b, t: (b, t, 0)),
        pl.BlockSpec((1, SWA_KV_HEADS, SWA_V_ROWS, T), lambda b, t: (b, 0, 0, t)),
    )
    in_specs = [
        pl.BlockSpec((1, T, D_MODEL), lambda b, t: (b, t, 0)),
        const(TR_ROWS, D_MODEL),
        const(D_MODEL, NAT_COLS),
        const(Q_LORA, 1),
        const(1, KV_LORA),
        const(MLA_HEADS * MLA_QK, Q_LORA),
        const(KV_LORA, MLA_HEADS * MLA_NOPE),
        const(D_MLA, KV_LORA),
        pl.BlockSpec((MLA_ROPE // 2, T), lambda b, t: (0, t)),
        pl.BlockSpec((MLA_ROPE // 2, T), lambda b, t: (0, t)),
    ]
    return pl.pallas_call(
        _proj_kernel,
        grid=grid,
        in_specs=in_specs,
        out_specs=out_specs,
        out_shape=out_shape,
        scratch_shapes=[pltpu.VMEM((TR_ROWS, T), jnp.float32),
                        pltpu.VMEM((T, NAT_COLS), jnp.float32)],
        compiler_params=pltpu.CompilerParams(
            dimension_semantics=("arbitrary", "arbitrary"),
            vmem_limit_bytes=VMEM_LIMIT_BYTES),
        name="proj",
    )(x, w_tr, w_nat, g_q, g_kv, w_uq_t, w_uk, w_uv_t, cos_t, sin_t)


def _mla_kernel(qt_ref, k_ref, vt_ref, o_ref, m_ref, acc_ref):
    n_chunks = k_ref.shape[2] // MLA_TK
    n_tiles = qt_ref.shape[3] // MLA_TQ

    def q_tile(t):
        return qt_ref[0, 0, :, t * MLA_TQ:(t + 1) * MLA_TQ]

    def chunk_slice(c):
        start = c * MLA_TK
        return pl.ds(start if isinstance(c, int) else pl.multiple_of(start, MLA_TK), MLA_TK)

    def key_chunk(c):
        return k_ref[0, 0, chunk_slice(c), :]

    def value_chunk(c):
        return vt_ref[0, 0, :, chunk_slice(c)]

    items = [(t, c) for t in range(n_tiles) for c in range(n_chunks)]
    m_use = [None] * n_tiles
    seen = [None] * n_tiles
    acc = [None] * n_tiles
    den = [None] * n_tiles
    excess = None

    def issue_scores(i):
        t, c = items[i]
        if c == 0:
            s0 = jnp.dot(k_ref[0, 0, 0:BF16_SUBLANES, :], q_tile(t),
                         preferred_element_type=jnp.float32)
            m_use[t] = jnp.max(s0, axis=0, keepdims=True)
        return jnp.dot(key_chunk(c), q_tile(t), preferred_element_type=jnp.float32)

    pending = [issue_scores(i) for i in range(MLA_AHEAD)]
    for i, (t, c) in enumerate(items):
        if i + MLA_AHEAD < len(items):
            pending.append(issue_scores(i + MLA_AHEAD))
        s = pending.pop(0)
        p = jnp.exp2(s - m_use[t])
        psum = jnp.sum(p.reshape(MLA_TK // SUBLANES, SUBLANES, MLA_TQ), axis=0)
        den[t] = psum if c == 0 else den[t] + psum
        p = p.astype(jnp.bfloat16)
        pv = jnp.dot(vt_ref[0, 0, 0:MLA_V, chunk_slice(c)], p,
                     preferred_element_type=jnp.float32)
        acc[t] = pv if c == 0 else acc[t] + pv
        pmax = jnp.max(p.reshape(MLA_TK // BF16_SUBLANES, BF16_SUBLANES, MLA_TQ), axis=0)
        seen[t] = pmax if seen[t] is None else jnp.maximum(seen[t], pmax)
        last = c + 1 == n_chunks
        if last or c == 0 or (c + 1) % MLA_REFRESH == 0:
            excess = seen[t] if excess is None else jnp.maximum(excess, seen[t])
            if last:
                acc_ref[t, 0:MLA_V, :] = acc[t]
                acc_ref[t, MLA_V:MLA_V + 1, :] = jnp.sum(den[t], axis=0, keepdims=True)
            else:
                top = jnp.max(seen[t].astype(jnp.float32), axis=0, keepdims=True)
                rise = jnp.maximum(jnp.log2(top), 0.0)
                alpha = jnp.exp2(-rise)
                acc[t] = acc[t] * alpha
                den[t] = den[t] * alpha
                m_use[t] = m_use[t] + rise
                seen[t] = None

    @pl.when(jnp.max(excess.astype(jnp.float32)) > 2.0 ** MLA_MAX_EXCESS)
    def _():
        for t in range(n_tiles):
            m_ref[...] = jnp.full(m_ref.shape, -jnp.inf, jnp.float32)
            acc_ref[t] = jnp.zeros(acc_ref.shape[1:], jnp.float32)

            def chunk(c, carry, t=t):
                s = jnp.dot(key_chunk(c), q_tile(t), preferred_element_type=jnp.float32)
                m_prev = m_ref[...]
                m_new = jnp.maximum(m_prev, jnp.max(s, axis=0, keepdims=True))
                p = jnp.exp2(s - m_new).astype(jnp.bfloat16)
                acc_ref[t] = jnp.exp2(m_prev - m_new) * acc_ref[t] + jnp.dot(
                    value_chunk(c), p, preferred_element_type=jnp.float32)
                m_ref[...] = m_new
                return carry

            lax.fori_loop(0, n_chunks, chunk, 0)

    for t in range(n_tiles):
        cols = slice(t * MLA_TQ, (t + 1) * MLA_TQ)
        o = acc_ref[t, 0:MLA_V, :] / acc_ref[t, MLA_V:MLA_V + 1, :]
        o_ref[0, :, cols] = o.astype(o_ref.dtype)


def _mla_attention(q_t, k, v_t):
    B, H, _, S = q_t.shape
    TQ = MLA_TQ
    step_q = MLA_STEP_TILES * TQ
    grid = (B, H, S // step_q)
    return pl.pallas_call(
        _mla_kernel,
        grid=grid,
        in_specs=[
            pl.BlockSpec((1, 1, QK_PAD, step_q), lambda b, h, i: (b, h, 0, i)),
            pl.BlockSpec((1, 1, S, QK_PAD), lambda b, h, i: (b, h, 0, 0)),
            pl.BlockSpec((1, 1, MLA_V_ROWS, S), lambda b, h, i: (b, h, 0, 0)),
        ],
        out_specs=pl.BlockSpec((1, MLA_V, step_q), lambda b, h, i: (b, h, i)),
        out_shape=jax.ShapeDtypeStruct((B, D_MLA, S), jnp.bfloat16),
        scratch_shapes=[pltpu.VMEM((1, TQ), jnp.float32),
                        pltpu.VMEM((MLA_STEP_TILES, MLA_V_ROWS, TQ), jnp.float32)],
        compiler_params=pltpu.CompilerParams(
            dimension_semantics=("arbitrary", "arbitrary", "arbitrary"),
            vmem_limit_bytes=VMEM_LIMIT_BYTES),
        name="mla",
    )(q_t, k, v_t)


def _bias_kernel(rel_ref, sink_ref, bucket_ref, o_ref):
    g = pl.program_id(0)
    for part in range(3):
        bucket = bucket_ref[part]
        for hh in range(SWA_GROUP):
            head = g * SWA_GROUP + hh
            tile = jnp.full((BLOCK, BLOCK), -jnp.inf, jnp.float32)
            for b in range(N_BUCKETS):
                tile = jnp.where(bucket == b, (rel_ref[b, head] - sink_ref[head]) * LOG2_E, tile)
            o_ref[0, part, :, hh * BLOCK:(hh + 1) * BLOCK] = tile


def _bias_table(rel_bias, sink, bucket):
    return pl.pallas_call(
        _bias_kernel,
        grid=(SWA_KV_HEADS,),
        in_specs=[pl.BlockSpec(memory_space=pltpu.SMEM),
                  pl.BlockSpec(memory_space=pltpu.SMEM),
                  pl.BlockSpec((3, BLOCK, BLOCK), lambda g: (0, 0, 0))],
        out_specs=pl.BlockSpec((1, 3, BLOCK, SWA_GROUP * BLOCK), lambda g: (g, 0, 0, 0)),
        out_shape=jax.ShapeDtypeStruct((SWA_KV_HEADS, 3, BLOCK, SWA_GROUP * BLOCK), jnp.float32),
        compiler_params=pltpu.CompilerParams(dimension_semantics=("arbitrary",)),
        name="bias",
    )(rel_bias, sink, bucket)


def _swa_kernel(qst_ref, ks_ref, vst_ref, bias_ref, o_ref):
    n_total = ks_ref.shape[1] // BLOCK
    kv_head = pl.program_id(1)
    n_local = o_ref.shape[2] // BLOCK
    step = pl.program_id(2)
    n_steps = pl.num_programs(2)
    neg_inf = jnp.float32(-jnp.inf)

    def band_starts(j):
        n = step * n_local + j
        return tuple(pl.multiple_of(b * BLOCK, BLOCK)
                     for b in (jnp.maximum(n - 1, 0), n, jnp.minimum(n + 1, n_total - 1)))

    def lane_block(j, width):
        start = j * width
        return pl.ds(start if isinstance(j, int) else pl.multiple_of(start, width), width)

    def band_scores(j):
        q_t = qst_ref[0, 0, :, lane_block(j, SWA_GROUP * BLOCK)]
        zeros = jnp.zeros_like(q_t)
        q_t = jnp.where(kv_head == 0, jnp.concatenate([q_t, zeros], axis=0),
                        jnp.concatenate([zeros, q_t], axis=0))
        n = step * n_local + j
        penalties = [None, None, None]
        if not isinstance(j, int) or j == 0:
            penalties[0] = jnp.where(n > 0, 0.0, neg_inf)
        if not isinstance(j, int) or j == n_local - 1:
            penalties[2] = jnp.where(n < n_total - 1, 0.0, neg_inf)
        scores = []
        for part, start in enumerate(band_starts(j)):
            k = ks_ref[0, pl.ds(start, BLOCK), :]
            s = jnp.dot(k, q_t, preferred_element_type=jnp.float32) + bias_ref[0, part]
            if penalties[part] is not None:
                s = s + penalties[part]
            scores.append(s)
        return scores

    def weighted_values(j, weights, n_rows):
        acc = None
        for part, start in enumerate(band_starts(j)):
            v_t = vst_ref[0, 0, 0:n_rows, pl.ds(start, BLOCK)]
            pv = jnp.dot(v_t, weights[part], preferred_element_type=jnp.float32)
            acc = pv if acc is None else acc + pv
        return acc

    def write_block(j, values, denom):
        o = values / denom
        cols = lane_block(j, BLOCK)
        for hh in range(SWA_GROUP):
            rows = slice(hh * SWA_HEAD_DIM, (hh + 1) * SWA_HEAD_DIM)
            o_ref[0, rows, cols] = o[:, hh * BLOCK:(hh + 1) * BLOCK].astype(o_ref.dtype)

    excess = None
    pending = [band_scores(j) for j in range(SWA_AHEAD)]
    for j in range(n_local):
        if j + SWA_AHEAD < n_local:
            pending.append(band_scores(j + SWA_AHEAD))
        weights, den = [], None
        for s in pending.pop(0):
            w = jnp.exp2(s)
            wsum = jnp.sum(w.reshape(BLOCK // SUBLANES, SUBLANES, w.shape[1]), axis=0)
            den = wsum if den is None else den + wsum
            w = w.astype(jnp.bfloat16)
            wmax = jnp.max(w.reshape(BLOCK // BF16_SUBLANES, BF16_SUBLANES, w.shape[1]), axis=0)
            excess = wmax if excess is None else jnp.maximum(excess, wmax)
            weights.append(w)
        write_block(j, weighted_values(j, weights, SWA_HEAD_DIM),
                    jnp.sum(den, axis=0, keepdims=True) + 1.0)

    @pl.when(jnp.max(excess.astype(jnp.float32)) > 2.0 ** SWA_MAX_EXCESS)
    def _():
        def exact_block(j, carry):
            scores = band_scores(j)
            m = jnp.zeros((1, scores[0].shape[1]), jnp.float32)
            for s in scores:
                m = jnp.maximum(m, jnp.max(s, axis=0, keepdims=True))
            acc = weighted_values(j, [jnp.exp2(s - m).astype(jnp.bfloat16) for s in scores],
                                  SWA_V_ROWS)
            write_block(j, acc[0:SWA_HEAD_DIM], acc[SWA_HEAD_DIM:SWA_HEAD_DIM + 1] + jnp.exp2(-m))
            return carry

        lax.fori_loop(0, n_local, exact_block, 0)


def _swa_attention(qs_t, ks, vs_t, bias_t):
    B, G, _, S = vs_t.shape
    T = SWA_TOKENS
    grid = (B, G, S // T)
    rows = SWA_GROUP * SWA_HEAD_DIM
    return pl.pallas_call(
        _swa_kernel,
        grid=grid,
        in_specs=[
            pl.BlockSpec((1, 1, SWA_HEAD_DIM, SWA_GROUP * T), lambda b, g, t: (b, g, 0, t)),
            pl.BlockSpec((1, S, SWA_KV_HEADS * SWA_HEAD_DIM), lambda b, g, t: (b, 0, 0)),
            pl.BlockSpec((1, 1, SWA_V_ROWS, S), lambda b, g, t: (b, g, 0, 0)),
            pl.BlockSpec((1, 3, BLOCK, SWA_GROUP * BLOCK), lambda b, g, t: (g, 0, 0, 0)),
        ],
        out_specs=pl.BlockSpec((1, rows, T), lambda b, g, t: (b, g, t)),
        out_shape=jax.ShapeDtypeStruct((B, D_SWA, S), jnp.bfloat16),
        compiler_params=pltpu.CompilerParams(
            dimension_semantics=("arbitrary", "arbitrary", "arbitrary"),
            vmem_limit_bytes=VMEM_LIMIT_BYTES),
        name="swa",
    )(qs_t, ks, vs_t, bias_t)


def _out_kernel(x_ref, oa_ref, ob_ref, wga_ref, wgb_ref, wa_ref, wb_ref, g_ref, b_ref, y_ref):
    n_chunks = x_ref.shape[1] // OUT_ROWS

    def gate_paths(c):
        xb = x_ref[0, c * OUT_ROWS:(c + 1) * OUT_ROWS, :].astype(jnp.bfloat16)
        return [jnp.dot(xb, wg_ref[...], preferred_element_type=jnp.float32)
                for wg_ref in (wga_ref, wgb_ref)]

    def project(c, gates):
        toks = slice(c * OUT_ROWS, (c + 1) * OUT_ROWS)
        out = None
        for o_ref_, gate, w_ref in ((oa_ref, gates[0], wa_ref), (ob_ref, gates[1], wb_ref)):
            o_nat = o_ref_[0, :, toks].astype(jnp.float32).T
            mixed = (o_nat * jax.nn.silu(gate)).astype(jnp.bfloat16)
            part = jnp.dot(mixed, w_ref[...], preferred_element_type=jnp.float32)
            out = part if out is None else out + part
        return out

    def residual_norm(c, out):
        rows = slice(c * OUT_ROWS, (c + 1) * OUT_ROWS)
        h = ALPHA * x_ref[0, rows, :] + out
        mu = jnp.mean(h, axis=-1, keepdims=True)
        var = jnp.mean(jnp.square(h - mu), axis=-1, keepdims=True)
        y_ref[0, rows, :] = ((h - mu) * lax.rsqrt(var + 1e-5)) * g_ref[...] + b_ref[...]

    gates_next = gate_paths(0)
    out_prev = None
    for c in range(n_chunks):
        gates = gates_next
        if c + 1 < n_chunks:
            gates_next = gate_paths(c + 1)
        out = project(c, gates)
        if out_prev is not None:
            residual_norm(c - 1, out_prev)
        out_prev = out
    residual_norm(n_chunks - 1, out_prev)


def _out_project(x, o_a_t, o_b_t, w_gate_a, w_gate_b, w_a, w_b, ln_g, ln_b):
    B, S, D = x.shape
    T = OUT_TOKENS
    grid = (B, S // T)
    const = lambda *shape: pl.BlockSpec(shape, lambda b, t: (0,) * len(shape))
    return pl.pallas_call(
        _out_kernel,
        grid=grid,
        in_specs=[
            pl.BlockSpec((1, T, D), lambda b, t: (b, t, 0)),
            pl.BlockSpec((1, D_MLA, T), lambda b, t: (b, 0, t)),
            pl.BlockSpec((1, D_SWA, T), lambda b, t: (b, 0, t)),
            const(D, D_MLA),
            const(D, D_SWA),
            const(D_MLA, D),
            const(D_SWA, D),
            const(1, D),
            const(1, D),
        ],
        out_specs=pl.BlockSpec((1, T, D), lambda b, t: (b, t, 0)),
        out_shape=jax.ShapeDtypeStruct((B, S, D), x.dtype),
        compiler_params=pltpu.CompilerParams(
            dimension_semantics=("arbitrary", "arbitrary"),
            vmem_limit_bytes=VMEM_LIMIT_BYTES),
        name="outproj",
    )(x, o_a_t, o_b_t, w_gate_a, w_gate_b, w_a, w_b, ln_g, ln_b)


def _t5_bucket(rel):
    half = N_BUCKETS // 2
    ret = np.where(rel > 0, half, 0)
    n = np.abs(rel)
    max_exact = half // 2
    large = max_exact + (np.log(np.maximum(n, 1).astype(np.float32) / max_exact)
                         / np.log(MAX_DISTANCE / max_exact) * (half - max_exact)).astype(np.int32)
    large = np.minimum(large, half - 1)
    return (ret + np.where(n < max_exact, n, large)).astype(np.int32)


def _band_geometry():
    q_loc = np.arange(BLOCK)
    k_loc = np.arange(3 * BLOCK) - BLOCK
    rel = k_loc[:, None] - q_loc[None, :]
    return _t5_bucket(rel), np.abs(rel) <= WINDOW


def _col(w_in, i):
    return w_in[:, IN_OFFSETS[i]:IN_OFFSETS[i + 1]]


def kernel(x, w_in, g_q, g_kv, w_uq, w_ukv, sink, rel_bias, w_out, ln_g, ln_b):
    B, S, _ = x.shape
    bf = jnp.bfloat16
    f32 = jnp.float32

    pos = jnp.arange(S, dtype=f32)
    inv_freq = ROPE_BASE ** (-jnp.arange(0, MLA_ROPE, 2, dtype=f32) / MLA_ROPE)
    ang_t = inv_freq[:, None] * pos[None, :]
    cos_t, sin_t = jnp.cos(ang_t), jnp.sin(ang_t)

    c_q, c_kv, k_rope, gate_a, q_s, k_s, v_s, gate_b = (_col(w_in, i) for i in range(8))
    w_tr = jnp.concatenate([c_q, q_s, v_s, k_rope], axis=1).T.astype(bf)
    w_nat = jnp.concatenate([c_kv, k_s], axis=1).astype(bf)
    w_ukv3 = w_ukv.reshape(KV_LORA, MLA_HEADS, MLA_NOPE + MLA_V)
    w_uk = w_ukv3[:, :, :MLA_NOPE].reshape(KV_LORA, MLA_HEADS * MLA_NOPE).astype(bf)
    w_uv_t = w_ukv3[:, :, MLA_NOPE:].reshape(KV_LORA, D_MLA).T.astype(bf)
    w_uq_t = w_uq.T.astype(bf)

    q_t, k, v_t, qs_t, ks, vs_t = _project(
        x, w_tr, w_nat, g_q.reshape(Q_LORA, 1), g_kv.reshape(1, KV_LORA),
        w_uq_t, w_uk, w_uv_t, cos_t, sin_t)

    o_a_t = _mla_attention(q_t, k, v_t)

    bucket, band = _band_geometry()
    bucket = jnp.asarray(np.where(band, bucket, -1).reshape(3, BLOCK, BLOCK), jnp.int32)
    bias_t = _bias_table(rel_bias.astype(f32), sink.astype(f32), bucket)

    o_b_t = _swa_attention(qs_t, ks, vs_t, bias_t)

    w_out_bf = w_out.astype(bf)
    return _out_project(x, o_a_t, o_b_t, gate_a.astype(bf), gate_b.astype(bf),
                        w_out_bf[:D_MLA], w_out_bf[D_MLA:],
                        ln_g.reshape(1, D_MODEL), ln_b.reshape(1, D_MODEL))
```

```python
import functools
import math

import jax
import jax.numpy as jnp
import numpy as np
from jax import lax
from jax.experimental import pallas as pl
from jax.experimental.pallas import tpu as pltpu

D_MODEL = 1024
MLA_HEADS = 8
MLA_NOPE = 64
MLA_ROPE = 32
MLA_V = 64
MLA_QK = MLA_NOPE + MLA_ROPE
Q_LORA = 256
KV_LORA = 128
D_MLA = MLA_HEADS * MLA_V
MLA_SCALE = 1.0 / math.sqrt(MLA_QK)
LOG2_E = math.log2(math.e)
MLA_Q_SCALE = MLA_SCALE * LOG2_E
BF16_SUBLANES = 16
MLA_V_ROWS = MLA_V + BF16_SUBLANES
ROPE_BASE = 10000.0

SWA_HEADS = 8
SWA_KV_HEADS = 2
SWA_HEAD_DIM = 64
SWA_GROUP = SWA_HEADS // SWA_KV_HEADS
D_SWA = SWA_HEADS * SWA_HEAD_DIM
WINDOW = 128
BLOCK = 128
SWA_SCALE = 1.0 / math.sqrt(SWA_HEAD_DIM)
SWA_Q_SCALE = SWA_SCALE * LOG2_E
SWA_V_ROWS = SWA_HEAD_DIM + BF16_SUBLANES
N_BUCKETS = 32
MAX_DISTANCE = 128

DEPTH = 1
ALPHA = (2.0 * DEPTH) ** 0.25

IN_SPLITS = (Q_LORA, KV_LORA, MLA_ROPE, D_MLA, D_SWA,
             SWA_KV_HEADS * SWA_HEAD_DIM, SWA_KV_HEADS * SWA_HEAD_DIM, D_SWA)
IN_OFFSETS = tuple(int(o) for o in np.cumsum((0,) + IN_SPLITS))

LANE = 128
SUBLANES = 8
QK_PAD = LANE
VMEM_LIMIT_BYTES = 56 * 1024 * 1024

PROJ_TOKENS = 1024
X_RING = 3
PROJ_PARTS = 4
MLA_TQ = 512
MLA_STEP_TILES = 4
MLA_TK = 256
MLA_AHEAD = 2
MLA_REFRESH = 8
MLA_MAX_EXCESS = 64.0
SWA_TOKENS = 4096
SWA_AHEAD = 1
SWA_MAX_EXCESS = 64.0
OUT_TOKENS = 1024
OUT_ROWS = 256

TR_CQ = 0
TR_QS = TR_CQ + Q_LORA
TR_VS = TR_QS + D_SWA
TR_KR = TR_VS + SWA_KV_HEADS * SWA_HEAD_DIM
TR_ROWS = TR_KR + MLA_ROPE
NAT_CKV = 0
NAT_KS = NAT_CKV + KV_LORA
NAT_COLS = NAT_KS + SWA_KV_HEADS * SWA_HEAD_DIM

_NT = (((1,), (1,)), ((), ()))
_TN = (((0,), (0,)), ((), ()))


def _rsqrt_mean_sq(x, axis, eps):
    return lax.rsqrt(jnp.mean(x * x, axis=axis, keepdims=True) + eps)


def _x_ring_tile(x_hbm, xbuf, xsem):
    n_t = pl.num_programs(1)
    total = pl.num_programs(0) * n_t
    step = pl.program_id(0) * n_t + pl.program_id(1)
    tile = xbuf.shape[1]

    def x_copy(m):
        slot = lax.rem(m, X_RING)
        rows = pl.ds(pl.multiple_of(lax.rem(m, n_t) * tile, tile), tile)
        return pltpu.make_async_copy(x_hbm.at[lax.div(m, n_t), rows, :], xbuf.at[slot], xsem.at[slot])

    @pl.when(step == 0)
    def _():
        for m in range(X_RING - 1):
            x_copy(jnp.int32(m)).start()

    @pl.when(step + (X_RING - 1) < total)
    def _():
        x_copy(step + (X_RING - 1)).start()

    x_copy(step).wait()
    return xbuf.at[lax.rem(step, X_RING)]


def _proj_kernel(x_hbm, wtr_ref, wnat_ref, gq_ref, gkv_ref, wuqt_ref, wuk_ref, wuvt_ref,
                 cos_t_ref, sin_t_ref,
                 qt_ref, k_ref, vt_ref, qst_ref, ks_ref, vst_ref,
                 tr_ref, nat_ref, xbuf, xsem):
    x_ref = _x_ring_tile(x_hbm, xbuf, xsem)
    n_tok = x_ref.shape[0] // PROJ_PARTS
    half = MLA_ROPE // 2

    def project(p):
        tok = slice(p * n_tok, (p + 1) * n_tok)
        xb = x_ref[tok, :].astype(jnp.bfloat16)
        tr_ref[:, tok] = lax.dot_general(wtr_ref[...], xb, _NT, preferred_element_type=jnp.float32)
        nat_ref[tok, :] = jnp.dot(xb, wnat_ref[...], preferred_element_type=jnp.float32)

    project(0)
    for p in range(PROJ_PARTS):
        if p + 1 < PROJ_PARTS:
            project(p + 1)
        _proj_finish(p, n_tok, tr_ref, nat_ref, gq_ref, gkv_ref, wuqt_ref, wuk_ref, wuvt_ref,
                     cos_t_ref, sin_t_ref,
                     qt_ref, k_ref, vt_ref, qst_ref, ks_ref, vst_ref)


def _proj_finish(p, n_tok, tr_ref, nat_ref, gq_ref, gkv_ref, wuqt_ref, wuk_ref, wuvt_ref,
                 cos_t_ref, sin_t_ref,
                 qt_ref, k_ref, vt_ref, qst_ref, ks_ref, vst_ref):
    tok = slice(p * n_tok, (p + 1) * n_tok)
    half = MLA_ROPE // 2

    cq = tr_ref[TR_CQ:TR_CQ + Q_LORA, tok]
    cqn = (cq * _rsqrt_mean_sq(cq, 0, 1e-6)) * gq_ref[...]
    q_t = jnp.dot(wuqt_ref[...], cqn.astype(jnp.bfloat16),
                  preferred_element_type=jnp.float32)
    cos_t = cos_t_ref[:, tok]
    sin_t = sin_t_ref[:, tok]
    zeros_pad = jnp.zeros((QK_PAD - MLA_QK, n_tok), jnp.bfloat16)
    for h in range(MLA_HEADS):
        base = h * MLA_QK
        nope = q_t[base:base + MLA_NOPE]
        r1 = q_t[base + MLA_NOPE:base + MLA_NOPE + half]
        r2 = q_t[base + MLA_NOPE + half:base + MLA_QK]
        qt_ref[0, h, 0:MLA_NOPE, tok] = (nope * MLA_Q_SCALE).astype(jnp.bfloat16)
        qt_ref[0, h, MLA_NOPE:MLA_NOPE + half, tok] = (
            (r1 * cos_t - r2 * sin_t) * MLA_Q_SCALE).astype(jnp.bfloat16)
        qt_ref[0, h, MLA_NOPE + half:MLA_QK, tok] = (
            (r2 * cos_t + r1 * sin_t) * MLA_Q_SCALE).astype(jnp.bfloat16)
        qt_ref[0, h, MLA_QK:QK_PAD, tok] = zeros_pad

    ckv = nat_ref[tok, NAT_CKV:NAT_CKV + KV_LORA]
    kvn = ((ckv * _rsqrt_mean_sq(ckv, 1, 1e-6)) * gkv_ref[...]).astype(jnp.bfloat16)
    k_nope = jnp.dot(kvn, wuk_ref[...], preferred_element_type=jnp.float32)
    v_t = lax.dot_general(wuvt_ref[...], kvn, _NT, preferred_element_type=jnp.float32)
    kr1 = tr_ref[TR_KR:TR_KR + half, tok]
    kr2 = tr_ref[TR_KR + half:TR_KR + MLA_ROPE, tok]
    k_rope = jnp.concatenate(
        [jnp.zeros((MLA_NOPE, n_tok), jnp.float32),
         kr1 * cos_t - kr2 * sin_t,
         kr2 * cos_t + kr1 * sin_t,
         jnp.zeros((QK_PAD - MLA_QK, n_tok), jnp.float32)], axis=0).T
    lane = lax.broadcasted_iota(jnp.int32, k_rope.shape, 1)
    row_id = lax.broadcasted_iota(jnp.int32, (BF16_SUBLANES, n_tok), 0)
    ones_row = jnp.where(row_id == 0, 1.0, 0.0).astype(jnp.bfloat16)
    for h in range(MLA_HEADS):
        pair = k_nope[:, (h // 2) * LANE:(h // 2 + 1) * LANE]
        if h % 2:
            pair = pltpu.roll(pair, MLA_NOPE, axis=1)
        k_ref[0, h, tok, :] = jnp.where(lane < MLA_NOPE, pair, k_rope).astype(jnp.bfloat16)
        vt_ref[0, h, 0:MLA_V, tok] = v_t[h * MLA_V:(h + 1) * MLA_V].astype(jnp.bfloat16)
        vt_ref[0, h, MLA_V:MLA_V_ROWS, tok] = ones_row

    n_blocks = n_tok // BLOCK
    ks_ref[0, tok, :] = nat_ref[tok, NAT_KS:NAT_KS + SWA_KV_HEADS * SWA_HEAD_DIM].astype(jnp.bfloat16)
    for g in range(SWA_KV_HEADS):
        vst_ref[0, g, 0:SWA_HEAD_DIM, tok] = tr_ref[TR_VS + g * SWA_HEAD_DIM:
                                                    TR_VS + (g + 1) * SWA_HEAD_DIM, tok].astype(jnp.bfloat16)
        vst_ref[0, g, SWA_HEAD_DIM:SWA_V_ROWS, tok] = ones_row
        for hh in range(SWA_GROUP):
            row = TR_QS + (g * SWA_GROUP + hh) * SWA_HEAD_DIM
            q_h = (tr_ref[row:row + SWA_HEAD_DIM, tok] * SWA_Q_SCALE).astype(jnp.bfloat16)
            for nb in range(n_blocks):
                col = ((p * n_blocks + nb) * SWA_GROUP + hh) * BLOCK
                qst_ref[0, g, :, col:col + BLOCK] = q_h[:, nb * BLOCK:(nb + 1) * BLOCK]


def _project(x, w_tr, w_nat, g_q, g_kv, w_uq_t, w_uk, w_uv_t, cos_t, sin_t):
    B, S, _ = x.shape
    T = PROJ_TOKENS
    grid = (B, S // T)
    const = lambda *shape: pl.BlockSpec(shape, lambda b, t: (0,) * len(shape))
    bf = jnp.bfloat16
    out_shape = (
        jax.ShapeDtypeStruct((B, MLA_HEADS, QK_PAD, S), bf),
        jax.ShapeDtypeStruct((B, MLA_HEADS, S, QK_PAD), bf),
        jax.ShapeDtypeStruct((B, MLA_HEADS, MLA_V_ROWS, S), bf),
        jax.ShapeDtypeStruct((B, SWA_KV_HEADS, SWA_HEAD_DIM, SWA_GROUP * S), bf),
        jax.ShapeDtypeStruct((B, S, SWA_KV_HEADS * SWA_HEAD_DIM), bf),
        jax.ShapeDtypeStruct((B, SWA_KV_HEADS, SWA_V_ROWS, S), bf),
    )
    out_specs = (
        pl.BlockSpec((1, MLA_HEADS, QK_PAD, T), lambda b, t: (b, 0, 0, t)),
        pl.BlockSpec((1, MLA_HEADS, T, QK_PAD), lambda b, t: (b, 0, t, 0)),
        pl.BlockSpec((1, MLA_HEADS, MLA_V_ROWS, T), lambda b, t: (b, 0, 0, t)),
        pl.BlockSpec((1, SWA_KV_HEADS, SWA_HEAD_DIM, SWA_GROUP * T), lambda b, t: (b, 0, 0, t)),
        pl.BlockSpec((1, T, SWA_KV_HEADS * SWA_HEAD_DIM), lambda b, t: (b, t, 0)),
        pl.BlockSpec((1, SWA_KV_HEADS, SWA_V_ROWS, T), lambda b, t: (b, 0, 0, t)),
    )
    in_specs = [
        pl.BlockSpec(memory_space=pl.ANY),
        const(TR_ROWS, D_MODEL),
        const(D_MODEL, NAT_COLS),
        const(Q_LORA, 1),
        const(1, KV_LORA),
        const(MLA_HEADS * MLA_QK, Q_LORA),
        const(KV_LORA, MLA_HEADS * MLA_NOPE),
        const(D_MLA, KV_LORA),
        pl.BlockSpec((MLA_ROPE // 2, T), lambda b, t: (0, t)),
        pl.BlockSpec((MLA_ROPE // 2, T), lambda b, t: (0, t)),
    ]
    return pl.pallas_call(
        _proj_kernel,
        grid=grid,
        in_specs=in_specs,
        out_specs=out_specs,
        out_shape=out_shape,
        scratch_shapes=[pltpu.VMEM((TR_ROWS, T), jnp.float32),
                        pltpu.VMEM((T, NAT_COLS), jnp.float32),
                        pltpu.VMEM((X_RING, T, D_MODEL), jnp.float32),
                        pltpu.SemaphoreType.DMA((X_RING,))],
        compiler_params=pltpu.CompilerParams(
            dimension_semantics=("arbitrary", "arbitrary"),
            vmem_limit_bytes=VMEM_LIMIT_BYTES),
        name="proj",
    )(x, w_tr, w_nat, g_q, g_kv, w_uq_t, w_uk, w_uv_t, cos_t, sin_t)


def _mla_kernel(qt_ref, k_ref, vt_ref, o_ref, m_ref, acc_ref):
    n_chunks = k_ref.shape[2] // MLA_TK
    n_tiles = qt_ref.shape[3] // MLA_TQ

    def q_tile(t):
        return qt_ref[0, 0, :, t * MLA_TQ:(t + 1) * MLA_TQ]

    def chunk_slice(c):
        start = c * MLA_TK
        return pl.ds(start if isinstance(c, int) else pl.multiple_of(start, MLA_TK), MLA_TK)

    def key_chunk(c):
        return k_ref[0, 0, chunk_slice(c), :]

    def value_chunk(c):
        return vt_ref[0, 0, :, chunk_slice(c)]

    items = [(t, c) for t in range(n_tiles) for c in range(n_chunks)]
    m_use = [None] * n_tiles
    seen = [None] * n_tiles
    acc = [None] * n_tiles
    den = [None] * n_tiles
    excess = None

    def issue_scores(i):
        t, c = items[i]
        if c == 0:
            s0 = jnp.dot(k_ref[0, 0, 0:BF16_SUBLANES, :], q_tile(t),
                         preferred_element_type=jnp.float32)
            m_use[t] = jnp.max(s0, axis=0, keepdims=True)
        return jnp.dot(key_chunk(c), q_tile(t), preferred_element_type=jnp.float32)

    pending = [issue_scores(i) for i in range(MLA_AHEAD)]
    for i, (t, c) in enumerate(items):
        if i + MLA_AHEAD < len(items):
            pending.append(issue_scores(i + MLA_AHEAD))
        s = pending.pop(0)
        p = jnp.exp2(s - m_use[t])
        psum = jnp.sum(p.reshape(MLA_TK // SUBLANES, SUBLANES, MLA_TQ), axis=0)
        den[t] = psum if c == 0 else den[t] + psum
        p = p.astype(jnp.bfloat16)
        pv = jnp.dot(vt_ref[0, 0, 0:MLA_V, chunk_slice(c)], p,
                     preferred_element_type=jnp.float32)
        acc[t] = pv if c == 0 else acc[t] + pv
        pmax = jnp.max(p.reshape(MLA_TK // BF16_SUBLANES, BF16_SUBLANES, MLA_TQ), axis=0)
        seen[t] = pmax if seen[t] is None else jnp.maximum(seen[t], pmax)
        last = c + 1 == n_chunks
        if last or c == 0 or (c + 1) % MLA_REFRESH == 0:
            excess = seen[t] if excess is None else jnp.maximum(excess, seen[t])
            if last:
                acc_ref[t, 0:MLA_V, :] = acc[t]
                acc_ref[t, MLA_V:MLA_V + 1, :] = jnp.sum(den[t], axis=0, keepdims=True)
            else:
                top = jnp.max(seen[t].astype(jnp.float32), axis=0, keepdims=True)
                rise = jnp.maximum(jnp.log2(top), 0.0)
                alpha = jnp.exp2(-rise)
                acc[t] = acc[t] * alpha
                den[t] = den[t] * alpha
                m_use[t] = m_use[t] + rise
                seen[t] = None

    @pl.when(jnp.max(excess.astype(jnp.float32)) > 2.0 ** MLA_MAX_EXCESS)
    def _():
        for t in range(n_tiles):
            m_ref[...] = jnp.full(m_ref.shape, -jnp.inf, jnp.float32)
            acc_ref[t] = jnp.zeros(acc_ref.shape[1:], jnp.float32)

            def chunk(c, carry, t=t):
                s = jnp.dot(key_chunk(c), q_tile(t), preferred_element_type=jnp.float32)
                m_prev = m_ref[...]
                m_new = jnp.maximum(m_prev, jnp.max(s, axis=0, keepdims=True))
                p = jnp.exp2(s - m_new).astype(jnp.bfloat16)
                acc_ref[t] = jnp.exp2(m_prev - m_new) * acc_ref[t] + jnp.dot(
                    value_chunk(c), p, preferred_element_type=jnp.float32)
                m_ref[...] = m_new
                return carry

            lax.fori_loop(0, n_chunks, chunk, 0)

    for t in range(n_tiles):
        cols = slice(t * MLA_TQ, (t + 1) * MLA_TQ)
        o = acc_ref[t, 0:MLA_V, :] / acc_ref[t, MLA_V:MLA_V + 1, :]
        o_ref[0, :, cols] = o.astype(o_ref.dtype)


def _mla_attention(q_t, k, v_t):
    B, H, _, S = q_t.shape
    TQ = MLA_TQ
    step_q = MLA_STEP_TILES * TQ
    grid = (B, H, S // step_q)
    return pl.pallas_call(
        _mla_kernel,
        grid=grid,
        in_specs=[
            pl.BlockSpec((1, 1, QK_PAD, step_q), lambda b, h, i: (b, h, 0, i)),
            pl.BlockSpec((1, 1, S, QK_PAD), lambda b, h, i: (b, h, 0, 0)),
            pl.BlockSpec((1, 1, MLA_V_ROWS, S), lambda b, h, i: (b, h, 0, 0)),
        ],
        out_specs=pl.BlockSpec((1, MLA_V, step_q), lambda b, h, i: (b, h, i)),
        out_shape=jax.ShapeDtypeStruct((B, D_MLA, S), jnp.bfloat16),
        scratch_shapes=[pltpu.VMEM((1, TQ), jnp.float32),
                        pltpu.VMEM((MLA_STEP_TILES, MLA_V_ROWS, TQ), jnp.float32)],
        compiler_params=pltpu.CompilerParams(
            dimension_semantics=("arbitrary", "arbitrary", "arbitrary"),
            vmem_limit_bytes=VMEM_LIMIT_BYTES),
        name="mla",
    )(q_t, k, v_t)


def _bias_kernel(rel_ref, sink_ref, bucket_ref, o_ref):
    g = pl.program_id(0)
    for part in range(3):
        bucket = bucket_ref[part]
        for hh in range(SWA_GROUP):
            head = g * SWA_GROUP + hh
            tile = jnp.full((BLOCK, BLOCK), -jnp.inf, jnp.float32)
            for b in range(N_BUCKETS):
                tile = jnp.where(bucket == b, (rel_ref[b, head] - sink_ref[head]) * LOG2_E, tile)
            o_ref[0, part, :, hh * BLOCK:(hh + 1) * BLOCK] = tile


def _bias_table(rel_bias, sink, bucket):
    return pl.pallas_call(
        _bias_kernel,
        grid=(SWA_KV_HEADS,),
        in_specs=[pl.BlockSpec(memory_space=pltpu.SMEM),
                  pl.BlockSpec(memory_space=pltpu.SMEM),
                  pl.BlockSpec((3, BLOCK, BLOCK), lambda g: (0, 0, 0))],
        out_specs=pl.BlockSpec((1, 3, BLOCK, SWA_GROUP * BLOCK), lambda g: (g, 0, 0, 0)),
        out_shape=jax.ShapeDtypeStruct((SWA_KV_HEADS, 3, BLOCK, SWA_GROUP * BLOCK), jnp.float32),
        compiler_params=pltpu.CompilerParams(dimension_semantics=("arbitrary",)),
        name="bias",
    )(rel_bias, sink, bucket)


def _swa_kernel(qst_ref, ks_ref, vst_ref, bias_ref, o_ref):
    n_total = ks_ref.shape[1] // BLOCK
    kv_head = pl.program_id(1)
    n_local = o_ref.shape[2] // BLOCK
    step = pl.program_id(2)
    n_steps = pl.num_programs(2)
    neg_inf = jnp.float32(-jnp.inf)

    def band_starts(j):
        n = step * n_local + j
        return tuple(pl.multiple_of(b * BLOCK, BLOCK)
                     for b in (jnp.maximum(n - 1, 0), n, jnp.minimum(n + 1, n_total - 1)))

    def lane_block(j, width):
        start = j * width
        return pl.ds(start if isinstance(j, int) else pl.multiple_of(start, width), width)

    def band_scores(j):
        q_t = qst_ref[0, 0, :, lane_block(j, SWA_GROUP * BLOCK)]
        zeros = jnp.zeros_like(q_t)
        q_t = jnp.where(kv_head == 0, jnp.concatenate([q_t, zeros], axis=0),
                        jnp.concatenate([zeros, q_t], axis=0))
        n = step * n_local + j
        penalties = [None, None, None]
        if not isinstance(j, int) or j == 0:
            penalties[0] = jnp.where(n > 0, 0.0, neg_inf)
        if not isinstance(j, int) or j == n_local - 1:
            penalties[2] = jnp.where(n < n_total - 1, 0.0, neg_inf)
        scores = []
        for part, start in enumerate(band_starts(j)):
            k = ks_ref[0, pl.ds(start, BLOCK), :]
            s = jnp.dot(k, q_t, preferred_element_type=jnp.float32) + bias_ref[0, part]
            if penalties[part] is not None:
                s = s + penalties[part]
            scores.append(s)
        return scores

    def weighted_values(j, weights, n_rows):
        acc = None
        for part, start in enumerate(band_starts(j)):
            v_t = vst_ref[0, 0, 0:n_rows, pl.ds(start, BLOCK)]
            pv = jnp.dot(v_t, weights[part], preferred_element_type=jnp.float32)
            acc = pv if acc is None else acc + pv
        return acc

    def write_block(j, values, denom):
        o = values / denom
        cols = lane_block(j, BLOCK)
        for hh in range(SWA_GROUP):
            rows = slice(hh * SWA_HEAD_DIM, (hh + 1) * SWA_HEAD_DIM)
            o_ref[0, rows, cols] = o[:, hh * BLOCK:(hh + 1) * BLOCK].astype(o_ref.dtype)

    excess = None
    pending = [band_scores(j) for j in range(SWA_AHEAD)]
    for j in range(n_local):
        if j + SWA_AHEAD < n_local:
            pending.append(band_scores(j + SWA_AHEAD))
        weights, den = [], None
        for s in pending.pop(0):
            w = jnp.exp2(s)
            wsum = jnp.sum(w.reshape(BLOCK // SUBLANES, SUBLANES, w.shape[1]), axis=0)
            den = wsum if den is None else den + wsum
            w = w.astype(jnp.bfloat16)
            wmax = jnp.max(w.reshape(BLOCK // BF16_SUBLANES, BF16_SUBLANES, w.shape[1]), axis=0)
            excess = wmax if excess is None else jnp.maximum(excess, wmax)
            weights.append(w)
        write_block(j, weighted_values(j, weights, SWA_HEAD_DIM),
                    jnp.sum(den, axis=0, keepdims=True) + 1.0)

    @pl.when(jnp.max(excess.astype(jnp.float32)) > 2.0 ** SWA_MAX_EXCESS)
    def _():
        def exact_block(j, carry):
            scores = band_scores(j)
            m = jnp.zeros((1, scores[0].shape[1]), jnp.float32)
            for s in scores:
                m = jnp.maximum(m, jnp.max(s, axis=0, keepdims=True))
            acc = weighted_values(j, [jnp.exp2(s - m).astype(jnp.bfloat16) for s in scores],
                                  SWA_V_ROWS)
            write_block(j, acc[0:SWA_HEAD_DIM], acc[SWA_HEAD_DIM:SWA_HEAD_DIM + 1] + jnp.exp2(-m))
            return carry

        lax.fori_loop(0, n_local, exact_block, 0)


def _swa_attention(qs_t, ks, vs_t, bias_t):
    B, G, _, S = vs_t.shape
    T = SWA_TOKENS
    grid = (B, G, S // T)
    rows = SWA_GROUP * SWA_HEAD_DIM
    return pl.pallas_call(
        _swa_kernel,
        grid=grid,
        in_specs=[
            pl.BlockSpec((1, 1, SWA_HEAD_DIM, SWA_GROUP * T), lambda b, g, t: (b, g, 0, t)),
            pl.BlockSpec((1, S, SWA_KV_HEADS * SWA_HEAD_DIM), lambda b, g, t: (b, 0, 0)),
            pl.BlockSpec((1, 1, SWA_V_ROWS, S), lambda b, g, t: (b, g, 0, 0)),
            pl.BlockSpec((1, 3, BLOCK, SWA_GROUP * BLOCK), lambda b, g, t: (g, 0, 0, 0)),
        ],
        out_specs=pl.BlockSpec((1, rows, T), lambda b, g, t: (b, g, t)),
        out_shape=jax.ShapeDtypeStruct((B, D_SWA, S), jnp.bfloat16),
        compiler_params=pltpu.CompilerParams(
            dimension_semantics=("arbitrary", "arbitrary", "arbitrary"),
            vmem_limit_bytes=VMEM_LIMIT_BYTES),
        name="swa",
    )(qs_t, ks, vs_t, bias_t)


def _out_kernel(x_ref, oa_ref, ob_ref, wga_ref, wgb_ref, wa_ref, wb_ref, g_ref, b_ref, y_ref):
    n_chunks = x_ref.shape[1] // OUT_ROWS

    def gate_paths(c):
        xb = x_ref[0, c * OUT_ROWS:(c + 1) * OUT_ROWS, :].astype(jnp.bfloat16)
        return [jnp.dot(xb, wg_ref[...], preferred_element_type=jnp.float32)
                for wg_ref in (wga_ref, wgb_ref)]

    def project(c, gates):
        toks = slice(c * OUT_ROWS, (c + 1) * OUT_ROWS)
        out = None
        for o_ref_, gate, w_ref in ((oa_ref, gates[0], wa_ref), (ob_ref, gates[1], wb_ref)):
            o_nat = o_ref_[0, :, toks].astype(jnp.float32).T
            mixed = (o_nat * jax.nn.silu(gate)).astype(jnp.bfloat16)
            part = jnp.dot(mixed, w_ref[...], preferred_element_type=jnp.float32)
            out = part if out is None else out + part
        return out

    def residual_norm(c, out):
        rows = slice(c * OUT_ROWS, (c + 1) * OUT_ROWS)
        h = ALPHA * x_ref[0, rows, :] + out
        mu = jnp.mean(h, axis=-1, keepdims=True)
        var = jnp.mean(jnp.square(h - mu), axis=-1, keepdims=True)
        y_ref[0, rows, :] = ((h - mu) * lax.rsqrt(var + 1e-5)) * g_ref[...] + b_ref[...]

    gates_next = gate_paths(0)
    out_prev = None
    for c in range(n_chunks):
        gates = gates_next
        if c + 1 < n_chunks:
            gates_next = gate_paths(c + 1)
        out = project(c, gates)
        if out_prev is not None:
            residual_norm(c - 1, out_prev)
        out_prev = out
    residual_norm(n_chunks - 1, out_prev)


def _out_project(x, o_a_t, o_b_t, w_gate_a, w_gate_b, w_a, w_b, ln_g, ln_b):
    B, S, D = x.shape
    T = OUT_TOKENS
    grid = (B, S // T)
    const = lambda *shape: pl.BlockSpec(shape, lambda b, t: (0,) * len(shape))
    return pl.pallas_call(
        _out_kernel,
        grid=grid,
        in_specs=[
            pl.BlockSpec((1, T, D), lambda b, t: (b, t, 0)),
            pl.BlockSpec((1, D_MLA, T), lambda b, t: (b, 0, t)),
            pl.BlockSpec((1, D_SWA, T), lambda b, t: (b, 0, t)),
            const(D, D_MLA),
            const(D, D_SWA),
            const(D_MLA, D),
            const(D_SWA, D),
            const(1, D),
            const(1, D),
        ],
        out_specs=pl.BlockSpec((1, T, D), lambda b, t: (b, t, 0)),
        out_shape=jax.ShapeDtypeStruct((B, S, D), x.dtype),
        compiler_params=pltpu.CompilerParams(
            dimension_semantics=("arbitrary", "arbitrary"),
            vmem_limit_bytes=VMEM_LIMIT_BYTES),
        name="outproj",
    )(x, o_a_t, o_b_t, w_gate_a, w_gate_b, w_a, w_b, ln_g, ln_b)


def _t5_bucket(rel):
    half = N_BUCKETS // 2
    ret = np.where(rel > 0, half, 0)
    n = np.abs(rel)
    max_exact = half // 2
    large = max_exact + (np.log(np.maximum(n, 1).astype(np.float32) / max_exact)
                         / np.log(MAX_DISTANCE / max_exact) * (half - max_exact)).astype(np.int32)
    large = np.minimum(large, half - 1)
    return (ret + np.where(n < max_exact, n, large)).astype(np.int32)


def _band_geometry():
    q_loc = np.arange(BLOCK)
    k_loc = np.arange(3 * BLOCK) - BLOCK
    rel = k_loc[:, None] - q_loc[None, :]
    return _t5_bucket(rel), np.abs(rel) <= WINDOW


def _col(w_in, i):
    return w_in[:, IN_OFFSETS[i]:IN_OFFSETS[i + 1]]


def kernel(x, w_in, g_q, g_kv, w_uq, w_ukv, sink, rel_bias, w_out, ln_g, ln_b):
    B, S, _ = x.shape
    bf = jnp.bfloat16
    f32 = jnp.float32

    pos = jnp.arange(S, dtype=f32)
    inv_freq = ROPE_BASE ** (-jnp.arange(0, MLA_ROPE, 2, dtype=f32) / MLA_ROPE)
    ang_t = inv_freq[:, None] * pos[None, :]
    cos_t, sin_t = jnp.cos(ang_t), jnp.sin(ang_t)

    c_q, c_kv, k_rope, gate_a, q_s, k_s, v_s, gate_b = (_col(w_in, i) for i in range(8))
    w_tr = jnp.concatenate([c_q, q_s, v_s, k_rope], axis=1).T.astype(bf)
    w_nat = jnp.concatenate([c_kv, k_s], axis=1).astype(bf)
    w_ukv3 = w_ukv.reshape(KV_LORA, MLA_HEADS, MLA_NOPE + MLA_V)
    w_uk = w_ukv3[:, :, :MLA_NOPE].reshape(KV_LORA, MLA_HEADS * MLA_NOPE).astype(bf)
    w_uv_t = w_ukv3[:, :, MLA_NOPE:].reshape(KV_LORA, D_MLA).T.astype(bf)
    w_uq_t = w_uq.T.astype(bf)

    q_t, k, v_t, qs_t, ks, vs_t = _project(
        x, w_tr, w_nat, g_q.reshape(Q_LORA, 1), g_kv.reshape(1, KV_LORA),
        w_uq_t, w_uk, w_uv_t, cos_t, sin_t)

    o_a_t = _mla_attention(q_t, k, v_t)

    bucket, band = _band_geometry()
    bucket = jnp.asarray(np.where(band, bucket, -1).reshape(3, BLOCK, BLOCK), jnp.int32)
    bias_t = _bias_table(rel_bias.astype(f32), sink.astype(f32), bucket)

    o_b_t = _swa_attention(qs_t, ks, vs_t, bias_t)

    w_out_bf = w_out.astype(bf)
    return _out_project(x, o_a_t, o_b_t, gate_a.astype(bf), gate_b.astype(bf),
                        w_out_bf[:D_MLA], w_out_bf[D_MLA:],
                        ln_g.reshape(1, D_MODEL), ln_b.reshape(1, D_MODEL))
```

```python
import math

import jax
import jax.numpy as jnp
import numpy as np
from jax import lax
from jax.experimental import pallas as pl
from jax.experimental.pallas import tpu as pltpu

D_MODEL = 1024
MLA_HEADS = 8
MLA_NOPE = 64
MLA_ROPE = 32
MLA_V = 64
MLA_QK = MLA_NOPE + MLA_ROPE
Q_LORA = 256
KV_LORA = 128
D_MLA = MLA_HEADS * MLA_V
MLA_SCALE = 1.0 / math.sqrt(MLA_QK)
LOG2_E = math.log2(math.e)
MLA_Q_SCALE = MLA_SCALE * LOG2_E
BF16_SUBLANES = 16
MLA_V_ROWS = MLA_V + BF16_SUBLANES
ROPE_BASE = 10000.0

SWA_HEADS = 8
SWA_KV_HEADS = 2
SWA_HEAD_DIM = 64
SWA_GROUP = SWA_HEADS // SWA_KV_HEADS
D_SWA = SWA_HEADS * SWA_HEAD_DIM
WINDOW = 128
BLOCK = 128
SWA_SCALE = 1.0 / math.sqrt(SWA_HEAD_DIM)
SWA_Q_SCALE = SWA_SCALE * LOG2_E
SWA_V_ROWS = SWA_HEAD_DIM + BF16_SUBLANES
N_BUCKETS = 32
MAX_DISTANCE = 128

DEPTH = 1
ALPHA = (2.0 * DEPTH) ** 0.25

IN_SPLITS = (Q_LORA, KV_LORA, MLA_ROPE, D_MLA, D_SWA,
             SWA_KV_HEADS * SWA_HEAD_DIM, SWA_KV_HEADS * SWA_HEAD_DIM, D_SWA)
IN_OFFSETS = tuple(int(o) for o in np.cumsum((0,) + IN_SPLITS))

LANE = 128
SUBLANES = 8
QK_PAD = LANE
VMEM_LIMIT_BYTES = 56 * 1024 * 1024

PROJ_TOKENS = 1024
PROJ_PARTS = 4
MLA_TQ = 512
MLA_STEP_TILES = 4
MLA_TK = 256
MLA_AHEAD = 2
MLA_REFRESH = 8
MLA_MAX_EXCESS = 64.0
SWA_TOKENS = 4096
SWA_AHEAD = 1
SWA_MAX_EXCESS = 64.0
OUT_TOKENS = 1024
OUT_ROWS = 256

TR_CQ = 0
TR_QS = TR_CQ + Q_LORA
TR_VS = TR_QS + D_SWA
TR_KR = TR_VS + SWA_KV_HEADS * SWA_HEAD_DIM
TR_ROWS = TR_KR + MLA_ROPE
NAT_CKV = 0
NAT_KS = NAT_CKV + KV_LORA
NAT_COLS = NAT_KS + SWA_KV_HEADS * SWA_HEAD_DIM

_NT = (((1,), (1,)), ((), ()))


def _rsqrt_mean_sq(x, axis, eps):
    return lax.rsqrt(jnp.mean(x * x, axis=axis, keepdims=True) + eps)


def _proj_kernel(x_ref, wtr_ref, wnat_ref, gq_ref, gkv_ref, wuqt_ref, wuk_ref, wuvt_ref,
                 cos_t_ref, sin_t_ref,
                 qt_ref, k_ref, vt_ref, qst_ref, ks_ref, vst_ref,
                 tr_ref, nat_ref):
    n_tok = x_ref.shape[1] // PROJ_PARTS

    def project(p):
        tok = slice(p * n_tok, (p + 1) * n_tok)
        xb = x_ref[0, tok, :].astype(jnp.bfloat16)
        tr_ref[:, tok] = lax.dot_general(wtr_ref[...], xb, _NT, preferred_element_type=jnp.float32)
        nat_ref[tok, :] = jnp.dot(xb, wnat_ref[...], preferred_element_type=jnp.float32)

    project(0)
    for p in range(PROJ_PARTS):
        if p + 1 < PROJ_PARTS:
            project(p + 1)
        _proj_finish(p, n_tok, tr_ref, nat_ref, gq_ref, gkv_ref, wuqt_ref, wuk_ref, wuvt_ref,
                     cos_t_ref, sin_t_ref,
                     qt_ref, k_ref, vt_ref, qst_ref, ks_ref, vst_ref)


def _proj_finish(p, n_tok, tr_ref, nat_ref, gq_ref, gkv_ref, wuqt_ref, wuk_ref, wuvt_ref,
                 cos_t_ref, sin_t_ref,
                 qt_ref, k_ref, vt_ref, qst_ref, ks_ref, vst_ref):
    tok = slice(p * n_tok, (p + 1) * n_tok)
    half = MLA_ROPE // 2

    cq = tr_ref[TR_CQ:TR_CQ + Q_LORA, tok]
    cqn = (cq * _rsqrt_mean_sq(cq, 0, 1e-6)) * gq_ref[...]
    q_t = jnp.dot(wuqt_ref[...], cqn.astype(jnp.bfloat16),
                  preferred_element_type=jnp.float32)
    cos_t = cos_t_ref[:, tok]
    sin_t = sin_t_ref[:, tok]
    zeros_pad = jnp.zeros((QK_PAD - MLA_QK, n_tok), jnp.bfloat16)
    for h in range(MLA_HEADS):
        base = h * MLA_QK
        nope = q_t[base:base + MLA_NOPE]
        r1 = q_t[base + MLA_NOPE:base + MLA_NOPE + half]
        r2 = q_t[base + MLA_NOPE + half:base + MLA_QK]
        qt_ref[0, h, 0:MLA_NOPE, tok] = (nope * MLA_Q_SCALE).astype(jnp.bfloat16)
        qt_ref[0, h, MLA_NOPE:MLA_NOPE + half, tok] = (
            (r1 * cos_t - r2 * sin_t) * MLA_Q_SCALE).astype(jnp.bfloat16)
        qt_ref[0, h, MLA_NOPE + half:MLA_QK, tok] = (
            (r2 * cos_t + r1 * sin_t) * MLA_Q_SCALE).astype(jnp.bfloat16)
        qt_ref[0, h, MLA_QK:QK_PAD, tok] = zeros_pad

    ckv = nat_ref[tok, NAT_CKV:NAT_CKV + KV_LORA]
    kvn = ((ckv * _rsqrt_mean_sq(ckv, 1, 1e-6)) * gkv_ref[...]).astype(jnp.bfloat16)
    k_nope = jnp.dot(kvn, wuk_ref[...], preferred_element_type=jnp.float32)
    v_t = lax.dot_general(wuvt_ref[...], kvn, _NT, preferred_element_type=jnp.float32)
    kr1 = tr_ref[TR_KR:TR_KR + half, tok]
    kr2 = tr_ref[TR_KR + half:TR_KR + MLA_ROPE, tok]
    k_rope = jnp.concatenate(
        [jnp.zeros((MLA_NOPE, n_tok), jnp.float32),
         kr1 * cos_t - kr2 * sin_t,
         kr2 * cos_t + kr1 * sin_t,
         jnp.zeros((QK_PAD - MLA_QK, n_tok), jnp.float32)], axis=0).T
    lane = lax.broadcasted_iota(jnp.int32, k_rope.shape, 1)
    row_id = lax.broadcasted_iota(jnp.int32, (BF16_SUBLANES, n_tok), 0)
    ones_row = jnp.where(row_id == 0, 1.0, 0.0).astype(jnp.bfloat16)
    for h in range(MLA_HEADS):
        pair = k_nope[:, (h // 2) * LANE:(h // 2 + 1) * LANE]
        if h % 2:
            pair = pltpu.roll(pair, MLA_NOPE, axis=1)
        k_ref[0, h, tok, :] = jnp.where(lane < MLA_NOPE, pair, k_rope).astype(jnp.bfloat16)
        vt_ref[0, h, 0:MLA_V, tok] = v_t[h * MLA_V:(h + 1) * MLA_V].astype(jnp.bfloat16)
        vt_ref[0, h, MLA_V:MLA_V_ROWS, tok] = ones_row

    n_blocks = n_tok // BLOCK
    ks_ref[0, tok, :] = nat_ref[tok, NAT_KS:NAT_KS + SWA_KV_HEADS * SWA_HEAD_DIM].astype(jnp.bfloat16)
    for g in range(SWA_KV_HEADS):
        vst_ref[0, g, 0:SWA_HEAD_DIM, tok] = tr_ref[TR_VS + g * SWA_HEAD_DIM:
                                                    TR_VS + (g + 1) * SWA_HEAD_DIM, tok].astype(jnp.bfloat16)
        vst_ref[0, g, SWA_HEAD_DIM:SWA_V_ROWS, tok] = ones_row
        for hh in range(SWA_GROUP):
            row = TR_QS + (g * SWA_GROUP + hh) * SWA_HEAD_DIM
            q_h = (tr_ref[row:row + SWA_HEAD_DIM, tok] * SWA_Q_SCALE).astype(jnp.bfloat16)
            for nb in range(n_blocks):
                col = ((p * n_blocks + nb) * SWA_GROUP + hh) * BLOCK
                qst_ref[0, g, :, col:col + BLOCK] = q_h[:, nb * BLOCK:(nb + 1) * BLOCK]


def _project(x, w_tr, w_nat, g_q, g_kv, w_uq_t, w_uk, w_uv_t, cos_t, sin_t):
    B, S, _ = x.shape
    T = PROJ_TOKENS
    grid = (B, S // T)
    const = lambda *shape: pl.BlockSpec(shape, lambda b, t: (0,) * len(shape))
    bf = jnp.bfloat16
    out_shape = (
        jax.ShapeDtypeStruct((B, MLA_HEADS, QK_PAD, S), bf),
        jax.ShapeDtypeStruct((B, MLA_HEADS, S, QK_PAD), bf),
        jax.ShapeDtypeStruct((B, MLA_HEADS, MLA_V_ROWS, S), bf),
        jax.ShapeDtypeStruct((B, SWA_KV_HEADS, SWA_HEAD_DIM, SWA_GROUP * S), bf),
        jax.ShapeDtypeStruct((B, S, SWA_KV_HEADS * SWA_HEAD_DIM), bf),
        jax.ShapeDtypeStruct((B, SWA_KV_HEADS, SWA_V_ROWS, S), bf),
    )
    out_specs = (
        pl.BlockSpec((1, MLA_HEADS, QK_PAD, T), lambda b, t: (b, 0, 0, t)),
        pl.BlockSpec((1, MLA_HEADS, T, QK_PAD), lambda b, t: (b, 0, t, 0)),
        pl.BlockSpec((1, MLA_HEADS, MLA_V_ROWS, T), lambda b, t: (b, 0, 0, t)),
        pl.BlockSpec((1, SWA_KV_HEADS, SWA_HEAD_DIM, SWA_GROUP * T), lambda b, t: (b, 0, 0, t)),
        pl.BlockSpec((1, T, SWA_KV_HEADS * SWA_HEAD_DIM), lambda b, t: (b, t, 0)),
        pl.BlockSpec((1, SWA_KV_HEADS, SWA_V_ROWS, T), lambda b, t: (b, 0, 0, t)),
    )
    in_specs = [
        pl.BlockSpec((1, T, D_MODEL), lambda b, t: (b, t, 0)),
        const(TR_ROWS, D_MODEL),
        const(D_MODEL, NAT_COLS),
        const(Q_LORA, 1),
        const(1, KV_LORA),
        const(MLA_HEADS * MLA_QK, Q_LORA),
        const(KV_LORA, MLA_HEADS * MLA_NOPE),
        const(D_MLA, KV_LORA),
        pl.BlockSpec((MLA_ROPE // 2, T), lambda b, t: (0, t)),
        pl.BlockSpec((MLA_ROPE // 2, T), lambda b, t: (0, t)),
    ]
    return pl.pallas_call(
        _proj_kernel,
        grid=grid,
        in_specs=in_specs,
        out_specs=out_specs,
        out_shape=out_shape,
        scratch_shapes=[pltpu.VMEM((TR_ROWS, T), jnp.float32),
                        pltpu.VMEM((T, NAT_COLS), jnp.float32)],
        compiler_params=pltpu.CompilerParams(
            dimension_semantics=("arbitrary", "arbitrary"),
            vmem_limit_bytes=VMEM_LIMIT_BYTES),
        name="proj",
    )(x, w_tr, w_nat, g_q, g_kv, w_uq_t, w_uk, w_uv_t, cos_t, sin_t)


def _mla_kernel(qt_ref, k_ref, vt_ref, o_ref, m_ref, acc_ref):
    n_chunks = k_ref.shape[2] // MLA_TK
    n_tiles = qt_ref.shape[3] // MLA_TQ

    def q_tile(t):
        return qt_ref[0, 0, :, t * MLA_TQ:(t + 1) * MLA_TQ]

    def chunk_slice(c):
        start = c * MLA_TK
        return pl.ds(start if isinstance(c, int) else pl.multiple_of(start, MLA_TK), MLA_TK)

    def key_chunk(c):
        return k_ref[0, 0, chunk_slice(c), :]

    def value_chunk(c):
        return vt_ref[0, 0, :, chunk_slice(c)]

    items = [(t, c) for t in range(n_tiles) for c in range(n_chunks)]
    m_use = [None] * n_tiles
    seen = [None] * n_tiles
    acc = [None] * n_tiles
    den = [None] * n_tiles
    excess = None

    def issue_scores(i):
        t, c = items[i]
        if c == 0:
            s0 = jnp.dot(k_ref[0, 0, 0:BF16_SUBLANES, :], q_tile(t),
                         preferred_element_type=jnp.float32)
            m_use[t] = jnp.max(s0, axis=0, keepdims=True)
        return jnp.dot(key_chunk(c), q_tile(t), preferred_element_type=jnp.float32)

    pending = [issue_scores(i) for i in range(MLA_AHEAD)]
    for i, (t, c) in enumerate(items):
        if i + MLA_AHEAD < len(items):
            pending.append(issue_scores(i + MLA_AHEAD))
        s = pending.pop(0)
        p = jnp.exp2(s - m_use[t])
        psum = jnp.sum(p.reshape(MLA_TK // SUBLANES, SUBLANES, MLA_TQ), axis=0)
        den[t] = psum if c == 0 else den[t] + psum
        p = p.astype(jnp.bfloat16)
        pv = jnp.dot(vt_ref[0, 0, 0:MLA_V, chunk_slice(c)], p,
                     preferred_element_type=jnp.float32)
        acc[t] = pv if c == 0 else acc[t] + pv
        pmax = jnp.max(p.reshape(MLA_TK // BF16_SUBLANES, BF16_SUBLANES, MLA_TQ), axis=0)
        seen[t] = pmax if seen[t] is None else jnp.maximum(seen[t], pmax)
        last = c + 1 == n_chunks
        if last or c == 0 or (c + 1) % MLA_REFRESH == 0:
            excess = seen[t] if excess is None else jnp.maximum(excess, seen[t])
            if last:
                acc_ref[t, 0:MLA_V, :] = acc[t]
                acc_ref[t, MLA_V:MLA_V + 1, :] = jnp.sum(den[t], axis=0, keepdims=True)
            else:
                top = jnp.max(seen[t].astype(jnp.float32), axis=0, keepdims=True)
                rise = jnp.maximum(jnp.log2(top), 0.0)
                alpha = jnp.exp2(-rise)
                acc[t] = acc[t] * alpha
                den[t] = den[t] * alpha
                m_use[t] = m_use[t] + rise
                seen[t] = None

    @pl.when(jnp.max(excess.astype(jnp.float32)) > 2.0 ** MLA_MAX_EXCESS)
    def _():
        for t in range(n_tiles):
            m_ref[...] = jnp.full(m_ref.shape, -jnp.inf, jnp.float32)
            acc_ref[t] = jnp.zeros(acc_ref.shape[1:], jnp.float32)

            def chunk(c, carry, t=t):
                s = jnp.dot(key_chunk(c), q_tile(t), preferred_element_type=jnp.float32)
                m_prev = m_ref[...]
                m_new = jnp.maximum(m_prev, jnp.max(s, axis=0, keepdims=True))
                p = jnp.exp2(s - m_new).astype(jnp.bfloat16)
                acc_ref[t] = jnp.exp2(m_prev - m_new) * acc_ref[t] + jnp.dot(
                    value_chunk(c), p, preferred_element_type=jnp.float32)
                m_ref[...] = m_new
                return carry

            lax.fori_loop(0, n_chunks, chunk, 0)

    for t in range(n_tiles):
        cols = slice(t * MLA_TQ, (t + 1) * MLA_TQ)
        o = acc_ref[t, 0:MLA_V, :] / acc_ref[t, MLA_V:MLA_V + 1, :]
        o_ref[0, :, cols] = o.astype(o_ref.dtype)


def _mla_attention(q_t, k, v_t):
    B, H, _, S = q_t.shape
    TQ = MLA_TQ
    step_q = MLA_STEP_TILES * TQ
    grid = (B, H, S // step_q)
    return pl.pallas_call(
        _mla_kernel,
        grid=grid,
        in_specs=[
            pl.BlockSpec((1, 1, QK_PAD, step_q), lambda b, h, i: (b, h, 0, i)),
            pl.BlockSpec((1, 1, S, QK_PAD), lambda b, h, i: (b, h, 0, 0)),
            pl.BlockSpec((1, 1, MLA_V_ROWS, S), lambda b, h, i: (b, h, 0, 0)),
        ],
        out_specs=pl.BlockSpec((1, MLA_V, step_q), lambda b, h, i: (b, h, i)),
        out_shape=jax.ShapeDtypeStruct((B, D_MLA, S), jnp.bfloat16),
        scratch_shapes=[pltpu.VMEM((1, TQ), jnp.float32),
                        pltpu.VMEM((MLA_STEP_TILES, MLA_V_ROWS, TQ), jnp.float32)],
        compiler_params=pltpu.CompilerParams(
            dimension_semantics=("arbitrary", "arbitrary", "arbitrary"),
            vmem_limit_bytes=VMEM_LIMIT_BYTES),
        name="mla",
    )(q_t, k, v_t)


def _bias_kernel(rel_ref, sink_ref, bucket_ref, o_ref):
    g = pl.program_id(0)
    for part in range(3):
        bucket = bucket_ref[part]
        for hh in range(SWA_GROUP):
            head = g * SWA_GROUP + hh
            tile = jnp.full((BLOCK, BLOCK), -jnp.inf, jnp.float32)
            for b in range(N_BUCKETS):
                tile = jnp.where(bucket == b, (rel_ref[b, head] - sink_ref[head]) * LOG2_E, tile)
            o_ref[0, part, :, hh * BLOCK:(hh + 1) * BLOCK] = tile


def _bias_table(rel_bias, sink, bucket):
    return pl.pallas_call(
        _bias_kernel,
        grid=(SWA_KV_HEADS,),
        in_specs=[pl.BlockSpec(memory_space=pltpu.SMEM),
                  pl.BlockSpec(memory_space=pltpu.SMEM),
                  pl.BlockSpec((3, BLOCK, BLOCK), lambda g: (0, 0, 0))],
        out_specs=pl.BlockSpec((1, 3, BLOCK, SWA_GROUP * BLOCK), lambda g: (g, 0, 0, 0)),
        out_shape=jax.ShapeDtypeStruct((SWA_KV_HEADS, 3, BLOCK, SWA_GROUP * BLOCK), jnp.float32),
        compiler_params=pltpu.CompilerParams(dimension_semantics=("arbitrary",)),
        name="bias",
    )(rel_bias, sink, bucket)


def _swa_kernel(qst_ref, ks_ref, vst_ref, bias_ref, o_ref):
    n_total = ks_ref.shape[1] // BLOCK
    kv_head = pl.program_id(1)
    n_local = o_ref.shape[2] // BLOCK
    step = pl.program_id(2)
    neg_inf = jnp.float32(-jnp.inf)

    def band_starts(j):
        n = step * n_local + j
        return tuple(pl.multiple_of(b * BLOCK, BLOCK)
                     for b in (jnp.maximum(n - 1, 0), n, jnp.minimum(n + 1, n_total - 1)))

    def lane_block(j, width):
        start = j * width
        return pl.ds(start if isinstance(j, int) else pl.multiple_of(start, width), width)

    def band_scores(j):
        q_t = qst_ref[0, 0, :, lane_block(j, SWA_GROUP * BLOCK)]
        zeros = jnp.zeros_like(q_t)
        q_t = jnp.where(kv_head == 0, jnp.concatenate([q_t, zeros], axis=0),
                        jnp.concatenate([zeros, q_t], axis=0))
        n = step * n_local + j
        penalties = [None, None, None]
        if not isinstance(j, int) or j == 0:
            penalties[0] = jnp.where(n > 0, 0.0, neg_inf)
        if not isinstance(j, int) or j == n_local - 1:
            penalties[2] = jnp.where(n < n_total - 1, 0.0, neg_inf)
        scores = []
        for part, start in enumerate(band_starts(j)):
            k = ks_ref[0, pl.ds(start, BLOCK), :]
            s = jnp.dot(k, q_t, preferred_element_type=jnp.float32) + bias_ref[0, part]
            if penalties[part] is not None:
                s = s + penalties[part]
            scores.append(s)
        return scores

    def weighted_values(j, weights, n_rows):
        acc = None
        for part, start in enumerate(band_starts(j)):
            v_t = vst_ref[0, 0, 0:n_rows, pl.ds(start, BLOCK)]
            pv = jnp.dot(v_t, weights[part], preferred_element_type=jnp.float32)
            acc = pv if acc is None else acc + pv
        return acc

    def write_block(j, values, denom):
        o = values / denom
        cols = lane_block(j, BLOCK)
        for hh in range(SWA_GROUP):
            rows = slice(hh * SWA_HEAD_DIM, (hh + 1) * SWA_HEAD_DIM)
            o_ref[0, rows, cols] = o[:, hh * BLOCK:(hh + 1) * BLOCK].astype(o_ref.dtype)

    excess = None
    pending = [band_scores(j) for j in range(SWA_AHEAD)]
    for j in range(n_local):
        if j + SWA_AHEAD < n_local:
            pending.append(band_scores(j + SWA_AHEAD))
        weights, den = [], None
        for s in pending.pop(0):
            w = jnp.exp2(s)
            wsum = jnp.sum(w.reshape(BLOCK // SUBLANES, SUBLANES, w.shape[1]), axis=0)
            den = wsum if den is None else den + wsum
            w = w.astype(jnp.bfloat16)
            wmax = jnp.max(w.reshape(BLOCK // BF16_SUBLANES, BF16_SUBLANES, w.shape[1]), axis=0)
            excess = wmax if excess is None else jnp.maximum(excess, wmax)
            weights.append(w)
        write_block(j, weighted_values(j, weights, SWA_HEAD_DIM),
                    jnp.sum(den, axis=0, keepdims=True) + 1.0)

    @pl.when(jnp.max(excess.astype(jnp.float32)) > 2.0 ** SWA_MAX_EXCESS)
    def _():
        def exact_block(j, carry):
            scores = band_scores(j)
            m = jnp.zeros((1, scores[0].shape[1]), jnp.float32)
            for s in scores:
                m = jnp.maximum(m, jnp.max(s, axis=0, keepdims=True))
            acc = weighted_values(j, [jnp.exp2(s - m).astype(jnp.bfloat16) for s in scores],
                                  SWA_V_ROWS)
            write_block(j, acc[0:SWA_HEAD_DIM], acc[SWA_HEAD_DIM:SWA_HEAD_DIM + 1] + jnp.exp2(-m))
            return carry

        lax.fori_loop(0, n_local, exact_block, 0)


def _swa_attention(qs_t, ks, vs_t, bias_t):
    B, G, _, S = vs_t.shape
    T = SWA_TOKENS
    grid = (B, G, S // T)
    rows = SWA_GROUP * SWA_HEAD_DIM
    return pl.pallas_call(
        _swa_kernel,
        grid=grid,
        in_specs=[
            pl.BlockSpec((1, 1, SWA_HEAD_DIM, SWA_GROUP * T), lambda b, g, t: (b, g, 0, t)),
            pl.BlockSpec((1, S, SWA_KV_HEADS * SWA_HEAD_DIM), lambda b, g, t: (b, 0, 0)),
            pl.BlockSpec((1, 1, SWA_V_ROWS, S), lambda b, g, t: (b, g, 0, 0)),
            pl.BlockSpec((1, 3, BLOCK, SWA_GROUP * BLOCK), lambda b, g, t: (g, 0, 0, 0)),
        ],
        out_specs=pl.BlockSpec((1, rows, T), lambda b, g, t: (b, g, t)),
        out_shape=jax.ShapeDtypeStruct((B, D_SWA, S), jnp.bfloat16),
        compiler_params=pltpu.CompilerParams(
            dimension_semantics=("arbitrary", "arbitrary", "arbitrary"),
            vmem_limit_bytes=VMEM_LIMIT_BYTES),
        name="swa",
    )(qs_t, ks, vs_t, bias_t)


def _out_kernel(x_ref, oa_ref, ob_ref, wga_ref, wgb_ref, wa_ref, wb_ref, g_ref, b_ref, y_ref):
    n_chunks = x_ref.shape[1] // OUT_ROWS

    def gate_paths(c):
        xb = x_ref[0, c * OUT_ROWS:(c + 1) * OUT_ROWS, :].astype(jnp.bfloat16)
        return [jnp.dot(xb, wg_ref[...], preferred_element_type=jnp.float32)
                for wg_ref in (wga_ref, wgb_ref)]

    def project(c, gates):
        toks = slice(c * OUT_ROWS, (c + 1) * OUT_ROWS)
        out = None
        for o_ref_, gate, w_ref in ((oa_ref, gates[0], wa_ref), (ob_ref, gates[1], wb_ref)):
            o_nat = o_ref_[0, :, toks].astype(jnp.float32).T
            mixed = (o_nat * jax.nn.silu(gate)).astype(jnp.bfloat16)
            part = jnp.dot(mixed, w_ref[...], preferred_element_type=jnp.float32)
            out = part if out is None else out + part
        return out

    def residual_norm(c, out):
        rows = slice(c * OUT_ROWS, (c + 1) * OUT_ROWS)
        h = ALPHA * x_ref[0, rows, :] + out
        mu = jnp.mean(h, axis=-1, keepdims=True)
        var = jnp.mean(jnp.square(h - mu), axis=-1, keepdims=True)
        y_ref[0, rows, :] = ((h - mu) * lax.rsqrt(var + 1e-5)) * g_ref[...] + b_ref[...]

    gates_next = gate_paths(0)
    out_prev = None
    for c in range(n_chunks):
        gates = gates_next
        if c + 1 < n_chunks:
            gates_next = gate_paths(c + 1)
        out = project(c, gates)
        if out_prev is not None:
            residual_norm(c - 1, out_prev)
        out_prev = out
    residual_norm(n_chunks - 1, out_prev)


def _out_project(x, o_a_t, o_b_t, w_gate_a, w_gate_b, w_a, w_b, ln_g, ln_b):
    B, S, D = x.shape
    T = OUT_TOKENS
    grid = (B, S // T)
    const = lambda *shape: pl.BlockSpec(shape, lambda b, t: (0,) * len(shape))
    return pl.pallas_call(
        _out_kernel,
        grid=grid,
        in_specs=[
            pl.BlockSpec((1, T, D), lambda b, t: (b, t, 0)),
            pl.BlockSpec((1, D_MLA, T), lambda b, t: (b, 0, t)),
            pl.BlockSpec((1, D_SWA, T), lambda b, t: (b, 0, t)),
            const(D, D_MLA),
            const(D, D_SWA),
            const(D_MLA, D),
            const(D_SWA, D),
            const(1, D),
            const(1, D),
        ],
        out_specs=pl.BlockSpec((1, T, D), lambda b, t: (b, t, 0)),
        out_shape=jax.ShapeDtypeStruct((B, S, D), x.dtype),
        compiler_params=pltpu.CompilerParams(
            dimension_semantics=("arbitrary", "arbitrary"),
            vmem_limit_bytes=VMEM_LIMIT_BYTES),
        name="outproj",
    )(x, o_a_t, o_b_t, w_gate_a, w_gate_b, w_a, w_b, ln_g, ln_b)


def _t5_bucket(rel):
    half = N_BUCKETS // 2
    ret = np.where(rel > 0, half, 0)
    n = np.abs(rel)
    max_exact = half // 2
    large = max_exact + (np.log(np.maximum(n, 1).astype(np.float32) / max_exact)
                         / np.log(MAX_DISTANCE / max_exact) * (half - max_exact)).astype(np.int32)
    large = np.minimum(large, half - 1)
    return (ret + np.where(n < max_exact, n, large)).astype(np.int32)


def _band_geometry():
    q_loc = np.arange(BLOCK)
    k_loc = np.arange(3 * BLOCK) - BLOCK
    rel = k_loc[:, None] - q_loc[None, :]
    return _t5_bucket(rel), np.abs(rel) <= WINDOW


def _col(w_in, i):
    return w_in[:, IN_OFFSETS[i]:IN_OFFSETS[i + 1]]


def kernel(x, w_in, g_q, g_kv, w_uq, w_ukv, sink, rel_bias, w_out, ln_g, ln_b):
    B, S, _ = x.shape
    bf = jnp.bfloat16
    f32 = jnp.float32

    pos = jnp.arange(S, dtype=f32)
    inv_freq = ROPE_BASE ** (-jnp.arange(0, MLA_ROPE, 2, dtype=f32) / MLA_ROPE)
    ang_t = inv_freq[:, None] * pos[None, :]
    cos_t, sin_t = jnp.cos(ang_t), jnp.sin(ang_t)

    c_q, c_kv, k_rope, gate_a, q_s, k_s, v_s, gate_b = (_col(w_in, i) for i in range(8))
    w_tr = jnp.concatenate([c_q, q_s, v_s, k_rope], axis=1).T.astype(bf)
    w_nat = jnp.concatenate([c_kv, k_s], axis=1).astype(bf)
    w_ukv3 = w_ukv.reshape(KV_LORA, MLA_HEADS, MLA_NOPE + MLA_V)
    w_uk = w_ukv3[:, :, :MLA_NOPE].reshape(KV_LORA, MLA_HEADS * MLA_NOPE).astype(bf)
    w_uv_t = w_ukv3[:, :, MLA_NOPE:].reshape(KV_LORA, D_MLA).T.astype(bf)
    w_uq_t = w_uq.T.astype(bf)

    q_t, k, v_t, qs_t, ks, vs_t = _project(
        x, w_tr, w_nat, g_q.reshape(Q_LORA, 1), g_kv.reshape(1, KV_LORA),
        w_uq_t, w_uk, w_uv_t, cos_t, sin_t)

    o_a_t = _mla_attention(q_t, k, v_t)

    bucket, band = _band_geometry()
    bucket = jnp.asarray(np.where(band, bucket, -1).reshape(3, BLOCK, BLOCK), jnp.int32)
    bias_t = _bias_table(rel_bias.astype(f32), sink.astype(f32), bucket)

    o_b_t = _swa_attention(qs_t, ks, vs_t, bias_t)

    w_out_bf = w_out.astype(bf)
    return _out_project(x, o_a_t, o_b_t, gate_a.astype(bf), gate_b.astype(bf),
                        w_out_bf[:D_MLA], w_out_bf[D_MLA:],
                        ln_g.reshape(1, D_MODEL), ln_b.reshape(1, D_MODEL))
```

```python
import math

import jax
import jax.numpy as jnp
import numpy as np
from jax import lax
from jax.experimental import pallas as pl
from jax.experimental.pallas import tpu as pltpu

D_MODEL = 1024
MLA_HEADS = 8
MLA_NOPE = 64
MLA_ROPE = 32
MLA_V = 64
MLA_QK = MLA_NOPE + MLA_ROPE
Q_LORA = 256
KV_LORA = 128
D_MLA = MLA_HEADS * MLA_V
MLA_SCALE = 1.0 / math.sqrt(MLA_QK)
LOG2_E = math.log2(math.e)
MLA_Q_SCALE = MLA_SCALE * LOG2_E
BF16_SUBLANES = 16
MLA_V_ROWS = MLA_V + BF16_SUBLANES
ROPE_BASE = 10000.0

SWA_HEADS = 8
SWA_KV_HEADS = 2
SWA_HEAD_DIM = 64
SWA_GROUP = SWA_HEADS // SWA_KV_HEADS
D_SWA = SWA_HEADS * SWA_HEAD_DIM
WINDOW = 128
BLOCK = 128
SWA_SCALE = 1.0 / math.sqrt(SWA_HEAD_DIM)
SWA_Q_SCALE = SWA_SCALE * LOG2_E
SWA_V_ROWS = SWA_HEAD_DIM + BF16_SUBLANES
N_BUCKETS = 32
MAX_DISTANCE = 128

DEPTH = 1
ALPHA = (2.0 * DEPTH) ** 0.25

IN_SPLITS = (Q_LORA, KV_LORA, MLA_ROPE, D_MLA, D_SWA,
             SWA_KV_HEADS * SWA_HEAD_DIM, SWA_KV_HEADS * SWA_HEAD_DIM, D_SWA)
IN_OFFSETS = tuple(int(o) for o in np.cumsum((0,) + IN_SPLITS))

LANE = 128
SUBLANES = 8
QK_PAD = LANE
VMEM_LIMIT_BYTES = 56 * 1024 * 1024

PROJ_TOKENS = 1024
PROJ_PARTS = 4
MLA_TQ = 512
MLA_STEP_TILES = 4
MLA_TK = 256
MLA_AHEAD = 2
MLA_REFRESH = 8
MLA_MAX_EXCESS = 64.0
SWA_TOKENS = 4096
SWA_AHEAD = 1
SWA_MAX_EXCESS = 64.0
OUT_TOKENS = 1024
OUT_ROWS = 256

TR_CQ = 0
TR_QS = TR_CQ + Q_LORA
TR_VS = TR_QS + D_SWA
TR_KR = TR_VS + SWA_KV_HEADS * SWA_HEAD_DIM
TR_ROWS = TR_KR + MLA_ROPE
NAT_CKV = 0
NAT_KS = NAT_CKV + KV_LORA
NAT_COLS = NAT_KS + SWA_KV_HEADS * SWA_HEAD_DIM

_NT = (((1,), (1,)), ((), ()))


def _rsqrt_mean_sq(x, axis, eps):
    return lax.rsqrt(jnp.mean(x * x, axis=axis, keepdims=True) + eps)


def _proj_kernel(x_ref, wtr_ref, wnat_ref, gq_ref, gkv_ref, wuqt_ref, wuk_ref, wuvt_ref,
                 cos_t_ref, sin_t_ref,
                 qt_ref, k_ref, vt_ref, qst_ref, ks_ref, vst_ref,
                 tr_ref, nat_ref):
    n_tok = x_ref.shape[1] // PROJ_PARTS

    def project(p):
        tok = slice(p * n_tok, (p + 1) * n_tok)
        xb = x_ref[0, tok, :].astype(jnp.bfloat16)
        tr_ref[:, tok] = lax.dot_general(wtr_ref[...], xb, _NT, preferred_element_type=jnp.float32)
        nat_ref[tok, :] = jnp.dot(xb, wnat_ref[...], preferred_element_type=jnp.float32)

    project(0)
    for p in range(PROJ_PARTS):
        if p + 1 < PROJ_PARTS:
            project(p + 1)
        _proj_finish(p, n_tok, tr_ref, nat_ref, gq_ref, gkv_ref, wuqt_ref, wuk_ref, wuvt_ref,
                     cos_t_ref, sin_t_ref,
                     qt_ref, k_ref, vt_ref, qst_ref, ks_ref, vst_ref)


def _proj_finish(p, n_tok, tr_ref, nat_ref, gq_ref, gkv_ref, wuqt_ref, wuk_ref, wuvt_ref,
                 cos_t_ref, sin_t_ref,
                 qt_ref, k_ref, vt_ref, qst_ref, ks_ref, vst_ref):
    tok = slice(p * n_tok, (p + 1) * n_tok)
    half = MLA_ROPE // 2

    cq = tr_ref[TR_CQ:TR_CQ + Q_LORA, tok]
    cqn = (cq * _rsqrt_mean_sq(cq, 0, 1e-6)) * gq_ref[...]
    q_t = jnp.dot(wuqt_ref[...], cqn.astype(jnp.bfloat16),
                  preferred_element_type=jnp.float32)
    cos_t = cos_t_ref[:, tok]
    sin_t = sin_t_ref[:, tok]
    zeros_pad = jnp.zeros((QK_PAD - MLA_QK, n_tok), jnp.bfloat16)
    for h in range(MLA_HEADS):
        base = h * MLA_QK
        nope = q_t[base:base + MLA_NOPE]
        r1 = q_t[base + MLA_NOPE:base + MLA_NOPE + half]
        r2 = q_t[base + MLA_NOPE + half:base + MLA_QK]
        qt_ref[0, h, 0:MLA_NOPE, tok] = (nope * MLA_Q_SCALE).astype(jnp.bfloat16)
        qt_ref[0, h, MLA_NOPE:MLA_NOPE + half, tok] = (
            (r1 * cos_t - r2 * sin_t) * MLA_Q_SCALE).astype(jnp.bfloat16)
        qt_ref[0, h, MLA_NOPE + half:MLA_QK, tok] = (
            (r2 * cos_t + r1 * sin_t) * MLA_Q_SCALE).astype(jnp.bfloat16)
        qt_ref[0, h, MLA_QK:QK_PAD, tok] = zeros_pad

    ckv = nat_ref[tok, NAT_CKV:NAT_CKV + KV_LORA]
    kvn = ((ckv * _rsqrt_mean_sq(ckv, 1, 1e-6)) * gkv_ref[...]).astype(jnp.bfloat16)
    k_nope = jnp.dot(kvn, wuk_ref[...], preferred_element_type=jnp.float32)
    v_t = lax.dot_general(wuvt_ref[...], kvn, _NT, preferred_element_type=jnp.float32)
    kr1 = tr_ref[TR_KR:TR_KR + half, tok]
    kr2 = tr_ref[TR_KR + half:TR_KR + MLA_ROPE, tok]
    k_rope = jnp.concatenate(
        [jnp.zeros((MLA_NOPE, n_tok), jnp.float32),
         kr1 * cos_t - kr2 * sin_t,
         kr2 * cos_t + kr1 * sin_t,
         jnp.zeros((QK_PAD - MLA_QK, n_tok), jnp.float32)], axis=0).T
    lane = lax.broadcasted_iota(jnp.int32, k_rope.shape, 1)
    row_id = lax.broadcasted_iota(jnp.int32, (BF16_SUBLANES, n_tok), 0)
    ones_row = jnp.where(row_id == 0, 1.0, 0.0).astype(jnp.bfloat16)
    for h in range(MLA_HEADS):
        pair = k_nope[:, (h // 2) * LANE:(h // 2 + 1) * LANE]
        if h % 2:
            pair = pltpu.roll(pair, MLA_NOPE, axis=1)
        k_ref[0, h, tok, :] = jnp.where(lane < MLA_NOPE, pair, k_rope).astype(jnp.bfloat16)
        vt_ref[0, h, 0:MLA_V, tok] = v_t[h * MLA_V:(h + 1) * MLA_V].astype(jnp.bfloat16)
        vt_ref[0, h, MLA_V:MLA_V_ROWS, tok] = ones_row

    n_blocks = n_tok // BLOCK
    ks_ref[0, tok, :] = nat_ref[tok, NAT_KS:NAT_KS + SWA_KV_HEADS * SWA_HEAD_DIM].astype(jnp.bfloat16)
    for g in range(SWA_KV_HEADS):
        vst_ref[0, g, 0:SWA_HEAD_DIM, tok] = tr_ref[TR_VS + g * SWA_HEAD_DIM:
                                                    TR_VS + (g + 1) * SWA_HEAD_DIM, tok].astype(jnp.bfloat16)
        vst_ref[0, g, SWA_HEAD_DIM:SWA_V_ROWS, tok] = ones_row
        for hh in range(SWA_GROUP):
            row = TR_QS + (g * SWA_GROUP + hh) * SWA_HEAD_DIM
            q_h = (tr_ref[row:row + SWA_HEAD_DIM, tok] * SWA_Q_SCALE).astype(jnp.bfloat16)
            for nb in range(n_blocks):
                col = ((p * n_blocks + nb) * SWA_GROUP + hh) * BLOCK
                qst_ref[0, g, :, col:col + BLOCK] = q_h[:, nb * BLOCK:(nb + 1) * BLOCK]


def _project(x, w_tr, w_nat, g_q, g_kv, w_uq_t, w_uk, w_uv_t, cos_t, sin_t):
    B, S, _ = x.shape
    T = PROJ_TOKENS
    grid = (B, S // T)
    const = lambda *shape: pl.BlockSpec(shape, lambda b, t: (0,) * len(shape))
    bf = jnp.bfloat16
    out_shape = (
        jax.ShapeDtypeStruct((B, MLA_HEADS, QK_PAD, S), bf),
        jax.ShapeDtypeStruct((B, MLA_HEADS, S, QK_PAD), bf),
        jax.ShapeDtypeStruct((B, MLA_HEADS, MLA_V_ROWS, S), bf),
        jax.ShapeDtypeStruct((B, SWA_KV_HEADS, SWA_HEAD_DIM, SWA_GROUP * S), bf),
        jax.ShapeDtypeStruct((B, S, SWA_KV_HEADS * SWA_HEAD_DIM), bf),
        jax.ShapeDtypeStruct((B, SWA_KV_HEADS, SWA_V_ROWS, S), bf),
    )
    out_specs = (
        pl.BlockSpec((1, MLA_HEADS, QK_PAD, T), lambda b, t: (b, 0, 0, t)),
        pl.BlockSpec((1, MLA_HEADS, T, QK_PAD), lambda b, t: (b, 0, t, 0)),
        pl.BlockSpec((1, MLA_HEADS, MLA_V_ROWS, T), lambda b, t: (b, 0, 0, t)),
        pl.BlockSpec((1, SWA_KV_HEADS, SWA_HEAD_DIM, SWA_GROUP * T), lambda b, t: (b, 0, 0, t)),
        pl.BlockSpec((1, T, SWA_KV_HEADS * SWA_HEAD_DIM), lambda b, t: (b, t, 0)),
        pl.BlockSpec((1, SWA_KV_HEADS, SWA_V_ROWS, T), lambda b, t: (b, 0, 0, t)),
    )
    in_specs = [
        pl.BlockSpec((1, T, D_MODEL), lambda b, t: (b, t, 0)),
        const(TR_ROWS, D_MODEL),
        const(D_MODEL, NAT_COLS),
        const(Q_LORA, 1),
        const(1, KV_LORA),
        const(MLA_HEADS * MLA_QK, Q_LORA),
        const(KV_LORA, MLA_HEADS * MLA_NOPE),
        const(D_MLA, KV_LORA),
        pl.BlockSpec((MLA_ROPE // 2, T), lambda b, t: (0, t)),
        pl.BlockSpec((MLA_ROPE // 2, T), lambda b, t: (0, t)),
    ]
    return pl.pallas_call(
        _proj_kernel,
        grid=grid,
        in_specs=in_specs,
        out_specs=out_specs,
        out_shape=out_shape,
        scratch_shapes=[pltpu.VMEM((TR_ROWS, T), jnp.float32),
                        pltpu.VMEM((T, NAT_COLS), jnp.float32)],
        compiler_params=pltpu.CompilerParams(
            dimension_semantics=("arbitrary", "arbitrary"),
            allow_input_fusion=[False, True, True, False, False, True, True, True, False, False],
            vmem_limit_bytes=VMEM_LIMIT_BYTES),
        name="proj",
    )(x, w_tr, w_nat, g_q, g_kv, w_uq_t, w_uk, w_uv_t, cos_t, sin_t)


def _mla_kernel(qt_ref, k_ref, vt_ref, o_ref, m_ref, acc_ref):
    n_chunks = k_ref.shape[2] // MLA_TK
    n_tiles = qt_ref.shape[3] // MLA_TQ

    def q_tile(t):
        return qt_ref[0, 0, :, t * MLA_TQ:(t + 1) * MLA_TQ]

    def chunk_slice(c):
        start = c * MLA_TK
        return pl.ds(start if isinstance(c, int) else pl.multiple_of(start, MLA_TK), MLA_TK)

    def key_chunk(c):
        return k_ref[0, 0, chunk_slice(c), :]

    def value_chunk(c):
        return vt_ref[0, 0, :, chunk_slice(c)]

    items = [(t, c) for t in range(n_tiles) for c in range(n_chunks)]
    m_use = [None] * n_tiles
    seen = [None] * n_tiles
    acc = [None] * n_tiles
    den = [None] * n_tiles
    excess = None

    def issue_scores(i):
        t, c = items[i]
        if c == 0:
            s0 = jnp.dot(k_ref[0, 0, 0:BF16_SUBLANES, :], q_tile(t),
                         preferred_element_type=jnp.float32)
            m_use[t] = jnp.max(s0, axis=0, keepdims=True)
        return jnp.dot(key_chunk(c), q_tile(t), preferred_element_type=jnp.float32)

    pending = [issue_scores(i) for i in range(MLA_AHEAD)]
    for i, (t, c) in enumerate(items):
        if i + MLA_AHEAD < len(items):
            pending.append(issue_scores(i + MLA_AHEAD))
        s = pending.pop(0)
        p = jnp.exp2(s - m_use[t])
        psum = jnp.sum(p.reshape(MLA_TK // SUBLANES, SUBLANES, MLA_TQ), axis=0)
        den[t] = psum if c == 0 else den[t] + psum
        p = p.astype(jnp.bfloat16)
        pv = jnp.dot(vt_ref[0, 0, 0:MLA_V, chunk_slice(c)], p,
                     preferred_element_type=jnp.float32)
        acc[t] = pv if c == 0 else acc[t] + pv
        pmax = jnp.max(p.reshape(MLA_TK // BF16_SUBLANES, BF16_SUBLANES, MLA_TQ), axis=0)
        seen[t] = pmax if seen[t] is None else jnp.maximum(seen[t], pmax)
        last = c + 1 == n_chunks
        if last or c == 0 or (c + 1) % MLA_REFRESH == 0:
            excess = seen[t] if excess is None else jnp.maximum(excess, seen[t])
            if last:
                acc_ref[t, 0:MLA_V, :] = acc[t]
                acc_ref[t, MLA_V:MLA_V + 1, :] = jnp.sum(den[t], axis=0, keepdims=True)
            else:
                top = jnp.max(seen[t].astype(jnp.float32), axis=0, keepdims=True)
                rise = jnp.maximum(jnp.log2(top), 0.0)
                alpha = jnp.exp2(-rise)
                acc[t] = acc[t] * alpha
                den[t] = den[t] * alpha
                m_use[t] = m_use[t] + rise
                seen[t] = None

    @pl.when(jnp.max(excess.astype(jnp.float32)) > 2.0 ** MLA_MAX_EXCESS)
    def _():
        for t in range(n_tiles):
            m_ref[...] = jnp.full(m_ref.shape, -jnp.inf, jnp.float32)
            acc_ref[t] = jnp.zeros(acc_ref.shape[1:], jnp.float32)

            def chunk(c, carry, t=t):
                s = jnp.dot(key_chunk(c), q_tile(t), preferred_element_type=jnp.float32)
                m_prev = m_ref[...]
                m_new = jnp.maximum(m_prev, jnp.max(s, axis=0, keepdims=True))
                p = jnp.exp2(s - m_new).astype(jnp.bfloat16)
                acc_ref[t] = jnp.exp2(m_prev - m_new) * acc_ref[t] + jnp.dot(
                    value_chunk(c), p, preferred_element_type=jnp.float32)
                m_ref[...] = m_new
                return carry

            lax.fori_loop(0, n_chunks, chunk, 0)

    for t in range(n_tiles):
        cols = slice(t * MLA_TQ, (t + 1) * MLA_TQ)
        o = acc_ref[t, 0:MLA_V, :] / acc_ref[t, MLA_V:MLA_V + 1, :]
        o_ref[0, :, cols] = o.astype(o_ref.dtype)


def _mla_attention(q_t, k, v_t):
    B, H, _, S = q_t.shape
    TQ = MLA_TQ
    step_q = MLA_STEP_TILES * TQ
    grid = (B, H, S // step_q)
    return pl.pallas_call(
        _mla_kernel,
        grid=grid,
        in_specs=[
            pl.BlockSpec((1, 1, QK_PAD, step_q), lambda b, h, i: (b, h, 0, i)),
            pl.BlockSpec((1, 1, S, QK_PAD), lambda b, h, i: (b, h, 0, 0)),
            pl.BlockSpec((1, 1, MLA_V_ROWS, S), lambda b, h, i: (b, h, 0, 0)),
        ],
        out_specs=pl.BlockSpec((1, MLA_V, step_q), lambda b, h, i: (b, h, i)),
        out_shape=jax.ShapeDtypeStruct((B, D_MLA, S), jnp.bfloat16),
        scratch_shapes=[pltpu.VMEM((1, TQ), jnp.float32),
                        pltpu.VMEM((MLA_STEP_TILES, MLA_V_ROWS, TQ), jnp.float32)],
        compiler_params=pltpu.CompilerParams(
            dimension_semantics=("arbitrary", "arbitrary", "arbitrary"),
            vmem_limit_bytes=VMEM_LIMIT_BYTES),
        name="mla",
    )(q_t, k, v_t)


def _bias_kernel(rel_ref, sink_ref, bucket_ref, o_ref):
    g = pl.program_id(0)
    for part in range(3):
        bucket = bucket_ref[part]
        for hh in range(SWA_GROUP):
            head = g * SWA_GROUP + hh
            tile = jnp.full((BLOCK, BLOCK), -jnp.inf, jnp.float32)
            for b in range(N_BUCKETS):
                tile = jnp.where(bucket == b, (rel_ref[b, head] - sink_ref[head]) * LOG2_E, tile)
            o_ref[0, part, :, hh * BLOCK:(hh + 1) * BLOCK] = tile


def _bias_table(rel_bias, sink, bucket):
    return pl.pallas_call(
        _bias_kernel,
        grid=(SWA_KV_HEADS,),
        in_specs=[pl.BlockSpec(memory_space=pltpu.SMEM),
                  pl.BlockSpec(memory_space=pltpu.SMEM),
                  pl.BlockSpec((3, BLOCK, BLOCK), lambda g: (0, 0, 0))],
        out_specs=pl.BlockSpec((1, 3, BLOCK, SWA_GROUP * BLOCK), lambda g: (g, 0, 0, 0)),
        out_shape=jax.ShapeDtypeStruct((SWA_KV_HEADS, 3, BLOCK, SWA_GROUP * BLOCK), jnp.float32),
        compiler_params=pltpu.CompilerParams(dimension_semantics=("arbitrary",)),
        name="bias",
    )(rel_bias, sink, bucket)


def _swa_kernel(qst_ref, ks_ref, vst_ref, bias_ref, o_ref):
    n_total = ks_ref.shape[1] // BLOCK
    kv_head = pl.program_id(1)
    n_local = o_ref.shape[2] // BLOCK
    step = pl.program_id(2)
    neg_inf = jnp.float32(-jnp.inf)

    def band_starts(j):
        n = step * n_local + j
        return tuple(pl.multiple_of(b * BLOCK, BLOCK)
                     for b in (jnp.maximum(n - 1, 0), n, jnp.minimum(n + 1, n_total - 1)))

    def lane_block(j, width):
        start = j * width
        return pl.ds(start if isinstance(j, int) else pl.multiple_of(start, width), width)

    def band_scores(j):
        q_t = qst_ref[0, 0, :, lane_block(j, SWA_GROUP * BLOCK)]
        zeros = jnp.zeros_like(q_t)
        q_t = jnp.where(kv_head == 0, jnp.concatenate([q_t, zeros], axis=0),
                        jnp.concatenate([zeros, q_t], axis=0))
        n = step * n_local + j
        penalties = [None, None, None]
        if not isinstance(j, int) or j == 0:
            penalties[0] = jnp.where(n > 0, 0.0, neg_inf)
        if not isinstance(j, int) or j == n_local - 1:
            penalties[2] = jnp.where(n < n_total - 1, 0.0, neg_inf)
        scores = []
        for part, start in enumerate(band_starts(j)):
            k = ks_ref[0, pl.ds(start, BLOCK), :]
            s = jnp.dot(k, q_t, preferred_element_type=jnp.float32) + bias_ref[0, part]
            if penalties[part] is not None:
                s = s + penalties[part]
            scores.append(s)
        return scores

    def weighted_values(j, weights, n_rows):
        acc = None
        for part, start in enumerate(band_starts(j)):
            v_t = vst_ref[0, 0, 0:n_rows, pl.ds(start, BLOCK)]
            pv = jnp.dot(v_t, weights[part], preferred_element_type=jnp.float32)
            acc = pv if acc is None else acc + pv
        return acc

    def write_block(j, values, denom):
        o = values / denom
        cols = lane_block(j, BLOCK)
        for hh in range(SWA_GROUP):
            rows = slice(hh * SWA_HEAD_DIM, (hh + 1) * SWA_HEAD_DIM)
            o_ref[0, rows, cols] = o[:, hh * BLOCK:(hh + 1) * BLOCK].astype(o_ref.dtype)

    excess = None
    pending = [band_scores(j) for j in range(SWA_AHEAD)]
    for j in range(n_local):
        if j + SWA_AHEAD < n_local:
            pending.append(band_scores(j + SWA_AHEAD))
        weights, den = [], None
        for s in pending.pop(0):
            w = jnp.exp2(s)
            wsum = jnp.sum(w.reshape(BLOCK // SUBLANES, SUBLANES, w.shape[1]), axis=0)
            den = wsum if den is None else den + wsum
            w = w.astype(jnp.bfloat16)
            wmax = jnp.max(w.reshape(BLOCK // BF16_SUBLANES, BF16_SUBLANES, w.shape[1]), axis=0)
            excess = wmax if excess is None else jnp.maximum(excess, wmax)
            weights.append(w)
        write_block(j, weighted_values(j, weights, SWA_HEAD_DIM),
                    jnp.sum(den, axis=0, keepdims=True) + 1.0)

    @pl.when(jnp.max(excess.astype(jnp.float32)) > 2.0 ** SWA_MAX_EXCESS)
    def _():
        def exact_block(j, carry):
            scores = band_scores(j)
            m = jnp.zeros((1, scores[0].shape[1]), jnp.float32)
            for s in scores:
                m = jnp.maximum(m, jnp.max(s, axis=0, keepdims=True))
            acc = weighted_values(j, [jnp.exp2(s - m).astype(jnp.bfloat16) for s in scores],
                                  SWA_V_ROWS)
            write_block(j, acc[0:SWA_HEAD_DIM], acc[SWA_HEAD_DIM:SWA_HEAD_DIM + 1] + jnp.exp2(-m))
            return carry

        lax.fori_loop(0, n_local, exact_block, 0)


def _swa_attention(qs_t, ks, vs_t, bias_t):
    B, G, _, S = vs_t.shape
    T = SWA_TOKENS
    grid = (B, G, S // T)
    rows = SWA_GROUP * SWA_HEAD_DIM
    return pl.pallas_call(
        _swa_kernel,
        grid=grid,
        in_specs=[
            pl.BlockSpec((1, 1, SWA_HEAD_DIM, SWA_GROUP * T), lambda b, g, t: (b, g, 0, t)),
            pl.BlockSpec((1, S, SWA_KV_HEADS * SWA_HEAD_DIM), lambda b, g, t: (b, 0, 0)),
            pl.BlockSpec((1, 1, SWA_V_ROWS, S), lambda b, g, t: (b, g, 0, 0)),
            pl.BlockSpec((1, 3, BLOCK, SWA_GROUP * BLOCK), lambda b, g, t: (g, 0, 0, 0)),
        ],
        out_specs=pl.BlockSpec((1, rows, T), lambda b, g, t: (b, g, t)),
        out_shape=jax.ShapeDtypeStruct((B, D_SWA, S), jnp.bfloat16),
        compiler_params=pltpu.CompilerParams(
            dimension_semantics=("arbitrary", "arbitrary", "arbitrary"),
            vmem_limit_bytes=VMEM_LIMIT_BYTES),
        name="swa",
    )(qs_t, ks, vs_t, bias_t)


def _out_kernel(x_ref, oa_ref, ob_ref, wga_ref, wgb_ref, wa_ref, wb_ref, g_ref, b_ref, y_ref):
    n_chunks = x_ref.shape[1] // OUT_ROWS

    def gate_paths(c):
        xb = x_ref[0, c * OUT_ROWS:(c + 1) * OUT_ROWS, :].astype(jnp.bfloat16)
        return [jnp.dot(xb, wg_ref[...], preferred_element_type=jnp.float32)
                for wg_ref in (wga_ref, wgb_ref)]

    def project(c, gates):
        toks = slice(c * OUT_ROWS, (c + 1) * OUT_ROWS)
        out = None
        for o_ref_, gate, w_ref in ((oa_ref, gates[0], wa_ref), (ob_ref, gates[1], wb_ref)):
            o_nat = o_ref_[0, :, toks].astype(jnp.float32).T
            mixed = (o_nat * jax.nn.silu(gate)).astype(jnp.bfloat16)
            part = jnp.dot(mixed, w_ref[...], preferred_element_type=jnp.float32)
            out = part if out is None else out + part
        return out

    def residual_norm(c, out):
        rows = slice(c * OUT_ROWS, (c + 1) * OUT_ROWS)
        h = ALPHA * x_ref[0, rows, :] + out
        mu = jnp.mean(h, axis=-1, keepdims=True)
        var = jnp.mean(jnp.square(h - mu), axis=-1, keepdims=True)
        y_ref[0, rows, :] = ((h - mu) * lax.rsqrt(var + 1e-5)) * g_ref[...] + b_ref[...]

    gates_next = gate_paths(0)
    out_prev = None
    for c in range(n_chunks):
        gates = gates_next
        if c + 1 < n_chunks:
            gates_next = gate_paths(c + 1)
        out = project(c, gates)
        if out_prev is not None:
            residual_norm(c - 1, out_prev)
        out_prev = out
    residual_norm(n_chunks - 1, out_prev)


def _out_project(x, o_a_t, o_b_t, w_gate_a, w_gate_b, w_a, w_b, ln_g, ln_b):
    B, S, D = x.shape
    T = OUT_TOKENS
    grid = (B, S // T)
    const = lambda *shape: pl.BlockSpec(shape, lambda b, t: (0,) * len(shape))
    return pl.pallas_call(
        _out_kernel,
        grid=grid,
        in_specs=[
            pl.BlockSpec((1, T, D), lambda b, t: (b, t, 0)),
            pl.BlockSpec((1, D_MLA, T), lambda b, t: (b, 0, t)),
            pl.BlockSpec((1, D_SWA, T), lambda b, t: (b, 0, t)),
            const(D, D_MLA),
            const(D, D_SWA),
            const(D_MLA, D),
            const(D_SWA, D),
            const(1, D),
            const(1, D),
        ],
        out_specs=pl.BlockSpec((1, T, D), lambda b, t: (b, t, 0)),
        out_shape=jax.ShapeDtypeStruct((B, S, D), x.dtype),
        compiler_params=pltpu.CompilerParams(
            dimension_semantics=("arbitrary", "arbitrary"),
            allow_input_fusion=[False, False, False, True, True, True, True, False, False],
            vmem_limit_bytes=VMEM_LIMIT_BYTES),
        name="outproj",
    )(x, o_a_t, o_b_t, w_gate_a, w_gate_b, w_a, w_b, ln_g, ln_b)


def _t5_bucket(rel):
    half = N_BUCKETS // 2
    ret = np.where(rel > 0, half, 0)
    n = np.abs(rel)
    max_exact = half // 2
    large = max_exact + (np.log(np.maximum(n, 1).astype(np.float32) / max_exact)
                         / np.log(MAX_DISTANCE / max_exact) * (half - max_exact)).astype(np.int32)
    large = np.minimum(large, half - 1)
    return (ret + np.where(n < max_exact, n, large)).astype(np.int32)


def _band_geometry():
    q_loc = np.arange(BLOCK)
    k_loc = np.arange(3 * BLOCK) - BLOCK
    rel = k_loc[:, None] - q_loc[None, :]
    return _t5_bucket(rel), np.abs(rel) <= WINDOW


def _col(w_in, i):
    return w_in[:, IN_OFFSETS[i]:IN_OFFSETS[i + 1]]


def kernel(x, w_in, g_q, g_kv, w_uq, w_ukv, sink, rel_bias, w_out, ln_g, ln_b):
    B, S, _ = x.shape
    bf = jnp.bfloat16
    f32 = jnp.float32

    pos = jnp.arange(S, dtype=f32)
    inv_freq = ROPE_BASE ** (-jnp.arange(0, MLA_ROPE, 2, dtype=f32) / MLA_ROPE)
    ang_t = inv_freq[:, None] * pos[None, :]
    cos_t, sin_t = jnp.cos(ang_t), jnp.sin(ang_t)

    c_q, c_kv, k_rope, gate_a, q_s, k_s, v_s, gate_b = (_col(w_in, i) for i in range(8))
    w_tr = jnp.concatenate([c_q, q_s, v_s, k_rope], axis=1).T.astype(bf)
    w_nat = jnp.concatenate([c_kv, k_s], axis=1).astype(bf)
    w_ukv3 = w_ukv.reshape(KV_LORA, MLA_HEADS, MLA_NOPE + MLA_V)
    w_uk = w_ukv3[:, :, :MLA_NOPE].reshape(KV_LORA, MLA_HEADS * MLA_NOPE).astype(bf)
    w_uv_t = w_ukv3[:, :, MLA_NOPE:].reshape(KV_LORA, D_MLA).T.astype(bf)
    w_uq_t = w_uq.T.astype(bf)

    q_t, k, v_t, qs_t, ks, vs_t = _project(
        x, w_tr, w_nat, g_q.reshape(Q_LORA, 1), g_kv.reshape(1, KV_LORA),
        w_uq_t, w_uk, w_uv_t, cos_t, sin_t)

    o_a_t = _mla_attention(q_t, k, v_t)

    bucket, band = _band_geometry()
    bucket = jnp.asarray(np.where(band, bucket, -1).reshape(3, BLOCK, BLOCK), jnp.int32)
    bias_t = _bias_table(rel_bias.astype(f32), sink.astype(f32), bucket)

    o_b_t = _swa_attention(qs_t, ks, vs_t, bias_t)

    w_out_bf = w_out.astype(bf)
    return _out_project(x, o_a_t, o_b_t, gate_a.astype(bf), gate_b.astype(bf),
                        w_out_bf[:D_MLA], w_out_bf[D_MLA:],
                        ln_g.reshape(1, D_MODEL), ln_b.reshape(1, D_MODEL))
```

```python
import math

import jax
import jax.numpy as jnp
import numpy as np
from jax import lax
from jax.experimental import pallas as pl
from jax.experimental.pallas import tpu as pltpu

D_MODEL = 1024
MLA_HEADS = 8
MLA_NOPE = 64
MLA_ROPE = 32
MLA_V = 64
MLA_QK = MLA_NOPE + MLA_ROPE
Q_LORA = 256
KV_LORA = 128
D_MLA = MLA_HEADS * MLA_V
MLA_SCALE = 1.0 / math.sqrt(MLA_QK)
LOG2_E = math.log2(math.e)
MLA_Q_SCALE = MLA_SCALE * LOG2_E
BF16_SUBLANES = 16
MLA_V_ROWS = MLA_V + BF16_SUBLANES
ROPE_BASE = 10000.0

SWA_HEADS = 8
SWA_KV_HEADS = 2
SWA_HEAD_DIM = 64
SWA_GROUP = SWA_HEADS // SWA_KV_HEADS
D_SWA = SWA_HEADS * SWA_HEAD_DIM
WINDOW = 128
BLOCK = 128
SWA_SCALE = 1.0 / math.sqrt(SWA_HEAD_DIM)
SWA_Q_SCALE = SWA_SCALE * LOG2_E
SWA_V_ROWS = SWA_HEAD_DIM + BF16_SUBLANES
N_BUCKETS = 32
MAX_DISTANCE = 128

DEPTH = 1
ALPHA = (2.0 * DEPTH) ** 0.25

IN_SPLITS = (Q_LORA, KV_LORA, MLA_ROPE, D_MLA, D_SWA,
             SWA_KV_HEADS * SWA_HEAD_DIM, SWA_KV_HEADS * SWA_HEAD_DIM, D_SWA)
IN_OFFSETS = tuple(int(o) for o in np.cumsum((0,) + IN_SPLITS))

LANE = 128
SUBLANES = 8
QK_PAD = LANE
VMEM_LIMIT_BYTES = 56 * 1024 * 1024

PROJ_TOKENS = 1024
PROJ_PARTS = 4
MLA_TQ = 512
MLA_STEP_TILES = 4
MLA_TK = 256
MLA_AHEAD = 2
MLA_REFRESH = 8
MLA_MAX_EXCESS = 64.0
SWA_TOKENS = 4096
SWA_AHEAD = 1
SWA_MAX_EXCESS = 64.0
OUT_TOKENS = 1024
OUT_ROWS = 256

TR_CQ = 0
TR_QS = TR_CQ + Q_LORA
TR_VS = TR_QS + D_SWA
TR_KR = TR_VS + SWA_KV_HEADS * SWA_HEAD_DIM
TR_ROWS = TR_KR + MLA_ROPE
NAT_CKV = 0
NAT_KS = NAT_CKV + KV_LORA
NAT_COLS = NAT_KS + SWA_KV_HEADS * SWA_HEAD_DIM

_NT = (((1,), (1,)), ((), ()))


def _rsqrt_mean_sq(x, axis, eps):
    return lax.rsqrt(jnp.mean(x * x, axis=axis, keepdims=True) + eps)


def _proj_kernel(x_ref, wtr_ref, wnat_ref, gq_ref, gkv_ref, wuqt_ref, wuk_ref, wuvt_ref,
                 cos_t_ref, sin_t_ref,
                 qt_ref, k_ref, vt_ref, qst_ref, ks_ref, vst_ref,
                 tr_ref, nat_ref):
    n_tok = x_ref.shape[1] // PROJ_PARTS

    def project(p):
        tok = slice(p * n_tok, (p + 1) * n_tok)
        xb = x_ref[0, tok, :].astype(jnp.bfloat16)
        tr_ref[:, tok] = lax.dot_general(wtr_ref[...], xb, _NT, preferred_element_type=jnp.float32)
        nat_ref[tok, :] = jnp.dot(xb, wnat_ref[...], preferred_element_type=jnp.float32)

    project(0)
    for p in range(PROJ_PARTS):
        if p + 1 < PROJ_PARTS:
            project(p + 1)
        _proj_finish(p, n_tok, tr_ref, nat_ref, gq_ref, gkv_ref, wuqt_ref, wuk_ref, wuvt_ref,
                     cos_t_ref, sin_t_ref,
                     qt_ref, k_ref, vt_ref, qst_ref, ks_ref, vst_ref)


def _proj_finish(p, n_tok, tr_ref, nat_ref, gq_ref, gkv_ref, wuqt_ref, wuk_ref, wuvt_ref,
                 cos_t_ref, sin_t_ref,
                 qt_ref, k_ref, vt_ref, qst_ref, ks_ref, vst_ref):
    tok = slice(p * n_tok, (p + 1) * n_tok)
    half = MLA_ROPE // 2

    cq = tr_ref[TR_CQ:TR_CQ + Q_LORA, tok]
    cqn = (cq * _rsqrt_mean_sq(cq, 0, 1e-6)) * gq_ref[...]
    q_t = jnp.dot(wuqt_ref[...], cqn.astype(jnp.bfloat16),
                  preferred_element_type=jnp.float32)
    cos_t = cos_t_ref[:, tok]
    sin_t = sin_t_ref[:, tok]
    zeros_pad = jnp.zeros((QK_PAD - MLA_QK, n_tok), jnp.bfloat16)
    for h in range(MLA_HEADS):
        base = h * MLA_QK
        nope = q_t[base:base + MLA_NOPE]
        r1 = q_t[base + MLA_NOPE:base + MLA_NOPE + half]
        r2 = q_t[base + MLA_NOPE + half:base + MLA_QK]
        qt_ref[0, h, 0, 0:MLA_NOPE, tok] = (nope * MLA_Q_SCALE).astype(jnp.bfloat16)
        qt_ref[0, h, 0, MLA_NOPE:MLA_NOPE + half, tok] = (
            (r1 * cos_t - r2 * sin_t) * MLA_Q_SCALE).astype(jnp.bfloat16)
        qt_ref[0, h, 0, MLA_NOPE + half:MLA_QK, tok] = (
            (r2 * cos_t + r1 * sin_t) * MLA_Q_SCALE).astype(jnp.bfloat16)
        qt_ref[0, h, 0, MLA_QK:QK_PAD, tok] = zeros_pad

    ckv = nat_ref[tok, NAT_CKV:NAT_CKV + KV_LORA]
    kvn = ((ckv * _rsqrt_mean_sq(ckv, 1, 1e-6)) * gkv_ref[...]).astype(jnp.bfloat16)
    k_nope = jnp.dot(kvn, wuk_ref[...], preferred_element_type=jnp.float32)
    v_t = lax.dot_general(wuvt_ref[...], kvn, _NT, preferred_element_type=jnp.float32)
    kr1 = tr_ref[TR_KR:TR_KR + half, tok]
    kr2 = tr_ref[TR_KR + half:TR_KR + MLA_ROPE, tok]
    k_rope = jnp.concatenate(
        [jnp.zeros((MLA_NOPE, n_tok), jnp.float32),
         kr1 * cos_t - kr2 * sin_t,
         kr2 * cos_t + kr1 * sin_t,
         jnp.zeros((QK_PAD - MLA_QK, n_tok), jnp.float32)], axis=0).T
    lane = lax.broadcasted_iota(jnp.int32, k_rope.shape, 1)
    row_id = lax.broadcasted_iota(jnp.int32, (BF16_SUBLANES, n_tok), 0)
    ones_row = jnp.where(row_id == 0, 1.0, 0.0).astype(jnp.bfloat16)
    for h in range(MLA_HEADS):
        pair = k_nope[:, (h // 2) * LANE:(h // 2 + 1) * LANE]
        if h % 2:
            pair = pltpu.roll(pair, MLA_NOPE, axis=1)
        k_ref[0, h, tok, :] = jnp.where(lane < MLA_NOPE, pair, k_rope).astype(jnp.bfloat16)
        vt_ref[0, h, 0, 0:MLA_V, tok] = v_t[h * MLA_V:(h + 1) * MLA_V].astype(jnp.bfloat16)
        vt_ref[0, h, 0, MLA_V:MLA_V_ROWS, tok] = ones_row

    n_blocks = n_tok // BLOCK
    ks_ref[0, tok, :] = nat_ref[tok, NAT_KS:NAT_KS + SWA_KV_HEADS * SWA_HEAD_DIM].astype(jnp.bfloat16)
    for g in range(SWA_KV_HEADS):
        vst_ref[0, g, 0:SWA_HEAD_DIM, tok] = tr_ref[TR_VS + g * SWA_HEAD_DIM:
                                                    TR_VS + (g + 1) * SWA_HEAD_DIM, tok].astype(jnp.bfloat16)
        vst_ref[0, g, SWA_HEAD_DIM:SWA_V_ROWS, tok] = ones_row
        for hh in range(SWA_GROUP):
            row = TR_QS + (g * SWA_GROUP + hh) * SWA_HEAD_DIM
            q_h = (tr_ref[row:row + SWA_HEAD_DIM, tok] * SWA_Q_SCALE).astype(jnp.bfloat16)
            for nb in range(n_blocks):
                col = ((p * n_blocks + nb) * SWA_GROUP + hh) * BLOCK
                qst_ref[0, g, :, col:col + BLOCK] = q_h[:, nb * BLOCK:(nb + 1) * BLOCK]


def _project(x, w_tr, w_nat, g_q, g_kv, w_uq_t, w_uk, w_uv_t, cos_t, sin_t):
    B, S, _ = x.shape
    T = PROJ_TOKENS
    grid = (B, S // T)
    const = lambda *shape: pl.BlockSpec(shape, lambda b, t: (0,) * len(shape))
    bf = jnp.bfloat16
    out_shape = (
        jax.ShapeDtypeStruct((B, MLA_HEADS, S // T, QK_PAD, T), bf),
        jax.ShapeDtypeStruct((B, MLA_HEADS, S, QK_PAD), bf),
        jax.ShapeDtypeStruct((B, MLA_HEADS, S // T, MLA_V_ROWS, T), bf),
        jax.ShapeDtypeStruct((B, SWA_KV_HEADS, SWA_HEAD_DIM, SWA_GROUP * S), bf),
        jax.ShapeDtypeStruct((B, S, SWA_KV_HEADS * SWA_HEAD_DIM), bf),
        jax.ShapeDtypeStruct((B, SWA_KV_HEADS, SWA_V_ROWS, S), bf),
    )
    out_specs = (
        pl.BlockSpec((1, MLA_HEADS, 1, QK_PAD, T), lambda b, t: (b, 0, t, 0, 0)),
        pl.BlockSpec((1, MLA_HEADS, T, QK_PAD), lambda b, t: (b, 0, t, 0)),
        pl.BlockSpec((1, MLA_HEADS, 1, MLA_V_ROWS, T), lambda b, t: (b, 0, t, 0, 0)),
        pl.BlockSpec((1, SWA_KV_HEADS, SWA_HEAD_DIM, SWA_GROUP * T), lambda b, t: (b, 0, 0, t)),
        pl.BlockSpec((1, T, SWA_KV_HEADS * SWA_HEAD_DIM), lambda b, t: (b, t, 0)),
        pl.BlockSpec((1, SWA_KV_HEADS, SWA_V_ROWS, T), lambda b, t: (b, 0, 0, t)),
    )
    in_specs = [
        pl.BlockSpec((1, T, D_MODEL), lambda b, t: (b, t, 0)),
        const(TR_ROWS, D_MODEL),
        const(D_MODEL, NAT_COLS),
        const(Q_LORA, 1),
        const(1, KV_LORA),
        const(MLA_HEADS * MLA_QK, Q_LORA),
        const(KV_LORA, MLA_HEADS * MLA_NOPE),
        const(D_MLA, KV_LORA),
        pl.BlockSpec((MLA_ROPE // 2, T), lambda b, t: (0, t)),
        pl.BlockSpec((MLA_ROPE // 2, T), lambda b, t: (0, t)),
    ]
    return pl.pallas_call(
        _proj_kernel,
        grid=grid,
        in_specs=in_specs,
        out_specs=out_specs,
        out_shape=out_shape,
        scratch_shapes=[pltpu.VMEM((TR_ROWS, T), jnp.float32),
                        pltpu.VMEM((T, NAT_COLS), jnp.float32)],
        compiler_params=pltpu.CompilerParams(
            dimension_semantics=("arbitrary", "arbitrary"),
            vmem_limit_bytes=VMEM_LIMIT_BYTES),
        name="proj",
    )(x, w_tr, w_nat, g_q, g_kv, w_uq_t, w_uk, w_uv_t, cos_t, sin_t)


def _mla_kernel(qt_ref, k_ref, vt_ref, o_ref, m_ref, acc_ref):
    tile_tokens = qt_ref.shape[4]
    n_chunks = k_ref.shape[2] // MLA_TK
    n_tiles = qt_ref.shape[2] * tile_tokens // MLA_TQ
    per_tile_q = tile_tokens // MLA_TQ
    per_tile_k = tile_tokens // MLA_TK

    def q_tile(t):
        off = (t % per_tile_q) * MLA_TQ
        return qt_ref[0, 0, t // per_tile_q, :, off:off + MLA_TQ]

    def chunk_slice(c):
        start = c * MLA_TK
        return pl.ds(start if isinstance(c, int) else pl.multiple_of(start, MLA_TK), MLA_TK)

    def key_chunk(c):
        return k_ref[0, 0, chunk_slice(c), :]

    def value_chunk(c, n_rows=MLA_V_ROWS):
        if isinstance(c, int):
            tile, off = c // per_tile_k, (c % per_tile_k) * MLA_TK
        else:
            tile = lax.div(c, per_tile_k)
            off = pl.multiple_of(lax.rem(c, per_tile_k) * MLA_TK, MLA_TK)
        return vt_ref[0, 0, tile, 0:n_rows, pl.ds(off, MLA_TK)]

    items = [(t, c) for t in range(n_tiles) for c in range(n_chunks)]
    m_use = [None] * n_tiles
    seen = [None] * n_tiles
    acc = [None] * n_tiles
    den = [None] * n_tiles
    excess = None

    def issue_scores(i):
        t, c = items[i]
        if c == 0:
            s0 = jnp.dot(k_ref[0, 0, 0:BF16_SUBLANES, :], q_tile(t),
                         preferred_element_type=jnp.float32)
            m_use[t] = jnp.max(s0, axis=0, keepdims=True)
        return jnp.dot(key_chunk(c), q_tile(t), preferred_element_type=jnp.float32)

    pending = [issue_scores(i) for i in range(MLA_AHEAD)]
    for i, (t, c) in enumerate(items):
        if i + MLA_AHEAD < len(items):
            pending.append(issue_scores(i + MLA_AHEAD))
        s = pending.pop(0)
        p = jnp.exp2(s - m_use[t])
        psum = jnp.sum(p.reshape(MLA_TK // SUBLANES, SUBLANES, MLA_TQ), axis=0)
        den[t] = psum if c == 0 else den[t] + psum
        p = p.astype(jnp.bfloat16)
        pv = jnp.dot(value_chunk(c, MLA_V), p,
                     preferred_element_type=jnp.float32)
        acc[t] = pv if c == 0 else acc[t] + pv
        pmax = jnp.max(p.reshape(MLA_TK // BF16_SUBLANES, BF16_SUBLANES, MLA_TQ), axis=0)
        seen[t] = pmax if seen[t] is None else jnp.maximum(seen[t], pmax)
        last = c + 1 == n_chunks
        if last or c == 0 or (c + 1) % MLA_REFRESH == 0:
            excess = seen[t] if excess is None else jnp.maximum(excess, seen[t])
            if last:
                acc_ref[t, 0:MLA_V, :] = acc[t]
                acc_ref[t, MLA_V:MLA_V + 1, :] = jnp.sum(den[t], axis=0, keepdims=True)
            else:
                top = jnp.max(seen[t].astype(jnp.float32), axis=0, keepdims=True)
                rise = jnp.maximum(jnp.log2(top), 0.0)
                alpha = jnp.exp2(-rise)
                acc[t] = acc[t] * alpha
                den[t] = den[t] * alpha
                m_use[t] = m_use[t] + rise
                seen[t] = None

    @pl.when(jnp.max(excess.astype(jnp.float32)) > 2.0 ** MLA_MAX_EXCESS)
    def _():
        for t in range(n_tiles):
            m_ref[...] = jnp.full(m_ref.shape, -jnp.inf, jnp.float32)
            acc_ref[t] = jnp.zeros(acc_ref.shape[1:], jnp.float32)

            def chunk(c, carry, t=t):
                s = jnp.dot(key_chunk(c), q_tile(t), preferred_element_type=jnp.float32)
                m_prev = m_ref[...]
                m_new = jnp.maximum(m_prev, jnp.max(s, axis=0, keepdims=True))
                p = jnp.exp2(s - m_new).astype(jnp.bfloat16)
                acc_ref[t] = jnp.exp2(m_prev - m_new) * acc_ref[t] + jnp.dot(
                    value_chunk(c), p, preferred_element_type=jnp.float32)
                m_ref[...] = m_new
                return carry

            lax.fori_loop(0, n_chunks, chunk, 0)

    for t in range(n_tiles):
        cols = slice(t * MLA_TQ, (t + 1) * MLA_TQ)
        o = acc_ref[t, 0:MLA_V, :] / acc_ref[t, MLA_V:MLA_V + 1, :]
        o_ref[0, :, cols] = o.astype(o_ref.dtype)


def _mla_attention(q_t, k, v_t):
    B, H, n_token_tiles, _, T = q_t.shape
    S = n_token_tiles * T
    TQ = MLA_TQ
    step_q = MLA_STEP_TILES * TQ
    step_tiles = step_q // T
    grid = (B, H, S // step_q)
    return pl.pallas_call(
        _mla_kernel,
        grid=grid,
        in_specs=[
            pl.BlockSpec((1, 1, step_tiles, QK_PAD, T), lambda b, h, i: (b, h, i, 0, 0)),
            pl.BlockSpec((1, 1, S, QK_PAD), lambda b, h, i: (b, h, 0, 0)),
            pl.BlockSpec((1, 1, n_token_tiles, MLA_V_ROWS, T), lambda b, h, i: (b, h, 0, 0, 0)),
        ],
        out_specs=pl.BlockSpec((1, MLA_V, step_q), lambda b, h, i: (b, h, i)),
        out_shape=jax.ShapeDtypeStruct((B, D_MLA, S), jnp.bfloat16),
        scratch_shapes=[pltpu.VMEM((1, TQ), jnp.float32),
                        pltpu.VMEM((MLA_STEP_TILES, MLA_V_ROWS, TQ), jnp.float32)],
        compiler_params=pltpu.CompilerParams(
            dimension_semantics=("arbitrary", "arbitrary", "arbitrary"),
            vmem_limit_bytes=VMEM_LIMIT_BYTES),
        name="mla",
    )(q_t, k, v_t)


def _bias_kernel(rel_ref, sink_ref, bucket_ref, o_ref):
    g = pl.program_id(0)
    for part in range(3):
        bucket = bucket_ref[part]
        for hh in range(SWA_GROUP):
            head = g * SWA_GROUP + hh
            tile = jnp.full((BLOCK, BLOCK), -jnp.inf, jnp.float32)
            for b in range(N_BUCKETS):
                tile = jnp.where(bucket == b, (rel_ref[b, head] - sink_ref[head]) * LOG2_E, tile)
            o_ref[0, part, :, hh * BLOCK:(hh + 1) * BLOCK] = tile


def _bias_table(rel_bias, sink, bucket):
    return pl.pallas_call(
        _bias_kernel,
        grid=(SWA_KV_HEADS,),
        in_specs=[pl.BlockSpec(memory_space=pltpu.SMEM),
                  pl.BlockSpec(memory_space=pltpu.SMEM),
                  pl.BlockSpec((3, BLOCK, BLOCK), lambda g: (0, 0, 0))],
        out_specs=pl.BlockSpec((1, 3, BLOCK, SWA_GROUP * BLOCK), lambda g: (g, 0, 0, 0)),
        out_shape=jax.ShapeDtypeStruct((SWA_KV_HEADS, 3, BLOCK, SWA_GROUP * BLOCK), jnp.float32),
        compiler_params=pltpu.CompilerParams(dimension_semantics=("arbitrary",)),
        name="bias",
    )(rel_bias, sink, bucket)


def _swa_kernel(qst_ref, ks_ref, vst_ref, bias_ref, o_ref):
    n_total = ks_ref.shape[1] // BLOCK
    kv_head = pl.program_id(1)
    n_local = o_ref.shape[2] // BLOCK
    step = pl.program_id(2)
    neg_inf = jnp.float32(-jnp.inf)

    def band_starts(j):
        n = step * n_local + j
        return tuple(pl.multiple_of(b * BLOCK, BLOCK)
                     for b in (jnp.maximum(n - 1, 0), n, jnp.minimum(n + 1, n_total - 1)))

    def lane_block(j, width):
        start = j * width
        return pl.ds(start if isinstance(j, int) else pl.multiple_of(start, width), width)

    def band_scores(j):
        q_t = qst_ref[0, 0, :, lane_block(j, SWA_GROUP * BLOCK)]
        zeros = jnp.zeros_like(q_t)
        q_t = jnp.where(kv_head == 0, jnp.concatenate([q_t, zeros], axis=0),
                        jnp.concatenate([zeros, q_t], axis=0))
        n = step * n_local + j
        penalties = [None, None, None]
        if not isinstance(j, int) or j == 0:
            penalties[0] = jnp.where(n > 0, 0.0, neg_inf)
        if not isinstance(j, int) or j == n_local - 1:
            penalties[2] = jnp.where(n < n_total - 1, 0.0, neg_inf)
        scores = []
        for part, start in enumerate(band_starts(j)):
            k = ks_ref[0, pl.ds(start, BLOCK), :]
            s = jnp.dot(k, q_t, preferred_element_type=jnp.float32) + bias_ref[0, part]
            if penalties[part] is not None:
                s = s + penalties[part]
            scores.append(s)
        return scores

    def weighted_values(j, weights, n_rows):
        acc = None
        for part, start in enumerate(band_starts(j)):
            v_t = vst_ref[0, 0, 0:n_rows, pl.ds(start, BLOCK)]
            pv = jnp.dot(v_t, weights[part], preferred_element_type=jnp.float32)
            acc = pv if acc is None else acc + pv
        return acc

    def write_block(j, values, denom):
        o = values / denom
        cols = lane_block(j, BLOCK)
        for hh in range(SWA_GROUP):
            rows = slice(hh * SWA_HEAD_DIM, (hh + 1) * SWA_HEAD_DIM)
            o_ref[0, rows, cols] = o[:, hh * BLOCK:(hh + 1) * BLOCK].astype(o_ref.dtype)

    excess = None
    pending = [band_scores(j) for j in range(SWA_AHEAD)]
    for j in range(n_local):
        if j + SWA_AHEAD < n_local:
            pending.append(band_scores(j + SWA_AHEAD))
        weights, den = [], None
        for s in pending.pop(0):
            w = jnp.exp2(s)
            wsum = jnp.sum(w.reshape(BLOCK // SUBLANES, SUBLANES, w.shape[1]), axis=0)
            den = wsum if den is None else den + wsum
            w = w.astype(jnp.bfloat16)
            wmax = jnp.max(w.reshape(BLOCK // BF16_SUBLANES, BF16_SUBLANES, w.shape[1]), axis=0)
            excess = wmax if excess is None else jnp.maximum(excess, wmax)
            weights.append(w)
        write_block(j, weighted_values(j, weights, SWA_HEAD_DIM),
                    jnp.sum(den, axis=0, keepdims=True) + 1.0)

    @pl.when(jnp.max(excess.astype(jnp.float32)) > 2.0 ** SWA_MAX_EXCESS)
    def _():
        def exact_block(j, carry):
            scores = band_scores(j)
            m = jnp.zeros((1, scores[0].shape[1]), jnp.float32)
            for s in scores:
                m = jnp.maximum(m, jnp.max(s, axis=0, keepdims=True))
            acc = weighted_values(j, [jnp.exp2(s - m).astype(jnp.bfloat16) for s in scores],
                                  SWA_V_ROWS)
            write_block(j, acc[0:SWA_HEAD_DIM], acc[SWA_HEAD_DIM:SWA_HEAD_DIM + 1] + jnp.exp2(-m))
            return carry

        lax.fori_loop(0, n_local, exact_block, 0)


def _swa_attention(qs_t, ks, vs_t, bias_t):
    B, G, _, S = vs_t.shape
    T = SWA_TOKENS
    grid = (B, G, S // T)
    rows = SWA_GROUP * SWA_HEAD_DIM
    return pl.pallas_call(
        _swa_kernel,
        grid=grid,
        in_specs=[
            pl.BlockSpec((1, 1, SWA_HEAD_DIM, SWA_GROUP * T), lambda b, g, t: (b, g, 0, t)),
            pl.BlockSpec((1, S, SWA_KV_HEADS * SWA_HEAD_DIM), lambda b, g, t: (b, 0, 0)),
            pl.BlockSpec((1, 1, SWA_V_ROWS, S), lambda b, g, t: (b, g, 0, 0)),
            pl.BlockSpec((1, 3, BLOCK, SWA_GROUP * BLOCK), lambda b, g, t: (g, 0, 0, 0)),
        ],
        out_specs=pl.BlockSpec((1, rows, T), lambda b, g, t: (b, g, t)),
        out_shape=jax.ShapeDtypeStruct((B, D_SWA, S), jnp.bfloat16),
        compiler_params=pltpu.CompilerParams(
            dimension_semantics=("arbitrary", "arbitrary", "arbitrary"),
            vmem_limit_bytes=VMEM_LIMIT_BYTES),
        name="swa",
    )(qs_t, ks, vs_t, bias_t)


def _out_kernel(x_ref, oa_ref, ob_ref, wga_ref, wgb_ref, wa_ref, wb_ref, g_ref, b_ref, y_ref):
    n_chunks = x_ref.shape[1] // OUT_ROWS

    def gate_paths(c):
        xb = x_ref[0, c * OUT_ROWS:(c + 1) * OUT_ROWS, :].astype(jnp.bfloat16)
        return [jnp.dot(xb, wg_ref[...], preferred_element_type=jnp.float32)
                for wg_ref in (wga_ref, wgb_ref)]

    def project(c, gates):
        toks = slice(c * OUT_ROWS, (c + 1) * OUT_ROWS)
        out = None
        for o_ref_, gate, w_ref in ((oa_ref, gates[0], wa_ref), (ob_ref, gates[1], wb_ref)):
            o_nat = o_ref_[0, :, toks].astype(jnp.float32).T
            mixed = (o_nat * jax.nn.silu(gate)).astype(jnp.bfloat16)
            part = jnp.dot(mixed, w_ref[...], preferred_element_type=jnp.float32)
            out = part if out is None else out + part
        return out

    def residual_norm(c, out):
        rows = slice(c * OUT_ROWS, (c + 1) * OUT_ROWS)
        h = ALPHA * x_ref[0, rows, :] + out
        mu = jnp.mean(h, axis=-1, keepdims=True)
        var = jnp.mean(jnp.square(h - mu), axis=-1, keepdims=True)
        y_ref[0, rows, :] = ((h - mu) * lax.rsqrt(var + 1e-5)) * g_ref[...] + b_ref[...]

    gates_next = gate_paths(0)
    out_prev = None
    for c in range(n_chunks):
        gates = gates_next
        if c + 1 < n_chunks:
            gates_next = gate_paths(c + 1)
        out = project(c, gates)
        if out_prev is not None:
            residual_norm(c - 1, out_prev)
        out_prev = out
    residual_norm(n_chunks - 1, out_prev)


def _out_project(x, o_a_t, o_b_t, w_gate_a, w_gate_b, w_a, w_b, ln_g, ln_b):
    B, S, D = x.shape
    T = OUT_TOKENS
    grid = (B, S // T)
    const = lambda *shape: pl.BlockSpec(shape, lambda b, t: (0,) * len(shape))
    return pl.pallas_call(
        _out_kernel,
        grid=grid,
        in_specs=[
            pl.BlockSpec((1, T, D), lambda b, t: (b, t, 0)),
            pl.BlockSpec((1, D_MLA, T), lambda b, t: (b, 0, t)),
            pl.BlockSpec((1, D_SWA, T), lambda b, t: (b, 0, t)),
            const(D, D_MLA),
            const(D, D_SWA),
            const(D_MLA, D),
            const(D_SWA, D),
            const(1, D),
            const(1, D),
        ],
        out_specs=pl.BlockSpec((1, T, D), lambda b, t: (b, t, 0)),
        out_shape=jax.ShapeDtypeStruct((B, S, D), x.dtype),
        compiler_params=pltpu.CompilerParams(
            dimension_semantics=("arbitrary", "arbitrary"),
            vmem_limit_bytes=VMEM_LIMIT_BYTES),
        name="outproj",
    )(x, o_a_t, o_b_t, w_gate_a, w_gate_b, w_a, w_b, ln_g, ln_b)


def _t5_bucket(rel):
    half = N_BUCKETS // 2
    ret = np.where(rel > 0, half, 0)
    n = np.abs(rel)
    max_exact = half // 2
    large = max_exact + (np.log(np.maximum(n, 1).astype(np.float32) / max_exact)
                         / np.log(MAX_DISTANCE / max_exact) * (half - max_exact)).astype(np.int32)
    large = np.minimum(large, half - 1)
    return (ret + np.where(n < max_exact, n, large)).astype(np.int32)


def _band_geometry():
    q_loc = np.arange(BLOCK)
    k_loc = np.arange(3 * BLOCK) - BLOCK
    rel = k_loc[:, None] - q_loc[None, :]
    return _t5_bucket(rel), np.abs(rel) <= WINDOW


def _col(w_in, i):
    return w_in[:, IN_OFFSETS[i]:IN_OFFSETS[i + 1]]


def kernel(x, w_in, g_q, g_kv, w_uq, w_ukv, sink, rel_bias, w_out, ln_g, ln_b):
    B, S, _ = x.shape
    bf = jnp.bfloat16
    f32 = jnp.float32

    pos = jnp.arange(S, dtype=f32)
    inv_freq = ROPE_BASE ** (-jnp.arange(0, MLA_ROPE, 2, dtype=f32) / MLA_ROPE)
    ang_t = inv_freq[:, None] * pos[None, :]
    cos_t, sin_t = jnp.cos(ang_t), jnp.sin(ang_t)

    c_q, c_kv, k_rope, gate_a, q_s, k_s, v_s, gate_b = (_col(w_in, i) for i in range(8))
    w_tr = jnp.concatenate([c_q, q_s, v_s, k_rope], axis=1).T.astype(bf)
    w_nat = jnp.concatenate([c_kv, k_s], axis=1).astype(bf)
    w_ukv3 = w_ukv.reshape(KV_LORA, MLA_HEADS, MLA_NOPE + MLA_V)
    w_uk = w_ukv3[:, :, :MLA_NOPE].reshape(KV_LORA, MLA_HEADS * MLA_NOPE).astype(bf)
    w_uv_t = w_ukv3[:, :, MLA_NOPE:].reshape(KV_LORA, D_MLA).T.astype(bf)
    w_uq_t = w_uq.T.astype(bf)

    q_t, k, v_t, qs_t, ks, vs_t = _project(
        x, w_tr, w_nat, g_q.reshape(Q_LORA, 1), g_kv.reshape(1, KV_LORA),
        w_uq_t, w_uk, w_uv_t, cos_t, sin_t)

    o_a_t = _mla_attention(q_t, k, v_t)

    bucket, band = _band_geometry()
    bucket = jnp.asarray(np.where(band, bucket, -1).reshape(3, BLOCK, BLOCK), jnp.int32)
    bias_t = _bias_table(rel_bias.astype(f32), sink.astype(f32), bucket)

    o_b_t = _swa_attention(qs_t, ks, vs_t, bias_t)

    w_out_bf = w_out.astype(bf)
    return _out_project(x, o_a_t, o_b_t, gate_a.astype(bf), gate_b.astype(bf),
                        w_out_bf[:D_MLA], w_out_bf[D_MLA:],
                        ln_g.reshape(1, D_MODEL), ln_b.reshape(1, D_MODEL))
```

```python
import math

import jax
import jax.numpy as jnp
import numpy as np
from jax import lax
from jax.experimental import pallas as pl
from jax.experimental.pallas import tpu as pltpu

D_MODEL = 1024
MLA_HEADS = 8
MLA_NOPE = 64
MLA_ROPE = 32
MLA_V = 64
MLA_QK = MLA_NOPE + MLA_ROPE
Q_LORA = 256
KV_LORA = 128
D_MLA = MLA_HEADS * MLA_V
MLA_SCALE = 1.0 / math.sqrt(MLA_QK)
LOG2_E = math.log2(math.e)
MLA_Q_SCALE = MLA_SCALE * LOG2_E
BF16_SUBLANES = 16
MLA_V_ROWS = MLA_V + BF16_SUBLANES
ROPE_BASE = 10000.0

SWA_HEADS = 8
SWA_KV_HEADS = 2
SWA_HEAD_DIM = 64
SWA_GROUP = SWA_HEADS // SWA_KV_HEADS
D_SWA = SWA_HEADS * SWA_HEAD_DIM
WINDOW = 128
BLOCK = 128
SWA_SCALE = 1.0 / math.sqrt(SWA_HEAD_DIM)
SWA_Q_SCALE = SWA_SCALE * LOG2_E
SWA_V_ROWS = SWA_HEAD_DIM + BF16_SUBLANES
N_BUCKETS = 32
MAX_DISTANCE = 128

DEPTH = 1
ALPHA = (2.0 * DEPTH) ** 0.25

IN_SPLITS = (Q_LORA, KV_LORA, MLA_ROPE, D_MLA, D_SWA,
             SWA_KV_HEADS * SWA_HEAD_DIM, SWA_KV_HEADS * SWA_HEAD_DIM, D_SWA)
IN_OFFSETS = tuple(int(o) for o in np.cumsum((0,) + IN_SPLITS))

LANE = 128
SUBLANES = 8
QK_PAD = LANE
VMEM_LIMIT_BYTES = 56 * 1024 * 1024

PROJ_TOKENS = 1024
PROJ_PARTS = 2
MLA_TQ = 512
MLA_STEP_TILES = 4
MLA_TK = 256
MLA_AHEAD = 2
MLA_REFRESH = 8
MLA_MAX_EXCESS = 64.0
SWA_TOKENS = 4096
SWA_AHEAD = 1
SWA_MAX_EXCESS = 64.0
OUT_TOKENS = 1024
OUT_ROWS = 256

TR_CQ = 0
TR_QS = TR_CQ + Q_LORA
TR_VS = TR_QS + D_SWA
TR_KR = TR_VS + SWA_KV_HEADS * SWA_HEAD_DIM
TR_ROWS = TR_KR + MLA_ROPE
TR_PAD = 1024
NAT_CKV = 0
NAT_KS = NAT_CKV + KV_LORA
NAT_COLS = NAT_KS + SWA_KV_HEADS * SWA_HEAD_DIM

_NT = (((1,), (1,)), ((), ()))


def _rsqrt_mean_sq(x, axis, eps):
    return lax.rsqrt(jnp.mean(x * x, axis=axis, keepdims=True) + eps)


def _proj_kernel(x_ref, wtr_ref, wnat_ref, gq_ref, gkv_ref, wuqt_ref, wuk_ref, wuvt_ref,
                 cos_t_ref, sin_t_ref,
                 qt_ref, k_ref, vt_ref, qst_ref, ks_ref, vst_ref,
                 tr_ref, nat_ref):
    n_tok = x_ref.shape[1] // PROJ_PARTS

    def project(p):
        tok = slice(p * n_tok, (p + 1) * n_tok)
        xb = x_ref[0, tok, :].astype(jnp.bfloat16)
        tr_ref[:, tok] = jnp.dot(xb, wtr_ref[...], preferred_element_type=jnp.float32).T
        nat_ref[tok, :] = jnp.dot(xb, wnat_ref[...], preferred_element_type=jnp.float32)

    project(0)
    for p in range(PROJ_PARTS):
        if p + 1 < PROJ_PARTS:
            project(p + 1)
        _proj_finish(p, n_tok, tr_ref, nat_ref, gq_ref, gkv_ref, wuqt_ref, wuk_ref, wuvt_ref,
                     cos_t_ref, sin_t_ref,
                     qt_ref, k_ref, vt_ref, qst_ref, ks_ref, vst_ref)


def _proj_finish(p, n_tok, tr_ref, nat_ref, gq_ref, gkv_ref, wuqt_ref, wuk_ref, wuvt_ref,
                 cos_t_ref, sin_t_ref,
                 qt_ref, k_ref, vt_ref, qst_ref, ks_ref, vst_ref):
    tok = slice(p * n_tok, (p + 1) * n_tok)
    half = MLA_ROPE // 2

    cq = tr_ref[TR_CQ:TR_CQ + Q_LORA, tok]
    cqn = (cq * _rsqrt_mean_sq(cq, 0, 1e-6)) * gq_ref[...]
    q_t = jnp.dot(wuqt_ref[...], cqn.astype(jnp.bfloat16),
                  preferred_element_type=jnp.float32)
    cos_t = cos_t_ref[:, tok]
    sin_t = sin_t_ref[:, tok]
    zeros_pad = jnp.zeros((QK_PAD - MLA_QK, n_tok), jnp.bfloat16)
    for h in range(MLA_HEADS):
        base = h * MLA_QK
        nope = q_t[base:base + MLA_NOPE]
        r1 = q_t[base + MLA_NOPE:base + MLA_NOPE + half]
        r2 = q_t[base + MLA_NOPE + half:base + MLA_QK]
        qt_ref[0, h, 0:MLA_NOPE, tok] = (nope * MLA_Q_SCALE).astype(jnp.bfloat16)
        qt_ref[0, h, MLA_NOPE:MLA_NOPE + half, tok] = (
            (r1 * cos_t - r2 * sin_t) * MLA_Q_SCALE).astype(jnp.bfloat16)
        qt_ref[0, h, MLA_NOPE + half:MLA_QK, tok] = (
            (r2 * cos_t + r1 * sin_t) * MLA_Q_SCALE).astype(jnp.bfloat16)
        qt_ref[0, h, MLA_QK:QK_PAD, tok] = zeros_pad

    ckv = nat_ref[tok, NAT_CKV:NAT_CKV + KV_LORA]
    kvn = ((ckv * _rsqrt_mean_sq(ckv, 1, 1e-6)) * gkv_ref[...]).astype(jnp.bfloat16)
    k_nope = jnp.dot(kvn, wuk_ref[...], preferred_element_type=jnp.float32)
    v_t = lax.dot_general(wuvt_ref[...], kvn, _NT, preferred_element_type=jnp.float32)
    kr1 = tr_ref[TR_KR:TR_KR + half, tok]
    kr2 = tr_ref[TR_KR + half:TR_KR + MLA_ROPE, tok]
    k_rope = jnp.concatenate(
        [jnp.zeros((MLA_NOPE, n_tok), jnp.float32),
         kr1 * cos_t - kr2 * sin_t,
         kr2 * cos_t + kr1 * sin_t,
         jnp.zeros((QK_PAD - MLA_QK, n_tok), jnp.float32)], axis=0).T
    lane = lax.broadcasted_iota(jnp.int32, k_rope.shape, 1)
    row_id = lax.broadcasted_iota(jnp.int32, (BF16_SUBLANES, n_tok), 0)
    ones_row = jnp.where(row_id == 0, 1.0, 0.0).astype(jnp.bfloat16)
    for h in range(MLA_HEADS):
        pair = k_nope[:, (h // 2) * LANE:(h // 2 + 1) * LANE]
        if h % 2:
            pair = pltpu.roll(pair, MLA_NOPE, axis=1)
        k_ref[0, h, tok, :] = jnp.where(lane < MLA_NOPE, pair, k_rope).astype(jnp.bfloat16)
        vt_ref[0, h, 0:MLA_V, tok] = v_t[h * MLA_V:(h + 1) * MLA_V].astype(jnp.bfloat16)
        vt_ref[0, h, MLA_V:MLA_V_ROWS, tok] = ones_row

    n_blocks = n_tok // BLOCK
    ks_ref[0, tok, :] = nat_ref[tok, NAT_KS:NAT_KS + SWA_KV_HEADS * SWA_HEAD_DIM].astype(jnp.bfloat16)
    for g in range(SWA_KV_HEADS):
        vst_ref[0, g, 0:SWA_HEAD_DIM, tok] = tr_ref[TR_VS + g * SWA_HEAD_DIM:
                                                    TR_VS + (g + 1) * SWA_HEAD_DIM, tok].astype(jnp.bfloat16)
        vst_ref[0, g, SWA_HEAD_DIM:SWA_V_ROWS, tok] = ones_row
        for hh in range(SWA_GROUP):
            row = TR_QS + (g * SWA_GROUP + hh) * SWA_HEAD_DIM
            q_h = (tr_ref[row:row + SWA_HEAD_DIM, tok] * SWA_Q_SCALE).astype(jnp.bfloat16)
            for nb in range(n_blocks):
                col = ((p * n_blocks + nb) * SWA_GROUP + hh) * BLOCK
                qst_ref[0, g, :, col:col + BLOCK] = q_h[:, nb * BLOCK:(nb + 1) * BLOCK]


def _project(x, w_tr, w_nat, g_q, g_kv, w_uq_t, w_uk, w_uv_t, cos_t, sin_t):
    B, S, _ = x.shape
    T = PROJ_TOKENS
    grid = (B, S // T)
    const = lambda *shape: pl.BlockSpec(shape, lambda b, t: (0,) * len(shape))
    bf = jnp.bfloat16
    out_shape = (
        jax.ShapeDtypeStruct((B, MLA_HEADS, QK_PAD, S), bf),
        jax.ShapeDtypeStruct((B, MLA_HEADS, S, QK_PAD), bf),
        jax.ShapeDtypeStruct((B, MLA_HEADS, MLA_V_ROWS, S), bf),
        jax.ShapeDtypeStruct((B, SWA_KV_HEADS, SWA_HEAD_DIM, SWA_GROUP * S), bf),
        jax.ShapeDtypeStruct((B, S, SWA_KV_HEADS * SWA_HEAD_DIM), bf),
        jax.ShapeDtypeStruct((B, SWA_KV_HEADS, SWA_V_ROWS, S), bf),
    )
    out_specs = (
        pl.BlockSpec((1, MLA_HEADS, QK_PAD, T), lambda b, t: (b, 0, 0, t)),
        pl.BlockSpec((1, MLA_HEADS, T, QK_PAD), lambda b, t: (b, 0, t, 0)),
        pl.BlockSpec((1, MLA_HEADS, MLA_V_ROWS, T), lambda b, t: (b, 0, 0, t)),
        pl.BlockSpec((1, SWA_KV_HEADS, SWA_HEAD_DIM, SWA_GROUP * T), lambda b, t: (b, 0, 0, t)),
        pl.BlockSpec((1, T, SWA_KV_HEADS * SWA_HEAD_DIM), lambda b, t: (b, t, 0)),
        pl.BlockSpec((1, SWA_KV_HEADS, SWA_V_ROWS, T), lambda b, t: (b, 0, 0, t)),
    )
    in_specs = [
        pl.BlockSpec((1, T, D_MODEL), lambda b, t: (b, t, 0)),
        const(D_MODEL, TR_PAD),
        const(D_MODEL, NAT_COLS),
        const(Q_LORA, 1),
        const(1, KV_LORA),
        const(MLA_HEADS * MLA_QK, Q_LORA),
        const(KV_LORA, MLA_HEADS * MLA_NOPE),
        const(D_MLA, KV_LORA),
        pl.BlockSpec((MLA_ROPE // 2, T), lambda b, t: (0, t)),
        pl.BlockSpec((MLA_ROPE // 2, T), lambda b, t: (0, t)),
    ]
    return pl.pallas_call(
        _proj_kernel,
        grid=grid,
        in_specs=in_specs,
        out_specs=out_specs,
        out_shape=out_shape,
        scratch_shapes=[pltpu.VMEM((TR_PAD, T), jnp.float32),
                        pltpu.VMEM((T, NAT_COLS), jnp.float32)],
        compiler_params=pltpu.CompilerParams(
            dimension_semantics=("arbitrary", "arbitrary"),
            vmem_limit_bytes=VMEM_LIMIT_BYTES),
        name="proj",
    )(x, w_tr, w_nat, g_q, g_kv, w_uq_t, w_uk, w_uv_t, cos_t, sin_t)


def _mla_kernel(qt_ref, k_ref, vt_ref, o_ref, m_ref, acc_ref):
    n_chunks = k_ref.shape[2] // MLA_TK
    n_tiles = qt_ref.shape[3] // MLA_TQ

    def q_tile(t):
        return qt_ref[0, 0, :, t * MLA_TQ:(t + 1) * MLA_TQ]

    def chunk_slice(c):
        start = c * MLA_TK
        return pl.ds(start if isinstance(c, int) else pl.multiple_of(start, MLA_TK), MLA_TK)

    def key_chunk(c):
        return k_ref[0, 0, chunk_slice(c), :]

    def value_chunk(c):
        return vt_ref[0, 0, :, chunk_slice(c)]

    items = [(t, c) for t in range(n_tiles) for c in range(n_chunks)]
    m_use = [None] * n_tiles
    seen = [None] * n_tiles
    acc = [None] * n_tiles
    den = [None] * n_tiles
    excess = None

    def issue_scores(i):
        t, c = items[i]
        if c == 0:
            s0 = jnp.dot(k_ref[0, 0, 0:BF16_SUBLANES, :], q_tile(t),
                         preferred_element_type=jnp.float32)
            m_use[t] = jnp.max(s0, axis=0, keepdims=True)
        return jnp.dot(key_chunk(c), q_tile(t), preferred_element_type=jnp.float32)

    pending = [issue_scores(i) for i in range(MLA_AHEAD)]
    for i, (t, c) in enumerate(items):
        if i + MLA_AHEAD < len(items):
            pending.append(issue_scores(i + MLA_AHEAD))
        s = pending.pop(0)
        p = jnp.exp2(s - m_use[t])
        psum = jnp.sum(p.reshape(MLA_TK // SUBLANES, SUBLANES, MLA_TQ), axis=0)
        den[t] = psum if c == 0 else den[t] + psum
        p = p.astype(jnp.bfloat16)
        pv = jnp.dot(vt_ref[0, 0, 0:MLA_V, chunk_slice(c)], p,
                     preferred_element_type=jnp.float32)
        acc[t] = pv if c == 0 else acc[t] + pv
        pmax = jnp.max(p.reshape(MLA_TK // BF16_SUBLANES, BF16_SUBLANES, MLA_TQ), axis=0)
        seen[t] = pmax if seen[t] is None else jnp.maximum(seen[t], pmax)
        last = c + 1 == n_chunks
        if last or c == 0 or (c + 1) % MLA_REFRESH == 0:
            excess = seen[t] if excess is None else jnp.maximum(excess, seen[t])
            if last:
                acc_ref[t, 0:MLA_V, :] = acc[t]
                acc_ref[t, MLA_V:MLA_V + 1, :] = jnp.sum(den[t], axis=0, keepdims=True)
            else:
                top = jnp.max(seen[t].astype(jnp.float32), axis=0, keepdims=True)
                rise = jnp.maximum(jnp.log2(top), 0.0)
                alpha = jnp.exp2(-rise)
                acc[t] = acc[t] * alpha
                den[t] = den[t] * alpha
                m_use[t] = m_use[t] + rise
                seen[t] = None

    @pl.when(jnp.max(excess.astype(jnp.float32)) > 2.0 ** MLA_MAX_EXCESS)
    def _():
        for t in range(n_tiles):
            m_ref[...] = jnp.full(m_ref.shape, -jnp.inf, jnp.float32)
            acc_ref[t] = jnp.zeros(acc_ref.shape[1:], jnp.float32)

            def chunk(c, carry, t=t):
                s = jnp.dot(key_chunk(c), q_tile(t), preferred_element_type=jnp.float32)
                m_prev = m_ref[...]
                m_new = jnp.maximum(m_prev, jnp.max(s, axis=0, keepdims=True))
                p = jnp.exp2(s - m_new).astype(jnp.bfloat16)
                acc_ref[t] = jnp.exp2(m_prev - m_new) * acc_ref[t] + jnp.dot(
                    value_chunk(c), p, preferred_element_type=jnp.float32)
                m_ref[...] = m_new
                return carry

            lax.fori_loop(0, n_chunks, chunk, 0)

    for t in range(n_tiles):
        cols = slice(t * MLA_TQ, (t + 1) * MLA_TQ)
        o = acc_ref[t, 0:MLA_V, :] / acc_ref[t, MLA_V:MLA_V + 1, :]
        o_ref[0, :, cols] = o.astype(o_ref.dtype)


def _mla_attention(q_t, k, v_t):
    B, H, _, S = q_t.shape
    TQ = MLA_TQ
    step_q = MLA_STEP_TILES * TQ
    grid = (B, H, S // step_q)
    return pl.pallas_call(
        _mla_kernel,
        grid=grid,
        in_specs=[
            pl.BlockSpec((1, 1, QK_PAD, step_q), lambda b, h, i: (b, h, 0, i)),
            pl.BlockSpec((1, 1, S, QK_PAD), lambda b, h, i: (b, h, 0, 0)),
            pl.BlockSpec((1, 1, MLA_V_ROWS, S), lambda b, h, i: (b, h, 0, 0)),
        ],
        out_specs=pl.BlockSpec((1, MLA_V, step_q), lambda b, h, i: (b, h, i)),
        out_shape=jax.ShapeDtypeStruct((B, D_MLA, S), jnp.bfloat16),
        scratch_shapes=[pltpu.VMEM((1, TQ), jnp.float32),
                        pltpu.VMEM((MLA_STEP_TILES, MLA_V_ROWS, TQ), jnp.float32)],
        compiler_params=pltpu.CompilerParams(
            dimension_semantics=("arbitrary", "arbitrary", "arbitrary"),
            vmem_limit_bytes=VMEM_LIMIT_BYTES),
        name="mla",
    )(q_t, k, v_t)


def _bias_kernel(rel_ref, sink_ref, bucket_ref, o_ref):
    g = pl.program_id(0)
    for part in range(3):
        bucket = bucket_ref[part]
        for hh in range(SWA_GROUP):
            head = g * SWA_GROUP + hh
            tile = jnp.full((BLOCK, BLOCK), -jnp.inf, jnp.float32)
            for b in range(N_BUCKETS):
                tile = jnp.where(bucket == b, (rel_ref[b, head] - sink_ref[head]) * LOG2_E, tile)
            o_ref[0, part, :, hh * BLOCK:(hh + 1) * BLOCK] = tile


def _bias_table(rel_bias, sink, bucket):
    return pl.pallas_call(
        _bias_kernel,
        grid=(SWA_KV_HEADS,),
        in_specs=[pl.BlockSpec(memory_space=pltpu.SMEM),
                  pl.BlockSpec(memory_space=pltpu.SMEM),
                  pl.BlockSpec((3, BLOCK, BLOCK), lambda g: (0, 0, 0))],
        out_specs=pl.BlockSpec((1, 3, BLOCK, SWA_GROUP * BLOCK), lambda g: (g, 0, 0, 0)),
        out_shape=jax.ShapeDtypeStruct((SWA_KV_HEADS, 3, BLOCK, SWA_GROUP * BLOCK), jnp.float32),
        compiler_params=pltpu.CompilerParams(dimension_semantics=("arbitrary",)),
        name="bias",
    )(rel_bias, sink, bucket)


def _swa_kernel(qst_ref, ks_ref, vst_ref, bias_ref, o_ref):
    n_total = ks_ref.shape[1] // BLOCK
    kv_head = pl.program_id(1)
    n_local = o_ref.shape[2] // BLOCK
    step = pl.program_id(2)
    neg_inf = jnp.float32(-jnp.inf)

    def band_starts(j):
        n = step * n_local + j
        return tuple(pl.multiple_of(b * BLOCK, BLOCK)
                     for b in (jnp.maximum(n - 1, 0), n, jnp.minimum(n + 1, n_total - 1)))

    def lane_block(j, width):
        start = j * width
        return pl.ds(start if isinstance(j, int) else pl.multiple_of(start, width), width)

    def band_scores(j):
        q_t = qst_ref[0, 0, :, lane_block(j, SWA_GROUP * BLOCK)]
        zeros = jnp.zeros_like(q_t)
        q_t = jnp.where(kv_head == 0, jnp.concatenate([q_t, zeros], axis=0),
                        jnp.concatenate([zeros, q_t], axis=0))
        n = step * n_local + j
        penalties = [None, None, None]
        if not isinstance(j, int) or j == 0:
            penalties[0] = jnp.where(n > 0, 0.0, neg_inf)
        if not isinstance(j, int) or j == n_local - 1:
            penalties[2] = jnp.where(n < n_total - 1, 0.0, neg_inf)
        scores = []
        for part, start in enumerate(band_starts(j)):
            k = ks_ref[0, pl.ds(start, BLOCK), :]
            s = jnp.dot(k, q_t, preferred_element_type=jnp.float32) + bias_ref[0, part]
            if penalties[part] is not None:
                s = s + penalties[part]
            scores.append(s)
        return scores

    def weighted_values(j, weights, n_rows):
        acc = None
        for part, start in enumerate(band_starts(j)):
            v_t = vst_ref[0, 0, 0:n_rows, pl.ds(start, BLOCK)]
            pv = jnp.dot(v_t, weights[part], preferred_element_type=jnp.float32)
            acc = pv if acc is None else acc + pv
        return acc

    def write_block(j, values, denom):
        o = values / denom
        cols = lane_block(j, BLOCK)
        for hh in range(SWA_GROUP):
            rows = slice(hh * SWA_HEAD_DIM, (hh + 1) * SWA_HEAD_DIM)
            o_ref[0, rows, cols] = o[:, hh * BLOCK:(hh + 1) * BLOCK].astype(o_ref.dtype)

    excess = None
    pending = [band_scores(j) for j in range(SWA_AHEAD)]
    for j in range(n_local):
        if j + SWA_AHEAD < n_local:
            pending.append(band_scores(j + SWA_AHEAD))
        weights, den = [], None
        for s in pending.pop(0):
            w = jnp.exp2(s)
            wsum = jnp.sum(w.reshape(BLOCK // SUBLANES, SUBLANES, w.shape[1]), axis=0)
            den = wsum if den is None else den + wsum
            w = w.astype(jnp.bfloat16)
            wmax = jnp.max(w.reshape(BLOCK // BF16_SUBLANES, BF16_SUBLANES, w.shape[1]), axis=0)
            excess = wmax if excess is None else jnp.maximum(excess, wmax)
            weights.append(w)
        write_block(j, weighted_values(j, weights, SWA_HEAD_DIM),
                    jnp.sum(den, axis=0, keepdims=True) + 1.0)

    @pl.when(jnp.max(excess.astype(jnp.float32)) > 2.0 ** SWA_MAX_EXCESS)
    def _():
        def exact_block(j, carry):
            scores = band_scores(j)
            m = jnp.zeros((1, scores[0].shape[1]), jnp.float32)
            for s in scores:
                m = jnp.maximum(m, jnp.max(s, axis=0, keepdims=True))
            acc = weighted_values(j, [jnp.exp2(s - m).astype(jnp.bfloat16) for s in scores],
                                  SWA_V_ROWS)
            write_block(j, acc[0:SWA_HEAD_DIM], acc[SWA_HEAD_DIM:SWA_HEAD_DIM + 1] + jnp.exp2(-m))
            return carry

        lax.fori_loop(0, n_local, exact_block, 0)


def _swa_attention(qs_t, ks, vs_t, bias_t):
    B, G, _, S = vs_t.shape
    T = SWA_TOKENS
    grid = (B, G, S // T)
    rows = SWA_GROUP * SWA_HEAD_DIM
    return pl.pallas_call(
        _swa_kernel,
        grid=grid,
        in_specs=[
            pl.BlockSpec((1, 1, SWA_HEAD_DIM, SWA_GROUP * T), lambda b, g, t: (b, g, 0, t)),
            pl.BlockSpec((1, S, SWA_KV_HEADS * SWA_HEAD_DIM), lambda b, g, t: (b, 0, 0)),
            pl.BlockSpec((1, 1, SWA_V_ROWS, S), lambda b, g, t: (b, g, 0, 0)),
            pl.BlockSpec((1, 3, BLOCK, SWA_GROUP * BLOCK), lambda b, g, t: (g, 0, 0, 0)),
        ],
        out_specs=pl.BlockSpec((1, rows, T), lambda b, g, t: (b, g, t)),
        out_shape=jax.ShapeDtypeStruct((B, D_SWA, S), jnp.bfloat16),
        compiler_params=pltpu.CompilerParams(
            dimension_semantics=("arbitrary", "arbitrary", "arbitrary"),
            vmem_limit_bytes=VMEM_LIMIT_BYTES),
        name="swa",
    )(qs_t, ks, vs_t, bias_t)


def _out_kernel(x_ref, oa_ref, ob_ref, wga_ref, wgb_ref, wa_ref, wb_ref, g_ref, b_ref, y_ref):
    n_chunks = x_ref.shape[1] // OUT_ROWS

    def gate_paths(c):
        xb = x_ref[0, c * OUT_ROWS:(c + 1) * OUT_ROWS, :].astype(jnp.bfloat16)
        return [jnp.dot(xb, wg_ref[...], preferred_element_type=jnp.float32)
                for wg_ref in (wga_ref, wgb_ref)]

    def project(c, gates):
        toks = slice(c * OUT_ROWS, (c + 1) * OUT_ROWS)
        out = None
        for o_ref_, gate, w_ref in ((oa_ref, gates[0], wa_ref), (ob_ref, gates[1], wb_ref)):
            o_nat = o_ref_[0, :, toks].astype(jnp.float32).T
            mixed = (o_nat * jax.nn.silu(gate)).astype(jnp.bfloat16)
            part = jnp.dot(mixed, w_ref[...], preferred_element_type=jnp.float32)
            out = part if out is None else out + part
        return out

    def residual_norm(c, out):
        rows = slice(c * OUT_ROWS, (c + 1) * OUT_ROWS)
        h = ALPHA * x_ref[0, rows, :] + out
        mu = jnp.mean(h, axis=-1, keepdims=True)
        var = jnp.mean(jnp.square(h - mu), axis=-1, keepdims=True)
        y_ref[0, rows, :] = ((h - mu) * lax.rsqrt(var + 1e-5)) * g_ref[...] + b_ref[...]

    gates_next = gate_paths(0)
    out_prev = None
    for c in range(n_chunks):
        gates = gates_next
        if c + 1 < n_chunks:
            gates_next = gate_paths(c + 1)
        out = project(c, gates)
        if out_prev is not None:
            residual_norm(c - 1, out_prev)
        out_prev = out
    residual_norm(n_chunks - 1, out_prev)


def _out_project(x, o_a_t, o_b_t, w_gate_a, w_gate_b, w_a, w_b, ln_g, ln_b):
    B, S, D = x.shape
    T = OUT_TOKENS
    grid = (B, S // T)
    const = lambda *shape: pl.BlockSpec(shape, lambda b, t: (0,) * len(shape))
    return pl.pallas_call(
        _out_kernel,
        grid=grid,
        in_specs=[
            pl.BlockSpec((1, T, D), lambda b, t: (b, t, 0)),
            pl.BlockSpec((1, D_MLA, T), lambda b, t: (b, 0, t)),
            pl.BlockSpec((1, D_SWA, T), lambda b, t: (b, 0, t)),
            const(D, D_MLA),
            const(D, D_SWA),
            const(D_MLA, D),
            const(D_SWA, D),
            const(1, D),
            const(1, D),
        ],
        out_specs=pl.BlockSpec((1, T, D), lambda b, t: (b, t, 0)),
        out_shape=jax.ShapeDtypeStruct((B, S, D), x.dtype),
        compiler_params=pltpu.CompilerParams(
            dimension_semantics=("arbitrary", "arbitrary"),
            vmem_limit_bytes=VMEM_LIMIT_BYTES),
        name="outproj",
    )(x, o_a_t, o_b_t, w_gate_a, w_gate_b, w_a, w_b, ln_g, ln_b)


def _t5_bucket(rel):
    half = N_BUCKETS // 2
    ret = np.where(rel > 0, half, 0)
    n = np.abs(rel)
    max_exact = half // 2
    large = max_exact + (np.log(np.maximum(n, 1).astype(np.float32) / max_exact)
                         / np.log(MAX_DISTANCE / max_exact) * (half - max_exact)).astype(np.int32)
    large = np.minimum(large, half - 1)
    return (ret + np.where(n < max_exact, n, large)).astype(np.int32)


def _band_geometry():
    q_loc = np.arange(BLOCK)
    k_loc = np.arange(3 * BLOCK) - BLOCK
    rel = k_loc[:, None] - q_loc[None, :]
    return _t5_bucket(rel), np.abs(rel) <= WINDOW


def _col(w_in, i):
    return w_in[:, IN_OFFSETS[i]:IN_OFFSETS[i + 1]]


def kernel(x, w_in, g_q, g_kv, w_uq, w_ukv, sink, rel_bias, w_out, ln_g, ln_b):
    B, S, _ = x.shape
    bf = jnp.bfloat16
    f32 = jnp.float32

    pos = jnp.arange(S, dtype=f32)
    inv_freq = ROPE_BASE ** (-jnp.arange(0, MLA_ROPE, 2, dtype=f32) / MLA_ROPE)
    ang_t = inv_freq[:, None] * pos[None, :]
    cos_t, sin_t = jnp.cos(ang_t), jnp.sin(ang_t)

    c_q, c_kv, k_rope, gate_a, q_s, k_s, v_s, gate_b = (_col(w_in, i) for i in range(8))
    w_tr = jnp.concatenate([c_q, q_s, v_s, k_rope,
                            jnp.zeros((D_MODEL, TR_PAD - TR_ROWS), w_in.dtype)], axis=1).astype(bf)
    w_nat = jnp.concatenate([c_kv, k_s], axis=1).astype(bf)
    w_ukv3 = w_ukv.reshape(KV_LORA, MLA_HEADS, MLA_NOPE + MLA_V)
    w_uk = w_ukv3[:, :, :MLA_NOPE].reshape(KV_LORA, MLA_HEADS * MLA_NOPE).astype(bf)
    w_uv_t = w_ukv3[:, :, MLA_NOPE:].reshape(KV_LORA, D_MLA).T.astype(bf)
    w_uq_t = w_uq.T.astype(bf)

    q_t, k, v_t, qs_t, ks, vs_t = _project(
        x, w_tr, w_nat, g_q.reshape(Q_LORA, 1), g_kv.reshape(1, KV_LORA),
        w_uq_t, w_uk, w_uv_t, cos_t, sin_t)

    o_a_t = _mla_attention(q_t, k, v_t)

    bucket, band = _band_geometry()
    bucket = jnp.asarray(np.where(band, bucket, -1).reshape(3, BLOCK, BLOCK), jnp.int32)
    bias_t = _bias_table(rel_bias.astype(f32), sink.astype(f32), bucket)

    o_b_t = _swa_attention(qs_t, ks, vs_t, bias_t)

    w_out_bf = w_out.astype(bf)
    return _out_project(x, o_a_t, o_b_t, gate_a.astype(bf), gate_b.astype(bf),
                        w_out_bf[:D_MLA], w_out_bf[D_MLA:],
                        ln_g.reshape(1, D_MODEL), ln_b.reshape(1, D_MODEL))
```
